```python
import jax, jax.numpy as jnp
from jax import lax
import numpy as np

D_MODEL = 1024
BATCH = 1
SEQ = 16384
DEPTH = 4
DEC_BATCH = 16
DEC_SEQ = 16
PAST_LEN = 4096

CHUNK = 64
HEAD_DIM = 64
N_HEADS = D_MODEL // HEAD_DIM
N_HEADS_A = N_HEADS // 2
N_HEADS_B = N_HEADS - N_HEADS_A
N_HEADS_C = N_HEADS
BAND_CHUNKS = 8
BAND = BAND_CHUNKS * CHUNK
REL_CLIP = 128
D_FF = 4 * D_MODEL
D_PLE = 256
Q_BLOCK = 128
N_EVEN = (DEPTH + 1) // 2
N_ODD = DEPTH // 2
EPS = 1e-6
SCALE = HEAD_DIM ** -0.5

kernel_name = "streaming_hybrid_stickbreak_band_fox_step"


def rms_norm(x, g):
    xf = x.astype(jnp.float32)
    y = xf * lax.rsqrt(jnp.mean(xf * xf, axis=-1, keepdims=True) + EPS)
    return (y * g.astype(jnp.float32)).astype(x.dtype)


def sweep_query_blocks(fn, q, q_pos, *q_side):
    n = q.shape[1]
    if n <= Q_BLOCK:
        return fn(q, q_pos, *q_side)
    nb = n // Q_BLOCK

    def split(a):
        return jnp.moveaxis(a.reshape(a.shape[0], nb, Q_BLOCK, *a.shape[2:]), 1, 0)

    out = lax.map(lambda args: fn(*args), (split(q), q_pos.reshape(nb, Q_BLOCK), *[split(s) for s in q_side]))
    out = jnp.moveaxis(out, 0, 1)
    return out.reshape(q.shape[0], n, *out.shape[3:])


def stick_breaking_block(q, k, v, q_pos, k_pos):
    z = jnp.einsum('bqhd,bkhd->bhqk', q, k).astype(jnp.float32) * SCALE
    visible = k_pos[None, :] < q_pos[:, None]
    log_keep = jnp.where(visible, -jax.nn.softplus(z), 0.0)
    later = lax.cumsum(log_keep, axis=3, reverse=True) - log_keep
    w = jnp.where(visible, jnp.exp(jax.nn.log_sigmoid(z) + later), 0.0)
    return jnp.einsum('bhqk,bkhd->bqhd', w.astype(v.dtype), v)


def band_block(q, k, v, q_pos, k_pos, rel_table):
    logits = jnp.einsum('bqhd,bkhd->bhqk', q, k).astype(jnp.float32) * SCALE
    q_chunk = q_pos // CHUNK
    k_chunk = k_pos // CHUNK
    visible = ((k_pos[None, :] >= 0) & (k_chunk[None, :] <= q_chunk[:, None])
               & (k_chunk[None, :] >= q_chunk[:, None] - BAND_CHUNKS))
    rel = jnp.clip(q_pos[:, None] - k_pos[None, :], -REL_CLIP, REL_CLIP) + REL_CLIP
    logits = logits + rel_table.astype(jnp.float32)[:, rel][None]
    w = jax.nn.softmax(jnp.where(visible, logits, -jnp.inf), axis=-1)
    return jnp.einsum('bhqk,bkhd->bqhd', w.astype(v.dtype), v)


def band_mix_prompt(q, k, v, rel_table):
    B, n, H, d = q.shape
    pad = ((0, 0), (BAND, 0), (0, 0), (0, 0))
    k_pad, v_pad = jnp.pad(k, pad), jnp.pad(v, pad)

    def one_chunk(c):
        start = c * CHUNK
        q_c = lax.dynamic_slice_in_dim(q, start, CHUNK, axis=1)
        k_c = lax.dynamic_slice_in_dim(k_pad, start, BAND + CHUNK, axis=1)
        v_c = lax.dynamic_slice_in_dim(v_pad, start, BAND + CHUNK, axis=1)
        q_pos = start + jnp.arange(CHUNK)
        k_pos = start - BAND + jnp.arange(BAND + CHUNK)
        return band_block(q_c, k_c, v_c, q_pos, k_pos, rel_table)

    out = lax.map(one_chunk, jnp.arange(n // CHUNK))
    return jnp.moveaxis(out, 0, 1).reshape(B, n, H, d)


def forgetting_block(q, dq, k, v, dk, q_pos, k_pos):
    logits = jnp.einsum('bqhd,bkhd->bhqk', q, k).astype(jnp.float32) * SCALE
    logits = logits + jnp.moveaxis(dq, 2, 1)[..., :, None] - jnp.moveaxis(dk, 2, 1)[..., None, :]
    visible = k_pos[None, :] <= q_pos[:, None]
    w = jax.nn.softmax(jnp.where(visible, logits, -jnp.inf), axis=-1)
    return jnp.einsum('bhqk,bkhd->bqhd', w.astype(v.dtype), v)


def even_mixer(xn, w_in, g_qb, g_kb, rel_table, past):
    B, n, _ = xn.shape
    wa, wb = N_HEADS_A * HEAD_DIM, N_HEADS_B * HEAD_DIM
    proj = jnp.einsum('bsd,de->bse', xn, w_in)
    qa, ka, va, qb, kb, vb = jnp.split(proj, [wa, 2 * wa, 3 * wa, 3 * wa + wb, 3 * wa + 2 * wb], axis=-1)
    qa, ka, va = (t.reshape(B, n, N_HEADS_A, HEAD_DIM) for t in (qa, ka, va))
    qb = rms_norm(qb.reshape(B, n, N_HEADS_B, HEAD_DIM), g_qb)
    kb = rms_norm(kb.reshape(B, n, N_HEADS_B, HEAD_DIM), g_kb)
    vb = vb.reshape(B, n, N_HEADS_B, HEAD_DIM)
    if past is None:
        P = 0
        ka_all, va_all = ka, va
        out_b = band_mix_prompt(qb, kb, vb, rel_table)
        keep = min(BAND, n)
        kb_state, vb_state = kb[:, n - keep:], vb[:, n - keep:]
    else:
        pa_k, pa_v, pb_k, pb_v = past
        P, L = pa_k.shape[1], pb_k.shape[1]
        ka_all = jnp.concatenate([pa_k, ka], axis=1)
        va_all = jnp.concatenate([pa_v, va], axis=1)
        kb_all = jnp.concatenate([pb_k, kb], axis=1)
        vb_all = jnp.concatenate([pb_v, vb], axis=1)
        out_b = band_block(qb, kb_all, vb_all, P + jnp.arange(n), P - L + jnp.arange(L + n), rel_table)
        kb_state, vb_state = kb_all[:, n:], vb_all[:, n:]
    q_pos = P + jnp.arange(n)
    k_pos = jnp.arange(P + n)
    out_a = sweep_query_blocks(lambda q_blk, qp: stick_breaking_block(q_blk, ka_all, va_all, qp, k_pos), qa, q_pos)
    mix = jnp.concatenate([out_a, out_b], axis=2).reshape(B, n, N_HEADS * HEAD_DIM)
    return mix, (ka, va, kb_state, vb_state)


def odd_mixer(xn, w_in, b_forget, g_qc, g_kc, past):
    B, n, _ = xn.shape
    w = N_HEADS_C * HEAD_DIM
    proj = jnp.einsum('bsd,de->bse', xn, w_in)
    q, k, v, f = jnp.split(proj, [w, 2 * w, 3 * w], axis=-1)
    q = rms_norm(q.reshape(B, n, N_HEADS_C, HEAD_DIM), g_qc)
    k = rms_norm(k.reshape(B, n, N_HEADS_C, HEAD_DIM), g_kc)
    v = v.reshape(B, n, N_HEADS_C, HEAD_DIM)
    log_f = jax.nn.log_sigmoid(f.astype(jnp.float32) + b_forget.astype(jnp.float32))
    if past is None:
        P = 0
        k_all, v_all, logf_all = k, v, log_f
    else:
        pk, pv, pf = past
        P = pk.shape[1]
        k_all = jnp.concatenate([pk, k], axis=1)
        v_all = jnp.concatenate([pv, v], axis=1)
        logf_all = jnp.concatenate([pf.astype(jnp.float32), log_f], axis=1)
    cum = lax.cumsum(logf_all, axis=1)
    q_pos = P + jnp.arange(n)
    k_pos = jnp.arange(P + n)
    out = sweep_query_blocks(
        lambda q_blk, qp, dq_blk: forgetting_block(q_blk, dq_blk, k_all, v_all, cum, qp, k_pos),
        q, q_pos, cum[:, P:])
    return out.reshape(B, n, w), (k, v, log_f)


def run_trunk(x, p, cache, prm):
    h = x
    a_k, a_v, b_k, b_v, c_k, c_v, c_f = [], [], [], [], [], [], []
    for i in range(DEPTH):
        j = i // 2
        xn = rms_norm(h, prm['g_mix'][i])
        if i % 2 == 0:
            past = None if cache is None else (cache[0][j], cache[1][j], cache[2][j], cache[3][j])
            mix, (ka, va, kb, vb) = even_mixer(xn, prm['w_in_even'][j], prm['g_qb'][j], prm['g_kb'][j],
                                               prm['rel_bias'][j], past)
            a_k.append(ka); a_v.append(va); b_k.append(kb); b_v.append(vb)
        else:
            past = None if cache is None else (cache[4][j], cache[5][j], cache[6][j])
            mix, (kc, vc, fc) = odd_mixer(xn, prm['w_in_odd'][j], prm['b_forget'][j], prm['g_qc'][j],
                                          prm['g_kc'][j], past)
            c_k.append(kc); c_v.append(vc); c_f.append(fc)
        h = h + jnp.einsum('bse,ed->bsd', mix, prm['w_out'][i])
        xn = rms_norm(h, prm['g_mlp'][i])
        hid = jnp.square(jax.nn.relu(jnp.einsum('bsd,df->bsf', xn, prm['w_ff1'][i])))
        h = h + jnp.einsum('bsf,fd->bsd', hid, prm['w_ff2'][i])
        gate = jax.nn.sigmoid(jnp.einsum('bsd,de->bse', rms_norm(h, prm['g_ple'][i]), prm['w_ple_gate'][i]))
        h = h + gate * jnp.einsum('bsp,pd->bsd', p[i], prm['w_ple_proj'][i])
    states = (jnp.stack(a_k), jnp.stack(a_v), jnp.stack(b_k), jnp.stack(b_v),
              jnp.stack(c_k), jnp.stack(c_v), jnp.stack(c_f))
    return h, states


def setup_inputs(seed: int = 0) -> dict:
    key = jax.random.key(seed)
    ks = jax.random.split(key, 32)
    f32 = jnp.float32
    d = HEAD_DIM
    b_len = min(BAND, PAST_LEN)

    def nrm(k, shape, scale=1.0):
        return jax.random.normal(k, shape, f32) * scale

    return {
        'x_prompt': nrm(ks[0], (BATCH, SEQ, D_MODEL)),
        'x_sample': nrm(ks[1], (DEC_BATCH, DEC_SEQ, D_MODEL)),
        'cache_a_k': nrm(ks[2], (N_EVEN, DEC_BATCH, PAST_LEN, N_HEADS_A, d)),
        'cache_a_v': nrm(ks[3], (N_EVEN, DEC_BATCH, PAST_LEN, N_HEADS_A, d)),
        'cache_b_k': nrm(ks[4], (N_EVEN, DEC_BATCH, b_len, N_HEADS_B, d)),
        'cache_b_v': nrm(ks[5], (N_EVEN, DEC_BATCH, b_len, N_HEADS_B, d)),
        'cache_c_k': nrm(ks[6], (N_ODD, DEC_BATCH, PAST_LEN, N_HEADS_C, d)),
        'cache_c_v': nrm(ks[7], (N_ODD, DEC_BATCH, PAST_LEN, N_HEADS_C, d)),
        'cache_c_logf': jax.nn.log_sigmoid(3.0 + nrm(ks[8], (N_ODD, DEC_BATCH, PAST_LEN, N_HEADS_C), 1.5)),
        'p_prompt': nrm(ks[9], (DEPTH, BATCH, SEQ, D_PLE)),
        'p_sample': nrm(ks[10], (DEPTH, DEC_BATCH, DEC_SEQ, D_PLE)),
        'g_mix': 1.0 + nrm(ks[11], (DEPTH, D_MODEL), 0.02),
        'w_in_even': nrm(ks[12], (N_EVEN, D_MODEL, 3 * (N_HEADS_A + N_HEADS_B) * d), D_MODEL ** -0.5),
        'g_qb': 1.0 + nrm(ks[13], (N_EVEN, d), 0.02),
        'g_kb': 1.0 + nrm(ks[14], (N_EVEN, d), 0.02),
        'rel_bias': nrm(ks[15], (N_EVEN, N_HEADS_B, 2 * REL_CLIP + 1), 0.5),
        'w_in_odd': nrm(ks[16], (N_ODD, D_MODEL, 3 * N_HEADS_C * d + N_HEADS_C), D_MODEL ** -0.5),
        'b_forget': jax.random.uniform(ks[17], (N_ODD, N_HEADS_C), f32, 1.0, 6.0),
        'g_qc': 1.0 + nrm(ks[18], (N_ODD, d), 0.02),
        'g_kc': 1.0 + nrm(ks[19], (N_ODD, d), 0.02),
        'w_out': nrm(ks[20], (DEPTH, N_HEADS * d, D_MODEL), (N_HEADS * d) ** -0.5),
        'g_mlp': 1.0 + nrm(ks[21], (DEPTH, D_MODEL), 0.02),
        'w_ff1': nrm(ks[22], (DEPTH, D_MODEL, D_FF), D_MODEL ** -0.5),
        'w_ff2': nrm(ks[23], (DEPTH, D_FF, D_MODEL), D_FF ** -0.5),
        'g_ple': 1.0 + nrm(ks[24], (DEPTH, D_MODEL), 0.02),
        'w_ple_gate': nrm(ks[25], (DEPTH, D_MODEL, D_MODEL), D_MODEL ** -0.5),
        'w_ple_proj': nrm(ks[26], (DEPTH, D_PLE, D_MODEL), D_PLE ** -0.5),
    }


def reference(x_prompt, x_sample, cache_a_k, cache_a_v, cache_b_k, cache_b_v, cache_c_k, cache_c_v,
              cache_c_logf, p_prompt, p_sample, g_mix, w_in_even, g_qb, g_kb, rel_bias, w_in_odd,
              b_forget, g_qc, g_kc, w_out, g_mlp, w_ff1, w_ff2, g_ple, w_ple_gate, w_ple_proj):
    prm = {'g_mix': g_mix, 'w_in_even': w_in_even, 'g_qb': g_qb, 'g_kb': g_kb, 'rel_bias': rel_bias,
           'w_in_odd': w_in_odd, 'b_forget': b_forget, 'g_qc': g_qc, 'g_kc': g_kc, 'w_out': w_out,
           'g_mlp': g_mlp, 'w_ff1': w_ff1, 'w_ff2': w_ff2, 'g_ple': g_ple, 'w_ple_gate': w_ple_gate,
           'w_ple_proj': w_ple_proj}
    y_prompt, st_p = run_trunk(x_prompt, p_prompt, None, prm)
    cache = (cache_a_k, cache_a_v, cache_b_k, cache_b_v, cache_c_k, cache_c_v, cache_c_logf)
    y_sample, st_s = run_trunk(x_sample, p_sample, cache, prm)
    pa_k, pa_v, pb_k, pb_v, pc_k, pc_v, pc_f = st_p
    sa_k, sa_v, sb_k, sb_v, sc_k, sc_v, sc_f = st_s
    return (y_prompt, y_sample, pa_k, pa_v, pb_k, pb_v, pc_k, pc_v, pc_f,
            sa_k, sa_v, sb_k, sb_v, sc_k, sc_v, sc_f)
```

```python
import functools

import jax
import jax.numpy as jnp
from jax import lax
from jax.experimental import pallas as pl
from jax.experimental.pallas import tpu as pltpu

F32 = jnp.float32
BF16 = jnp.bfloat16

HEAD_DIM = 64
CHUNK = 64
BAND_CHUNKS = 8
BAND = BAND_CHUNKS * CHUNK
REL_CLIP = 128
EPS = 1e-6
SCALE = HEAD_DIM ** -0.5
LANES = 128
HEADS_PER_SLAB = LANES // HEAD_DIM
NORM_SLAB = 256
EXP_ZERO_BELOW = -104.0
MASKED = -1e30
VMEM_LIMIT = 56 * 1024 * 1024


def _cparams(*sem):
    return pltpu.CompilerParams(dimension_semantics=sem, vmem_limit_bytes=VMEM_LIMIT)


def _split2(x):
    hi = x.astype(BF16)
    lo = (x - hi.astype(F32)).astype(BF16)
    return hi, lo


def _split3(x):
    hi = x.astype(BF16)
    r = x - hi.astype(F32)
    mid = r.astype(BF16)
    lo = (r - mid.astype(F32)).astype(BF16)
    return hi, mid, lo


def _dot(a, b):
    return jnp.dot(a, b, preferred_element_type=F32)


def _dot_nt(a, b):
    return lax.dot_general(a, b, (((1,), (1,)), ((), ())), preferred_element_type=F32)


def _dot_split2(x, m):
    hi, lo = _split2(x)
    return _dot(hi, m) + _dot(lo, m)


def _dot_split3(x, m):
    hi, mid, lo = _split3(x)
    return (_dot(hi, m) + _dot(mid, m)) + _dot(lo, m)


def _rms_rows(x, g):
    ms = jnp.mean(x * x, axis=-1, keepdims=True)
    return x * lax.rsqrt(ms + EPS) * g


def _head_rms(x, gmat, gain_row):
    width = x.shape[1]
    outs = []
    for s in range(width // NORM_SLAB):
        xs = x[:, NORM_SLAB * s:NORM_SLAB * (s + 1)]
        ms = _dot_split2(xs * xs, gmat)
        outs.append(xs * lax.rsqrt(ms + EPS))
    y = outs[0] if len(outs) == 1 else jnp.concatenate(outs, axis=1)
    return y * gain_row


def _softplus(z):
    return jnp.maximum(z, 0.0) + jnp.log(1.0 + jnp.exp(-jnp.abs(z)))


def _log_sigmoid(z):
    return jnp.minimum(z, 0.0) - jnp.log(1.0 + jnp.exp(-jnp.abs(z)))


def _proj_even_kernel(h_ref, g_ref, w_ref, gq_ref, gk_ref, gmat_ref,
                      qa_ref, ka16_ref, va16_ref, qb_ref, kb16_ref, vb16_ref,
                      ka_ref, va_ref, kb_ref, vb_ref, *, wa, wb):
    xn = _rms_rows(h_ref[...], g_ref[...]).astype(BF16)
    proj = _dot(xn, w_ref[...])
    gmat = gmat_ref[...]
    qa = proj[:, 0:wa]
    ka = proj[:, wa:2 * wa]
    va = proj[:, 2 * wa:3 * wa]
    o = 3 * wa
    qb = _head_rms(proj[:, o:o + wb], gmat, gq_ref[...])
    kb = _head_rms(proj[:, o + wb:o + 2 * wb], gmat, gk_ref[...])
    vb = proj[:, o + 2 * wb:o + 3 * wb]
    qa_ref[...] = (qa * SCALE).astype(BF16)
    ka16_ref[...] = ka.astype(BF16)
    va16_ref[...] = va.astype(BF16)
    qb_ref[...] = (qb * SCALE).astype(BF16)
    kb16_ref[...] = kb.astype(BF16)
    vb16_ref[...] = vb.astype(BF16)
    ka_ref[...] = ka
    va_ref[...] = va
    kb_ref[...] = kb
    vb_ref[...] = vb


def _proj_odd_kernel(h_ref, g_ref, w_ref, gq_ref, gk_ref, gmat_ref, bf_ref,
                     q_ref, k16_ref, v16_ref, k_ref, v_ref, lf_ref, *, wc):
    xn = _rms_rows(h_ref[...], g_ref[...]).astype(BF16)
    proj = _dot(xn, w_ref[...])
    gmat = gmat_ref[...]
    q = _head_rms(proj[:, 0:wc], gmat, gq_ref[...])
    k = _head_rms(proj[:, wc:2 * wc], gmat, gk_ref[...])
    v = proj[:, 2 * wc:3 * wc]
    f = proj[:, 3 * wc:3 * wc + LANES]
    q_ref[...] = (q * SCALE).astype(BF16)
    k16_ref[...] = k.astype(BF16)
    v16_ref[...] = v.astype(BF16)
    k_ref[...] = k
    v_ref[...] = v
    lf_ref[...] = _log_sigmoid(f + bf_ref[...])


def _post1_kernel(h_ref, mix_ref, wo_ref, g_ref, w1_ref, h1_ref, hid_ref, *, ff_chunk):
    h1 = h_ref[...] + _dot(mix_ref[...], wo_ref[...])
    h1_ref[...] = h1
    xn = _rms_rows(h1, g_ref[...]).astype(BF16)
    d_ff = w1_ref.shape[1]
    for c in range(d_ff // ff_chunk):
        sl = slice(c * ff_chunk, (c + 1) * ff_chunk)
        a = jnp.maximum(_dot(xn, w1_ref[:, sl]), 0.0)
        hid_ref[:, sl] = (a * a).astype(BF16)


def _post2_kernel(h1_ref, hid_ref, w2_ref, g_ref, wg_ref, p_ref, wp_ref, out_ref):
    h2 = h1_ref[...] + _dot(hid_ref[...], w2_ref[...])
    xg = _rms_rows(h2, g_ref[...]).astype(BF16)
    gate = jax.nn.sigmoid(_dot(xg, wg_ref[...]))
    out_ref[...] = h2 + gate * _dot(p_ref[...].astype(BF16), wp_ref[...])


def _row_spec(tm, width):
    return pl.BlockSpec((tm, width), lambda i: (i, 0))


def _full_spec(shape):
    return pl.BlockSpec(shape, lambda i: (0,) * len(shape))


def _row_tile(rows):
    for tm in (256, 128, 64, 32, 16, 8):
        if rows % tm == 0:
            return tm
    raise ValueError(f"row count {rows} is not a multiple of 8")


def _proj_even(h, g, w, gq_row, gk_row, gmat, wa, wb):
    rows, d = h.shape
    tm = _row_tile(rows)
    widths = [wa, wa, wa, wb, wb, wb, wa, wa, wb, wb]
    dtypes = [BF16] * 6 + [F32] * 4
    return pl.pallas_call(
        functools.partial(_proj_even_kernel, wa=wa, wb=wb),
        grid=(rows // tm,),
        in_specs=[_row_spec(tm, d), _full_spec(g.shape), _full_spec(w.shape),
                  _full_spec(gq_row.shape), _full_spec(gk_row.shape), _full_spec(gmat.shape)],
        out_specs=[_row_spec(tm, wd) for wd in widths],
        out_shape=[jax.ShapeDtypeStruct((rows, wd), dt) for wd, dt in zip(widths, dtypes)],
        compiler_params=_cparams("parallel"),
        name="proj_even",
    )(h, g, w, gq_row, gk_row, gmat)


def _proj_odd(h, g, w, gq_row, gk_row, gmat, bf_row, wc):
    rows, d = h.shape
    tm = _row_tile(rows)
    widths = [wc, wc, wc, wc, wc, LANES]
    dtypes = [BF16] * 3 + [F32] * 3
    return pl.pallas_call(
        functools.partial(_proj_odd_kernel, wc=wc),
        grid=(rows // tm,),
        in_specs=[_row_spec(tm, d), _full_spec(g.shape), _full_spec(w.shape),
                  _full_spec(gq_row.shape), _full_spec(gk_row.shape), _full_spec(gmat.shape),
                  _full_spec(bf_row.shape)],
        out_specs=[_row_spec(tm, wd) for wd in widths],
        out_shape=[jax.ShapeDtypeStruct((rows, wd), dt) for wd, dt in zip(widths, dtypes)],
        compiler_params=_cparams("parallel"),
        name="proj_odd",
    )(h, g, w, gq_row, gk_row, gmat, bf_row)


def _post(h, mix, wo, g_mlp, w1, w2, g_ple, wg, p, wp):
    rows, d = h.shape
    tm = _row_tile(rows)
    d_ff = w1.shape[1]
    ff_chunk = 1024 if d_ff % 1024 == 0 else d_ff
    h1, hid = pl.pallas_call(
        functools.partial(_post1_kernel, ff_chunk=ff_chunk),
        grid=(rows // tm,),
        in_specs=[_row_spec(tm, d), _row_spec(tm, mix.shape[1]), _full_spec(wo.shape),
                  _full_spec(g_mlp.shape), _full_spec(w1.shape)],
        out_specs=[_row_spec(tm, d), _row_spec(tm, d_ff)],
        out_shape=[jax.ShapeDtypeStruct((rows, d), F32), jax.ShapeDtypeStruct((rows, d_ff), BF16)],
        compiler_params=_cparams("parallel"),
        name="post_attn_mlp_up",
    )(h, mix, wo, g_mlp, w1)
    return pl.pallas_call(
        _post2_kernel,
        grid=(rows // tm,),
        in_specs=[_row_spec(tm, d), _row_spec(tm, d_ff), _full_spec(w2.shape), _full_spec(g_ple.shape),
                  _full_spec(wg.shape), _row_spec(tm, p.shape[1]), _full_spec(wp.shape)],
        out_specs=_row_spec(tm, d),
        out_shape=jax.ShapeDtypeStruct((rows, d), F32),
        compiler_params=_cparams("parallel"),
        name="mlp_down_ple",
    )(h1, hid, w2, g_ple, wg, p, wp)


def _cumsum_kernel(x_ref, o_ref):
    gs, nb, _ = x_ref.shape
    x2 = x_ref[...].reshape(gs * nb, LANES)
    r = lax.broadcasted_iota(jnp.int32, (LANES, LANES), 0)
    c = lax.broadcasted_iota(jnp.int32, (LANES, LANES), 1)
    upper = jnp.where(r <= c, 1.0, 0.0).astype(BF16)
    ones = jnp.ones((LANES, LANES), BF16)
    within = _dot_split3(x2, upper).reshape(gs, nb, LANES)
    total = _dot_split3(x2, ones).reshape(gs, nb, LANES)
    rb = lax.broadcasted_iota(jnp.int32, (nb, nb), 0)
    cb = lax.broadcasted_iota(jnp.int32, (nb, nb), 1)
    strict_lower = jnp.where(cb < rb, 1.0, 0.0).astype(BF16)
    for g in range(gs):
        hi, mid, lo = _split3(total[g])
        offs = (_dot(strict_lower, hi) + _dot(strict_lower, mid)) + _dot(strict_lower, lo)
        o_ref[g] = within[g] + offs


def _cumsum_lanes(x):
    n_in = x.shape[1]
    x = _pad_to(x, LANES * LANES, 1)
    groups, n = x.shape
    nb = n // LANES
    gs = min(groups, 16)
    out = pl.pallas_call(
        _cumsum_kernel,
        grid=(groups // gs,),
        in_specs=[pl.BlockSpec((gs, nb, LANES), lambda i: (i, 0, 0))],
        out_specs=pl.BlockSpec((gs, nb, LANES), lambda i: (i, 0, 0)),
        out_shape=jax.ShapeDtypeStruct((groups, nb, LANES), F32),
        compiler_params=_cparams("parallel"),
        name="logf_cumsum",
    )(x.reshape(groups, nb, LANES))
    return out.reshape(groups, n)[:, :n_in]


def _neg_suffix_matrix(tk):
    r = lax.broadcasted_iota(jnp.int32, (tk, tk), 0)
    c = lax.broadcasted_iota(jnp.int32, (tk, tk), 1)
    return jnp.where(r >= c, -1.0, 0.0).astype(BF16)


def _stick_tile(qrows, kb, vb, nsuf, carry, acc, mask):
    z = _dot_nt(qrows, kb)
    sp = _softplus(z)
    if mask is not None:
        sp = jnp.where(mask, sp, 0.0)
    incl = _dot_split2(sp, nsuf)
    w = jnp.exp(z + incl + carry)
    if mask is not None:
        w = jnp.where(mask, w, 0.0)
    acc = acc + _dot(w.astype(BF16), vb)
    carry = carry + incl[:, 0:1]
    return carry, acc


def _softmax_tile(logits, vb, m, l, acc):
    m_new = jnp.maximum(m, jnp.max(logits, axis=1, keepdims=True))
    alpha = jnp.exp(m - m_new)
    p = jnp.exp(logits - m_new)
    l = alpha * l + jnp.sum(p, axis=1, keepdims=True)
    acc = alpha * acc + _dot(p.astype(BF16), vb)
    return m_new, l, acc


def _stack_heads(q, n_heads):
    t, width = q.shape
    rep = jnp.concatenate([q] * n_heads, axis=0) if n_heads > 1 else q
    row = lax.broadcasted_iota(jnp.int32, (n_heads * t, width), 0)
    lane = lax.broadcasted_iota(jnp.int32, (n_heads * t, width), 1)
    lo = (row // t) * HEAD_DIM
    keep = (lane >= lo) & (lane < lo + HEAD_DIM)
    return jnp.where(keep, rep, jnp.zeros_like(rep))


def _unstack_heads(acc, n_heads):
    rows, width = acc.shape
    t = rows // n_heads
    lane = lax.broadcasted_iota(jnp.int32, (t, width), 1)
    out = jnp.zeros((t, width), acc.dtype)
    for h in range(n_heads):
        sel = (lane >= h * HEAD_DIM) & (lane < (h + 1) * HEAD_DIM)
        out = jnp.where(sel, acc[h * t:(h + 1) * t, :], out)
    return out


def _row_query_index(rows, cols, t):
    r = lax.broadcasted_iota(jnp.int32, (rows, cols), 0)
    c = lax.broadcasted_iota(jnp.int32, (rows, cols), 1)
    return r % t, c


def _stick_prompt_kernel(q_ref, k_ref, v_ref, o_ref, carry_ref, acc_ref, *, t):
    i = pl.program_id(1)
    nh = HEADS_PER_SLAB
    qrows = _stack_heads(q_ref[...], nh)
    nsuf = _neg_suffix_matrix(t)
    tq_idx, c_idx = _row_query_index(nh * t, t, t)
    start = pl.multiple_of(i * t, t)
    carry, acc = _stick_tile(qrows, k_ref[pl.ds(start, t), :], v_ref[pl.ds(start, t), :], nsuf,
                             jnp.zeros((nh * t, 1), F32), jnp.zeros((nh * t, LANES), F32),
                             c_idx < tq_idx)
    carry_ref[...] = carry
    acc_ref[...] = acc

    def cond(state):
        j, carry_max = state
        return jnp.logical_and(j >= 0, carry_max > EXP_ZERO_BELOW)

    def body(state):
        j, _ = state
        s = pl.multiple_of(j * t, t)
        carry, acc = _stick_tile(qrows, k_ref[pl.ds(s, t), :], v_ref[pl.ds(s, t), :], nsuf,
                                 carry_ref[...], acc_ref[...], None)
        carry_ref[...] = carry
        acc_ref[...] = acc
        return j - 1, jnp.max(carry)

    lax.while_loop(cond, body, (i - 1, jnp.max(carry)))
    o_ref[...] = _unstack_heads(acc_ref[...], nh).astype(o_ref.dtype)


def _stick_prompt(q, k, v, t=256):
    n, width = q.shape
    t = min(t, n)
    slabs = width // LANES
    return pl.pallas_call(
        functools.partial(_stick_prompt_kernel, t=t),
        grid=(slabs, n // t),
        in_specs=[pl.BlockSpec((t, LANES), lambda p, i: (i, p)),
                  pl.BlockSpec((n, LANES), lambda p, i: (0, p)),
                  pl.BlockSpec((n, LANES), lambda p, i: (0, p))],
        out_specs=pl.BlockSpec((t, LANES), lambda p, i: (i, p)),
        out_shape=jax.ShapeDtypeStruct((n, width), BF16),
        scratch_shapes=[pltpu.VMEM((HEADS_PER_SLAB * t, 1), F32),
                        pltpu.VMEM((HEADS_PER_SLAB * t, LANES), F32)],
        compiler_params=_cparams("parallel", "parallel"),
        name="stick_prompt",
    )(q, k, v)


def _stick_sample_kernel(q_ref, kn_ref, vn_ref, ck_ref, cv_ref, o_ref, qrows_ref, carry_ref, acc_ref,
                         *, nh, tk):
    jj = pl.program_id(1)
    t = q_ref.shape[1]

    @pl.when(jj == 0)
    def _():
        qrows = _stack_heads(q_ref[0], nh)
        qrows_ref[...] = qrows
        tq_idx, c_idx = _row_query_index(nh * t, t, t)
        carry, acc = _stick_tile(qrows, kn_ref[0], vn_ref[0], _neg_suffix_matrix(t),
                                 jnp.zeros((nh * t, 1), F32), jnp.zeros(acc_ref.shape, F32),
                                 c_idx < tq_idx)
        carry_ref[...] = carry
        acc_ref[...] = acc

    carry, acc = _stick_tile(qrows_ref[...], ck_ref[0].astype(BF16), cv_ref[0].astype(BF16),
                             _neg_suffix_matrix(tk), carry_ref[...], acc_ref[...], None)
    carry_ref[...] = carry
    acc_ref[...] = acc

    @pl.when(jj == pl.num_programs(1) - 1)
    def _():
        o_ref[0] = _unstack_heads(acc_ref[...], nh).astype(o_ref.dtype)


def _stick_sample(q, kn, vn, cache_k, cache_v, tk=256):
    b, t, width = q.shape
    past = cache_k.shape[1]
    tk = min(tk, past)
    nblk = past // tk
    nh = width // HEAD_DIM
    new_spec = pl.BlockSpec((1, t, width), lambda bi, jj: (bi, 0, 0))
    cache_spec = pl.BlockSpec((1, tk, width), lambda bi, jj: (bi, nblk - 1 - jj, 0))
    return pl.pallas_call(
        functools.partial(_stick_sample_kernel, nh=nh, tk=tk),
        grid=(b, nblk),
        in_specs=[new_spec, new_spec, new_spec, cache_spec, cache_spec],
        out_specs=pl.BlockSpec((1, t, width), lambda bi, jj: (bi, 0, 0)),
        out_shape=jax.ShapeDtypeStruct((b, t, width), BF16),
        scratch_shapes=[pltpu.VMEM((nh * t, width), BF16),
                        pltpu.VMEM((nh * t, 1), F32),
                        pltpu.VMEM((nh * t, width), F32)],
        compiler_params=_cparams("parallel", "arbitrary"),
        name="stick_sample",
    )(q, kn, vn, cache_k, cache_v)


def _band_prompt_kernel(q_ref, k0_ref, k1_ref, k2_ref, v0_ref, v1_ref, v2_ref, bias_ref, o_ref, *, t, nprev):
    i = pl.program_id(1)
    nh = HEADS_PER_SLAB
    qrows = _stack_heads(q_ref[...], nh)
    krefs = (k0_ref, k1_ref, k2_ref)[3 - (nprev + 1):]
    vrefs = (v0_ref, v1_ref, v2_ref)[3 - (nprev + 1):]
    kcat = jnp.concatenate([r[...] for r in krefs], axis=0)
    vcat = jnp.concatenate([r[...] for r in vrefs], axis=0)
    logits = _dot_nt(qrows, kcat) + bias_ref[0]
    c_idx = lax.broadcasted_iota(jnp.int32, logits.shape, 1)
    logits = jnp.where(c_idx >= (nprev - i) * t, logits, MASKED)
    m = jnp.max(logits, axis=1, keepdims=True)
    p = jnp.exp(logits - m)
    l = jnp.sum(p, axis=1, keepdims=True)
    acc = _dot(p.astype(BF16), vcat) / l
    o_ref[...] = _unstack_heads(acc, nh).astype(o_ref.dtype)


def _band_bias_prompt(rel_table, t, nprev):
    n_heads = rel_table.shape[0]
    qp = jnp.arange(t)[:, None] + nprev * t
    kp = jnp.arange((nprev + 1) * t)[None, :]
    qc, kc = qp // CHUNK, kp // CHUNK
    visible = (kc <= qc) & (kc >= qc - BAND_CHUNKS)
    rel = jnp.clip(qp - kp, -REL_CLIP, REL_CLIP) + REL_CLIP
    bias = jnp.where(visible[None], rel_table.astype(F32)[:, rel], MASKED)
    return bias.reshape(n_heads // HEADS_PER_SLAB, HEADS_PER_SLAB * t, (nprev + 1) * t)


def _band_prompt(q, k, v, rel_table, t=256):
    n, width = q.shape
    t = min(t, n)
    assert t % CHUNK == 0
    nprev = min(-(-BAND // t), 2)
    assert nprev * t >= BAND
    slabs = width // LANES
    bias = _band_bias_prompt(rel_table, t, nprev)

    def kv_spec(back):
        return pl.BlockSpec((t, LANES), lambda p, i: (jnp.maximum(i - back, 0), p))

    return pl.pallas_call(
        functools.partial(_band_prompt_kernel, t=t, nprev=nprev),
        grid=(slabs, n // t),
        in_specs=[pl.BlockSpec((t, LANES), lambda p, i: (i, p)),
                  kv_spec(2), kv_spec(1), kv_spec(0), kv_spec(2), kv_spec(1), kv_spec(0),
                  pl.BlockSpec((1,) + bias.shape[1:], lambda p, i: (p, 0, 0))],
        out_specs=pl.BlockSpec((t, LANES), lambda p, i: (i, p)),
        out_shape=jax.ShapeDtypeStruct((n, width), BF16),
        compiler_params=_cparams("parallel", "parallel"),
        name="band_prompt",
    )(q, k, k, k, v, v, v, bias)


def _band_sample_kernel(q_ref, kn_ref, vn_ref, ck_ref, cv_ref, bc_ref, bn_ref, o_ref, *, nh):
    qrows = _stack_heads(q_ref[0], nh)
    lc = _dot_nt(qrows, ck_ref[0].astype(BF16)) + bc_ref[...]
    ln = _dot_nt(qrows, kn_ref[0]) + bn_ref[...]
    m = jnp.maximum(jnp.max(lc, axis=1, keepdims=True), jnp.max(ln, axis=1, keepdims=True))
    pc = jnp.exp(lc - m)
    pn = jnp.exp(ln - m)
    l = jnp.sum(pc, axis=1, keepdims=True) + jnp.sum(pn, axis=1, keepdims=True)
    acc = (_dot(pc.astype(BF16), cv_ref[0].astype(BF16)) + _dot(pn.astype(BF16), vn_ref[0])) / l
    o_ref[0] = _unstack_heads(acc, nh).astype(o_ref.dtype)


def _band_bias_sample(rel_table, t, past_len, buf_len):
    qp = past_len + jnp.arange(t)[:, None]

    def make(kp):
        qc, kc = qp // CHUNK, kp // CHUNK
        visible = (kp >= 0) & (kc <= qc) & (kc >= qc - BAND_CHUNKS)
        rel = jnp.clip(qp - kp, -REL_CLIP, REL_CLIP) + REL_CLIP
        bias = jnp.where(visible[None], rel_table.astype(F32)[:, rel], MASKED)
        return bias.reshape(-1, bias.shape[-1])

    return make(past_len - buf_len + jnp.arange(buf_len)[None, :]), make(past_len + jnp.arange(t)[None, :])


def _band_sample(q, kn, vn, cache_k, cache_v, rel_table, past_len):
    b, t, width = q.shape
    buf_len = cache_k.shape[1]
    nh = width // HEAD_DIM
    bias_c, bias_n = _band_bias_sample(rel_table, t, past_len, buf_len)
    new_spec = pl.BlockSpec((1, t, width), lambda bi: (bi, 0, 0))
    cache_spec = pl.BlockSpec((1, buf_len, width), lambda bi: (bi, 0, 0))
    return pl.pallas_call(
        functools.partial(_band_sample_kernel, nh=nh),
        grid=(b,),
        in_specs=[new_spec, new_spec, new_spec, cache_spec, cache_spec,
                  _full_spec(bias_c.shape), _full_spec(bias_n.shape)],
        out_specs=pl.BlockSpec((1, t, width), lambda bi: (bi, 0, 0)),
        out_shape=jax.ShapeDtypeStruct((b, t, width), BF16),
        compiler_params=_cparams("parallel"),
        name="band_sample",
    )(q, kn, vn, cache_k, cache_v, bias_c, bias_n)


def _key_decay_rows(dk, t):
    n_heads, tk = dk.shape
    return jnp.concatenate([jnp.broadcast_to(dk[h:h + 1, :], (t, tk)) for h in range(n_heads)], axis=0)


def _fox_prompt_kernel(q_ref, dq_ref, k_ref, v_ref, dk_ref, o_ref, m_ref, l_ref, acc_ref, *, t):
    i = pl.program_id(1)
    nh = HEADS_PER_SLAB
    qrows = _stack_heads(q_ref[...], nh)
    dq = jnp.concatenate([dq_ref[:, h:h + 1] for h in range(nh)], axis=0)
    m_ref[...] = jnp.full(m_ref.shape, MASKED, F32)
    l_ref[...] = jnp.zeros(l_ref.shape, F32)
    acc_ref[...] = jnp.zeros(acc_ref.shape, F32)

    def tile(j, mask):
        s = pl.multiple_of(j * t, t)
        z = _dot_nt(qrows, k_ref[pl.ds(s, t), :])
        logits = (z + dq) - _key_decay_rows(dk_ref[:, pl.ds(s, t)], t)
        if mask is not None:
            logits = jnp.where(mask, logits, MASKED)
        m, l, acc = _softmax_tile(logits, v_ref[pl.ds(s, t), :], m_ref[...], l_ref[...], acc_ref[...])
        m_ref[...] = m
        l_ref[...] = l
        acc_ref[...] = acc

    def body(j, carry):
        tile(j, None)
        return carry

    lax.fori_loop(0, i, body, 0)
    tq_idx, c_idx = _row_query_index(nh * t, t, t)
    tile(i, c_idx <= tq_idx)
    o_ref[...] = _unstack_heads(acc_ref[...] / l_ref[...], nh).astype(o_ref.dtype)


def _fox_prompt(q, k, v, cum, t=256):
    n, width = q.shape
    t = min(t, n)
    slabs = width // LANES
    nh = HEADS_PER_SLAB
    dq = cum.reshape(n, slabs, nh).transpose(1, 0, 2)
    dk = cum.reshape(n, slabs, nh).transpose(1, 2, 0)
    return pl.pallas_call(
        functools.partial(_fox_prompt_kernel, t=t),
        grid=(slabs, n // t),
        in_specs=[pl.BlockSpec((t, LANES), lambda p, i: (i, p)),
                  pl.BlockSpec((None, t, nh), lambda p, i: (p, i, 0)),
                  pl.BlockSpec((n, LANES), lambda p, i: (0, p)),
                  pl.BlockSpec((n, LANES), lambda p, i: (0, p)),
                  pl.BlockSpec((None, nh, n), lambda p, i: (p, 0, 0))],
        out_specs=pl.BlockSpec((t, LANES), lambda p, i: (i, p)),
        out_shape=jax.ShapeDtypeStruct((n, width), BF16),
        scratch_shapes=[pltpu.VMEM((nh * t, 1), F32), pltpu.VMEM((nh * t, 1), F32),
                        pltpu.VMEM((nh * t, LANES), F32)],
        compiler_params=_cparams("parallel", "parallel"),
        name="fox_prompt",
    )(q, dq, k, v, dk)


def _fox_sample_kernel(q_ref, dq_ref, kn_ref, vn_ref, dkn_ref, ck_ref, cv_ref, dkc_ref, o_ref,
                       qrows_ref, m_ref, l_ref, acc_ref, *, nh):
    jj = pl.program_id(1)
    t = q_ref.shape[1]
    dq = dq_ref[0]

    @pl.when(jj == 0)
    def _():
        qrows = _stack_heads(q_ref[0], nh)
        qrows_ref[...] = qrows
        tq_idx, c_idx = _row_query_index(nh * t, t, t)
        logits = (_dot_nt(qrows, kn_ref[0]) + dq) - _key_decay_rows(dkn_ref[0], t)
        logits = jnp.where(c_idx <= tq_idx, logits, MASKED)
        m, l, acc = _softmax_tile(logits, vn_ref[0], jnp.full(m_ref.shape, MASKED, F32),
                                  jnp.zeros(l_ref.shape, F32), jnp.zeros(acc_ref.shape, F32))
        m_ref[...] = m
        l_ref[...] = l
        acc_ref[...] = acc

    logits = (_dot_nt(qrows_ref[...], ck_ref[0].astype(BF16)) + dq) - _key_decay_rows(dkc_ref[0], t)
    m, l, acc = _softmax_tile(logits, cv_ref[0].astype(BF16), m_ref[...], l_ref[...], acc_ref[...])
    m_ref[...] = m
    l_ref[...] = l
    acc_ref[...] = acc

    @pl.when(jj == pl.num_programs(1) - 1)
    def _():
        o_ref[0] = _unstack_heads(acc_ref[...] / l_ref[...], nh).astype(o_ref.dtype)


def _fox_sample(q, kn, vn, cache_k, cache_v, cum_cache, cum_new, tk=256):
    b, t, width = q.shape
    past = cache_k.shape[1]
    tk = min(tk, past)
    nblk = past // tk
    nh = width // HEAD_DIM
    dq = cum_new.reshape(b, nh * t, 1)
    new_spec = pl.BlockSpec((1, t, width), lambda bi, jj: (bi, 0, 0))
    cache_spec = pl.BlockSpec((1, tk, width), lambda bi, jj: (bi, jj, 0))
    return pl.pallas_call(
        functools.partial(_fox_sample_kernel, nh=nh),
        grid=(b, nblk),
        in_specs=[new_spec, pl.BlockSpec((1, nh * t, 1), lambda bi, jj: (bi, 0, 0)),
                  new_spec, new_spec, pl.BlockSpec((1, nh, t), lambda bi, jj: (bi, 0, 0)),
                  cache_spec, cache_spec, pl.BlockSpec((1, nh, tk), lambda bi, jj: (bi, 0, jj))],
        out_specs=pl.BlockSpec((1, t, width), lambda bi, jj: (bi, 0, 0)),
        out_shape=jax.ShapeDtypeStruct((b, t, width), BF16),
        scratch_shapes=[pltpu.VMEM((nh * t, width), BF16), pltpu.VMEM((nh * t, 1), F32),
                        pltpu.VMEM((nh * t, 1), F32), pltpu.VMEM((nh * t, width), F32)],
        compiler_params=_cparams("parallel", "arbitrary"),
        name="fox_sample",
    )(q, dq, kn, vn, cum_new, cache_k, cache_v, cum_cache)


def _pad_to(x, multiple, axis):
    size = x.shape[axis]
    target = -(-size // multiple) * multiple
    if target == size:
        return x
    pad = [(0, 0)] * x.ndim
    pad[axis] = (0, target - size)
    return jnp.pad(x, pad)


def kernel(x_prompt, x_sample, cache_a_k, cache_a_v, cache_b_k, cache_b_v, cache_c_k, cache_c_v, cache_c_logf, p_prompt, p_sample, g_mix, w_in_even, g_qb, g_kb, rel_bias, w_in_odd, b_forget, g_qc, g_kc, w_out, g_mlp, w_ff1, w_ff2, g_ple, w_ple_gate, w_ple_proj):
    bp, n, d = x_prompt.shape
    bs, ts, _ = x_sample.shape
    assert bp == 1, "prompt kernels assume a single prompt stream"
    depth = g_mix.shape[0]
    past = cache_a_k.shape[2]
    buf_len = cache_b_k.shape[2]
    ha, hb, hc = cache_a_k.shape[3], cache_b_k.shape[3], cache_c_k.shape[3]
    wa, wb, wc = ha * HEAD_DIM, hb * HEAD_DIM, hc * HEAD_DIM
    rows_s = bs * ts

    blk = jnp.arange(NORM_SLAB) // HEAD_DIM
    gmat = jnp.where(blk[:, None] == blk[None, :], 1.0 / HEAD_DIM, 0.0).astype(BF16)

    h = jnp.concatenate([x_prompt.reshape(n, d), x_sample.reshape(rows_s, d)], axis=0)
    p_all = jnp.concatenate([p_prompt.reshape(depth, n, -1), p_sample.reshape(depth, rows_s, -1)], axis=1)

    st = {name: [] for name in ("pa_k", "pa_v", "pb_k", "pb_v", "pc_k", "pc_v", "pc_f",
                                "sa_k", "sa_v", "sb_k", "sb_v", "sc_k", "sc_v", "sc_f")}
    for i in range(depth):
        j = i // 2
        g_row = g_mix[i].reshape(1, d)
        if i % 2 == 0:
            gq_row = jnp.tile(g_qb[j], hb).reshape(1, wb)
            gk_row = jnp.tile(g_kb[j], hb).reshape(1, wb)
            qa, ka16, va16, qb, kb16, vb16, ka, va, kb, vb = _proj_even(
                h, g_row, w_in_even[j].astype(BF16), gq_row, gk_row, gmat, wa, wb)
            mix_a_p = _stick_prompt(qa[:n], ka16[:n], va16[:n])
            mix_b_p = _band_prompt(qb[:n], kb16[:n], vb16[:n], rel_bias[j])
            s3 = lambda a: a[n:].reshape(bs, ts, a.shape[1])
            mix_a_s = _stick_sample(s3(qa), s3(ka16), s3(va16),
                                    cache_a_k[j].reshape(bs, past, wa), cache_a_v[j].reshape(bs, past, wa))
            mix_b_s = _band_sample(s3(qb), s3(kb16), s3(vb16),
                                   cache_b_k[j].reshape(bs, buf_len, wb), cache_b_v[j].reshape(bs, buf_len, wb),
                                   rel_bias[j], past)
            mix = jnp.concatenate([jnp.concatenate([mix_a_p, mix_b_p], axis=1),
                                   jnp.concatenate([mix_a_s.reshape(rows_s, wa), mix_b_s.reshape(rows_s, wb)],
                                                   axis=1)], axis=0)
            st["pa_k"].append(ka[:n].reshape(1, n, ha, HEAD_DIM))
            st["pa_v"].append(va[:n].reshape(1, n, ha, HEAD_DIM))
            keep = min(BAND, n)
            st["pb_k"].append(kb[n - keep:n].reshape(1, keep, hb, HEAD_DIM))
            st["pb_v"].append(vb[n - keep:n].reshape(1, keep, hb, HEAD_DIM))
            st["sa_k"].append(ka[n:].reshape(bs, ts, ha, HEAD_DIM))
            st["sa_v"].append(va[n:].reshape(bs, ts, ha, HEAD_DIM))
            st["sb_k"].append(jnp.concatenate([cache_b_k[j], kb[n:].reshape(bs, ts, hb, HEAD_DIM)], axis=1)[:, ts:])
            st["sb_v"].append(jnp.concatenate([cache_b_v[j], vb[n:].reshape(bs, ts, hb, HEAD_DIM)], axis=1)[:, ts:])
        else:
            gq_row = jnp.tile(g_qc[j], hc).reshape(1, wc)
            gk_row = jnp.tile(g_kc[j], hc).reshape(1, wc)
            w_pad = _pad_to(w_in_odd[j], LANES, 1).astype(BF16)
            bf_row = _pad_to(b_forget[j].reshape(1, hc), LANES, 1)
            q, k16, v16, k, v, lf = _proj_odd(h, g_row, w_pad, gq_row, gk_row, gmat, bf_row, wc)
            log_f = lf[:, :hc]
            cum_p = _cumsum_lanes(log_f[:n].T)
            mix_p = _fox_prompt(q[:n], k16[:n], v16[:n], cum_p.T)
            lf_new = log_f[n:].reshape(bs, ts, hc).transpose(0, 2, 1)
            lf_all = jnp.concatenate([cache_c_logf[j].transpose(0, 2, 1), lf_new], axis=2)
            cum_s = _cumsum_lanes(lf_all.reshape(bs * hc, past + ts)).reshape(bs, hc, past + ts)
            s3 = lambda a: a[n:].reshape(bs, ts, a.shape[1])
            mix_s = _fox_sample(s3(q), s3(k16), s3(v16),
                                cache_c_k[j].reshape(bs, past, wc), cache_c_v[j].reshape(bs, past, wc),
                                cum_s[:, :, :past], cum_s[:, :, past:past + ts])
            mix = jnp.concatenate([mix_p, mix_s.reshape(rows_s, wc)], axis=0)
            st["pc_k"].append(k[:n].reshape(1, n, hc, HEAD_DIM))
            st["pc_v"].append(v[:n].reshape(1, n, hc, HEAD_DIM))
            st["pc_f"].append(log_f[:n].reshape(1, n, hc))
            st["sc_k"].append(k[n:].reshape(bs, ts, hc, HEAD_DIM))
            st["sc_v"].append(v[n:].reshape(bs, ts, hc, HEAD_DIM))
            st["sc_f"].append(log_f[n:].reshape(bs, ts, hc))
        h = _post(h, mix, w_out[i].astype(BF16), g_mlp[i].reshape(1, d), w_ff1[i].astype(BF16),
                  w_ff2[i].astype(BF16), g_ple[i].reshape(1, d), w_ple_gate[i].astype(BF16),
                  p_all[i], w_ple_proj[i].astype(BF16))

    y_prompt = h[:n].reshape(1, n, d)
    y_sample = h[n:].reshape(bs, ts, d)
    stk = {name: jnp.stack(vals) for name, vals in st.items()}
    return (y_prompt, y_sample, stk["pa_k"], stk["pa_v"], stk["pb_k"], stk["pb_v"],
            stk["pc_k"], stk["pc_v"], stk["pc_f"], stk["sa_k"], stk["sa_v"], stk["sb_k"], stk["sb_v"],
            stk["sc_k"], stk["sc_v"], stk["sc_f"])
```

```python
import functools

import numpy as np
import jax
import jax.numpy as jnp
from jax import lax
from jax.experimental import pallas as pl
from jax.experimental.pallas import tpu as pltpu

F32 = jnp.float32
BF16 = jnp.bfloat16

HEAD_DIM = 64
CHUNK = 64
BAND_CHUNKS = 8
BAND = BAND_CHUNKS * CHUNK
REL_CLIP = 128
EPS = 1e-6
SCALE = HEAD_DIM ** -0.5
LANES = 128
HEADS_PER_SLAB = LANES // HEAD_DIM
NORM_SLAB = 256
EXP_ZERO_BELOW = -104.0
MASKED = -1e30
VMEM_LIMIT = 56 * 1024 * 1024


def _cparams(*sem):
    return pltpu.CompilerParams(dimension_semantics=sem, vmem_limit_bytes=VMEM_LIMIT)


def _split2(x):
    hi = x.astype(BF16)
    lo = (x - hi.astype(F32)).astype(BF16)
    return hi, lo


def _split3(x):
    hi = x.astype(BF16)
    r = x - hi.astype(F32)
    mid = r.astype(BF16)
    lo = (r - mid.astype(F32)).astype(BF16)
    return hi, mid, lo


def _dot(a, b):
    return jnp.dot(a, b, preferred_element_type=F32)


def _dot_nt(a, b):
    return lax.dot_general(a, b, (((1,), (1,)), ((), ())), preferred_element_type=F32)


def _dot_split2(x, m):
    hi, lo = _split2(x)
    return _dot(hi, m) + _dot(lo, m)


def _dot_split3(x, m):
    hi, mid, lo = _split3(x)
    return (_dot(hi, m) + _dot(mid, m)) + _dot(lo, m)


def _rms_rows(x, g):
    ms = jnp.mean(x * x, axis=-1, keepdims=True)
    return x * lax.rsqrt(ms + EPS) * g


def _head_rms(x, gmat, gain_row):
    width = x.shape[1]
    outs = []
    for s in range(width // NORM_SLAB):
        xs = x[:, NORM_SLAB * s:NORM_SLAB * (s + 1)]
        ms = _dot_split2(xs * xs, gmat)
        outs.append(xs * lax.rsqrt(ms + EPS))
    y = outs[0] if len(outs) == 1 else jnp.concatenate(outs, axis=1)
    return y * gain_row


def _softplus(z):
    return jnp.maximum(z, 0.0) + jnp.log(1.0 + jnp.exp(-jnp.abs(z)))


def _log_sigmoid(z):
    return jnp.minimum(z, 0.0) - jnp.log(1.0 + jnp.exp(-jnp.abs(z)))


def _proj_even_kernel(h_ref, g_ref, w_ref, gq_ref, gk_ref, gmat_ref,
                      qa_ref, ka16_ref, va16_ref, qb_ref, kb16_ref, vb16_ref,
                      ka_ref, va_ref, kb_ref, vb_ref, *, wa, wb):
    xn = _rms_rows(h_ref[...], g_ref[...]).astype(BF16)
    proj = _dot(xn, w_ref[...])
    gmat = gmat_ref[...]
    qa = proj[:, 0:wa]
    ka = proj[:, wa:2 * wa]
    va = proj[:, 2 * wa:3 * wa]
    o = 3 * wa
    qb = _head_rms(proj[:, o:o + wb], gmat, gq_ref[...])
    kb = _head_rms(proj[:, o + wb:o + 2 * wb], gmat, gk_ref[...])
    vb = proj[:, o + 2 * wb:o + 3 * wb]
    qa_ref[...] = (qa * SCALE).astype(BF16)
    ka16_ref[...] = ka.astype(BF16)
    va16_ref[...] = va.astype(BF16)
    qb_ref[...] = (qb * SCALE).astype(BF16)
    kb16_ref[...] = kb.astype(BF16)
    vb16_ref[...] = vb.astype(BF16)
    ka_ref[...] = ka
    va_ref[...] = va
    kb_ref[...] = kb
    vb_ref[...] = vb


def _proj_odd_kernel(h_ref, g_ref, w_ref, gq_ref, gk_ref, gmat_ref, bf_ref,
                     q_ref, k16_ref, v16_ref, k_ref, v_ref, lf_ref, *, wc):
    xn = _rms_rows(h_ref[...], g_ref[...]).astype(BF16)
    proj = _dot(xn, w_ref[...])
    gmat = gmat_ref[...]
    q = _head_rms(proj[:, 0:wc], gmat, gq_ref[...])
    k = _head_rms(proj[:, wc:2 * wc], gmat, gk_ref[...])
    v = proj[:, 2 * wc:3 * wc]
    f = proj[:, 3 * wc:3 * wc + LANES]
    q_ref[...] = (q * SCALE).astype(BF16)
    k16_ref[...] = k.astype(BF16)
    v16_ref[...] = v.astype(BF16)
    k_ref[...] = k
    v_ref[...] = v
    lf_ref[...] = _log_sigmoid(f + bf_ref[...])


def _post1_kernel(h_ref, mix_ref, wo_ref, g_ref, w1_ref, h1_ref, hid_ref, *, ff_chunk):
    h1 = h_ref[...] + _dot(mix_ref[...], wo_ref[...])
    h1_ref[...] = h1
    xn = _rms_rows(h1, g_ref[...]).astype(BF16)
    d_ff = w1_ref.shape[1]
    for c in range(d_ff // ff_chunk):
        sl = slice(c * ff_chunk, (c + 1) * ff_chunk)
        a = jnp.maximum(_dot(xn, w1_ref[:, sl]), 0.0)
        hid_ref[:, sl] = (a * a).astype(BF16)


def _post2_kernel(h1_ref, hid_ref, w2_ref, g_ref, wg_ref, p_ref, wp_ref, out_ref):
    h2 = h1_ref[...] + _dot(hid_ref[...], w2_ref[...])
    xg = _rms_rows(h2, g_ref[...]).astype(BF16)
    gate = jax.nn.sigmoid(_dot(xg, wg_ref[...]))
    out_ref[...] = h2 + gate * _dot(p_ref[...].astype(BF16), wp_ref[...])


def _row_spec(tm, width):
    return pl.BlockSpec((tm, width), lambda i: (i, 0))


def _full_spec(shape):
    return pl.BlockSpec(shape, lambda i: (0,) * len(shape))


def _row_tile(rows):
    for tm in (256, 128, 64, 32, 16, 8):
        if rows % tm == 0:
            return tm
    raise ValueError(f"row count {rows} is not a multiple of 8")


def _proj_even(h, g, w, gq_row, gk_row, gmat, wa, wb):
    rows, d = h.shape
    tm = _row_tile(rows)
    widths = [wa, wa, wa, wb, wb, wb, wa, wa, wb, wb]
    dtypes = [BF16] * 6 + [F32] * 4
    return pl.pallas_call(
        functools.partial(_proj_even_kernel, wa=wa, wb=wb),
        grid=(rows // tm,),
        in_specs=[_row_spec(tm, d), _full_spec(g.shape), _full_spec(w.shape),
                  _full_spec(gq_row.shape), _full_spec(gk_row.shape), _full_spec(gmat.shape)],
        out_specs=[_row_spec(tm, wd) for wd in widths],
        out_shape=[jax.ShapeDtypeStruct((rows, wd), dt) for wd, dt in zip(widths, dtypes)],
        compiler_params=_cparams("parallel"),
        name="proj_even",
    )(h, g, w, gq_row, gk_row, gmat)


def _proj_odd(h, g, w, gq_row, gk_row, gmat, bf_row, wc):
    rows, d = h.shape
    tm = _row_tile(rows)
    widths = [wc, wc, wc, wc, wc, LANES]
    dtypes = [BF16] * 3 + [F32] * 3
    return pl.pallas_call(
        functools.partial(_proj_odd_kernel, wc=wc),
        grid=(rows // tm,),
        in_specs=[_row_spec(tm, d), _full_spec(g.shape), _full_spec(w.shape),
                  _full_spec(gq_row.shape), _full_spec(gk_row.shape), _full_spec(gmat.shape),
                  _full_spec(bf_row.shape)],
        out_specs=[_row_spec(tm, wd) for wd in widths],
        out_shape=[jax.ShapeDtypeStruct((rows, wd), dt) for wd, dt in zip(widths, dtypes)],
        compiler_params=_cparams("parallel"),
        name="proj_odd",
    )(h, g, w, gq_row, gk_row, gmat, bf_row)


def _post(h, mix, wo, g_mlp, w1, w2, g_ple, wg, p, wp):
    rows, d = h.shape
    tm = _row_tile(rows)
    d_ff = w1.shape[1]
    ff_chunk = 1024 if d_ff % 1024 == 0 else d_ff
    h1, hid = pl.pallas_call(
        functools.partial(_post1_kernel, ff_chunk=ff_chunk),
        grid=(rows // tm,),
        in_specs=[_row_spec(tm, d), _row_spec(tm, mix.shape[1]), _full_spec(wo.shape),
                  _full_spec(g_mlp.shape), _full_spec(w1.shape)],
        out_specs=[_row_spec(tm, d), _row_spec(tm, d_ff)],
        out_shape=[jax.ShapeDtypeStruct((rows, d), F32), jax.ShapeDtypeStruct((rows, d_ff), BF16)],
        compiler_params=_cparams("parallel"),
        name="post_attn_mlp_up",
    )(h, mix, wo, g_mlp, w1)
    return pl.pallas_call(
        _post2_kernel,
        grid=(rows // tm,),
        in_specs=[_row_spec(tm, d), _row_spec(tm, d_ff), _full_spec(w2.shape), _full_spec(g_ple.shape),
                  _full_spec(wg.shape), _row_spec(tm, p.shape[1]), _full_spec(wp.shape)],
        out_specs=_row_spec(tm, d),
        out_shape=jax.ShapeDtypeStruct((rows, d), F32),
        compiler_params=_cparams("parallel"),
        name="mlp_down_ple",
    )(h1, hid, w2, g_ple, wg, p, wp)


def _cumsum_kernel(x_ref, o_ref):
    gs, nb, _ = x_ref.shape
    x2 = x_ref[...].reshape(gs * nb, LANES)
    r = lax.broadcasted_iota(jnp.int32, (LANES, LANES), 0)
    c = lax.broadcasted_iota(jnp.int32, (LANES, LANES), 1)
    upper = jnp.where(r <= c, 1.0, 0.0).astype(BF16)
    ones = jnp.ones((LANES, LANES), BF16)
    within = _dot_split3(x2, upper).reshape(gs, nb, LANES)
    total = _dot_split3(x2, ones).reshape(gs, nb, LANES)
    rb = lax.broadcasted_iota(jnp.int32, (nb, nb), 0)
    cb = lax.broadcasted_iota(jnp.int32, (nb, nb), 1)
    strict_lower = jnp.where(cb < rb, 1.0, 0.0).astype(BF16)
    for g in range(gs):
        hi, mid, lo = _split3(total[g])
        offs = (_dot(strict_lower, hi) + _dot(strict_lower, mid)) + _dot(strict_lower, lo)
        o_ref[g] = within[g] + offs


def _cumsum_lanes(x):
    n_in = x.shape[1]
    x = _pad_to(x, LANES * LANES, 1)
    groups, n = x.shape
    nb = n // LANES
    gs = min(groups, 16)
    out = pl.pallas_call(
        _cumsum_kernel,
        grid=(groups // gs,),
        in_specs=[pl.BlockSpec((gs, nb, LANES), lambda i: (i, 0, 0))],
        out_specs=pl.BlockSpec((gs, nb, LANES), lambda i: (i, 0, 0)),
        out_shape=jax.ShapeDtypeStruct((groups, nb, LANES), F32),
        compiler_params=_cparams("parallel"),
        name="logf_cumsum",
    )(x.reshape(groups, nb, LANES))
    return out.reshape(groups, n)[:, :n_in]


def _neg_suffix_matrix(tk):
    r = lax.broadcasted_iota(jnp.int32, (tk, tk), 0)
    c = lax.broadcasted_iota(jnp.int32, (tk, tk), 1)
    return jnp.where(r >= c, -1.0, 0.0).astype(BF16)


def _stick_tile(qrows, kb, vb, nsuf, carry, acc, mask, feature_major=False):
    z = _dot(qrows, kb) if feature_major else _dot_nt(qrows, kb)
    sp = _softplus(z)
    if mask is not None:
        sp = jnp.where(mask, sp, 0.0)
    incl = _dot_split2(sp, nsuf)
    w = jnp.exp(z + incl + carry)
    if mask is not None:
        w = jnp.where(mask, w, 0.0)
    wb = w.astype(BF16)
    acc = acc + (_dot_nt(wb, vb) if feature_major else _dot(wb, vb))
    carry = carry + incl[:, 0:1]
    return carry, acc


def _softmax_tile(logits, vb, m, l, acc, feature_major=False):
    m_new = jnp.maximum(m, jnp.max(logits, axis=1, keepdims=True))
    alpha = jnp.exp(m - m_new)
    p = jnp.exp(logits - m_new)
    l = alpha * l + jnp.sum(p, axis=1, keepdims=True)
    pb = p.astype(BF16)
    acc = alpha * acc + (_dot_nt(pb, vb) if feature_major else _dot(pb, vb))
    return m_new, l, acc


def _stack_heads(q, n_heads):
    t, width = q.shape
    rep = jnp.concatenate([q] * n_heads, axis=0) if n_heads > 1 else q
    row = lax.broadcasted_iota(jnp.int32, (n_heads * t, width), 0)
    lane = lax.broadcasted_iota(jnp.int32, (n_heads * t, width), 1)
    lo = (row // t) * HEAD_DIM
    keep = (lane >= lo) & (lane < lo + HEAD_DIM)
    return jnp.where(keep, rep, jnp.zeros_like(rep))


def _unstack_heads(acc, n_heads):
    rows, width = acc.shape
    t = rows // n_heads
    lane = lax.broadcasted_iota(jnp.int32, (t, width), 1)
    out = jnp.zeros((t, width), acc.dtype)
    for h in range(n_heads):
        sel = (lane >= h * HEAD_DIM) & (lane < (h + 1) * HEAD_DIM)
        out = jnp.where(sel, acc[h * t:(h + 1) * t, :], out)
    return out


def _row_query_index(rows, cols, t):
    r = lax.broadcasted_iota(jnp.int32, (rows, cols), 0)
    c = lax.broadcasted_iota(jnp.int32, (rows, cols), 1)
    return r % t, c


def _stick_prompt_kernel(q_ref, k_ref, v_ref, o_ref, carry_ref, acc_ref, *, t):
    i = pl.program_id(1)
    nh = HEADS_PER_SLAB
    qrows = _stack_heads(q_ref[...], nh)
    nsuf = _neg_suffix_matrix(t)
    tq_idx, c_idx = _row_query_index(nh * t, t, t)
    start = pl.multiple_of(i * t, t)
    carry, acc = _stick_tile(qrows, k_ref[pl.ds(start, t), :], v_ref[pl.ds(start, t), :], nsuf,
                             jnp.zeros((nh * t, 1), F32), jnp.zeros((nh * t, LANES), F32),
                             c_idx < tq_idx)
    carry_ref[...] = carry
    acc_ref[...] = acc

    def cond(state):
        j, carry_max = state
        return jnp.logical_and(j >= 0, carry_max > EXP_ZERO_BELOW)

    def body(state):
        j, _ = state
        s = pl.multiple_of(j * t, t)
        carry, acc = _stick_tile(qrows, k_ref[pl.ds(s, t), :], v_ref[pl.ds(s, t), :], nsuf,
                                 carry_ref[...], acc_ref[...], None)
        carry_ref[...] = carry
        acc_ref[...] = acc
        return j - 1, jnp.max(carry)

    lax.while_loop(cond, body, (i - 1, jnp.max(carry)))
    o_ref[...] = _unstack_heads(acc_ref[...], nh).astype(o_ref.dtype)


def _stick_prompt(q, k, v, n, t=256):
    width = q.shape[1]
    t = min(t, n)
    slabs = width // LANES
    return pl.pallas_call(
        functools.partial(_stick_prompt_kernel, t=t),
        grid=(slabs, n // t),
        in_specs=[pl.BlockSpec((t, LANES), lambda p, i: (i, p)),
                  pl.BlockSpec((n, LANES), lambda p, i: (0, p)),
                  pl.BlockSpec((n, LANES), lambda p, i: (0, p))],
        out_specs=pl.BlockSpec((t, LANES), lambda p, i: (i, p)),
        out_shape=jax.ShapeDtypeStruct((n, width), BF16),
        scratch_shapes=[pltpu.VMEM((HEADS_PER_SLAB * t, 1), F32),
                        pltpu.VMEM((HEADS_PER_SLAB * t, LANES), F32)],
        compiler_params=_cparams("parallel", "parallel"),
        name="stick_prompt",
    )(q, k, v)


def _stick_sample_kernel(q_ref, kn_ref, vn_ref, ck_ref, cv_ref, nsuf_ref, o_ref, qrows_ref, carry_ref, acc_ref,
                         *, nh):
    jj = pl.program_id(1)
    t = q_ref.shape[1]

    @pl.when(jj == 0)
    def _():
        qrows = _stack_heads(q_ref[0], nh)
        qrows_ref[...] = qrows
        tq_idx, c_idx = _row_query_index(nh * t, t, t)
        carry, acc = _stick_tile(qrows, kn_ref[0], vn_ref[0], _neg_suffix_matrix(t),
                                 jnp.zeros((nh * t, 1), F32), jnp.zeros(acc_ref.shape, F32),
                                 c_idx < tq_idx)
        carry_ref[...] = carry
        acc_ref[...] = acc

    carry, acc = _stick_tile(qrows_ref[...], ck_ref[...].astype(BF16), cv_ref[...].astype(BF16),
                             nsuf_ref[...], carry_ref[...], acc_ref[...], None, feature_major=True)
    carry_ref[...] = carry
    acc_ref[...] = acc

    @pl.when(jj == pl.num_programs(1) - 1)
    def _():
        o_ref[0] = _unstack_heads(acc_ref[...], nh).astype(o_ref.dtype)


def _feature_major(cache):
    layers, b, past, heads, hd = cache.shape
    return jnp.transpose(cache, (0, 1, 3, 4, 2)).reshape(layers, b, heads * hd, past)


def _stick_sample(q, kn, vn, cache_kt, cache_vt, layer, tk=512):
    b, t, width = q.shape
    past = cache_kt.shape[3]
    tk = min(tk, past)
    nblk = past // tk
    nh = width // HEAD_DIM
    r = jnp.arange(tk)
    nsuf = jnp.where(r[:, None] >= r[None, :], -1.0, 0.0).astype(BF16)
    new_spec = pl.BlockSpec((1, t, width), lambda bi, jj: (bi, 0, 0))
    cache_spec = pl.BlockSpec((None, None, width, tk), lambda bi, jj: (layer, bi, 0, nblk - 1 - jj))
    return pl.pallas_call(
        functools.partial(_stick_sample_kernel, nh=nh),
        grid=(b, nblk),
        in_specs=[new_spec, new_spec, new_spec, cache_spec, cache_spec,
                  pl.BlockSpec((tk, tk), lambda bi, jj: (0, 0))],
        out_specs=pl.BlockSpec((1, t, width), lambda bi, jj: (bi, 0, 0)),
        out_shape=jax.ShapeDtypeStruct((b, t, width), BF16),
        scratch_shapes=[pltpu.VMEM((nh * t, width), BF16),
                        pltpu.VMEM((nh * t, 1), F32),
                        pltpu.VMEM((nh * t, width), F32)],
        compiler_params=_cparams("parallel", "arbitrary"),
        name="stick_sample",
    )(q, kn, vn, cache_kt, cache_vt, nsuf)


def _band_prompt_kernel(q_ref, k0_ref, k1_ref, k2_ref, v0_ref, v1_ref, v2_ref, bias_ref, o_ref, *, t, nprev):
    i = pl.program_id(1)
    nh = HEADS_PER_SLAB
    qrows = _stack_heads(q_ref[...], nh)
    krefs = (k0_ref, k1_ref, k2_ref)[3 - (nprev + 1):]
    vrefs = (v0_ref, v1_ref, v2_ref)[3 - (nprev + 1):]
    kcat = jnp.concatenate([r[...] for r in krefs], axis=0)
    vcat = jnp.concatenate([r[...] for r in vrefs], axis=0)
    logits = _dot_nt(qrows, kcat) + bias_ref[0]
    c_idx = lax.broadcasted_iota(jnp.int32, logits.shape, 1)
    logits = jnp.where(c_idx >= (nprev - i) * t, logits, MASKED)
    m = jnp.max(logits, axis=1, keepdims=True)
    p = jnp.exp(logits - m)
    l = jnp.sum(p, axis=1, keepdims=True)
    acc = _dot(p.astype(BF16), vcat) / l
    o_ref[...] = _unstack_heads(acc, nh).astype(o_ref.dtype)


def _rel_bias(rel_table, nq, nk, q0, k0):
    m_len = nq + nk - 1
    d = np.clip(np.arange(m_len) - (nk - 1) + (q0 - k0), -REL_CLIP, REL_CLIP) + REL_CLIP
    n_lo = int(np.sum(d == d[0])) - 1 if d[0] == 0 else 0
    n_hi = int(np.sum(d == d[-1])) - 1 if d[-1] == 2 * REL_CLIP else 0
    tab = rel_table.astype(F32)
    n_heads = tab.shape[0]
    mid = tab[:, int(d[n_lo]):int(d[m_len - 1 - n_hi]) + 1]
    g = jnp.concatenate([jnp.broadcast_to(tab[:, :1], (n_heads, n_lo)), mid,
                         jnp.broadcast_to(tab[:, -1:], (n_heads, n_hi))], axis=1)
    hankel = jnp.tile(g, (1, nq + 1))[:, :nq * (m_len + 1)].reshape(n_heads, nq, m_len + 1)[:, :, :nk]
    bias = hankel[:, :, ::-1]
    qp = q0 + np.arange(nq)[:, None]
    kp = k0 + np.arange(nk)[None, :]
    qc, kc = qp // CHUNK, kp // CHUNK
    visible = (kp >= 0) & (kc <= qc) & (kc >= qc - BAND_CHUNKS)
    return jnp.where(jnp.asarray(visible)[None], bias, MASKED)


def _band_prompt(q, k, v, rel_table, n, t=256):
    width = q.shape[1]
    t = min(t, n)
    assert t % CHUNK == 0
    nprev = min(-(-BAND // t), 2)
    assert nprev * t >= BAND
    slabs = width // LANES
    bias = _rel_bias(rel_table, t, (nprev + 1) * t, nprev * t, 0)
    bias = bias.reshape(slabs, HEADS_PER_SLAB * t, (nprev + 1) * t)

    def kv_spec(back):
        return pl.BlockSpec((t, LANES), lambda p, i: (jnp.maximum(i - back, 0), p))

    return pl.pallas_call(
        functools.partial(_band_prompt_kernel, t=t, nprev=nprev),
        grid=(slabs, n // t),
        in_specs=[pl.BlockSpec((t, LANES), lambda p, i: (i, p)),
                  kv_spec(2), kv_spec(1), kv_spec(0), kv_spec(2), kv_spec(1), kv_spec(0),
                  pl.BlockSpec((1,) + bias.shape[1:], lambda p, i: (p, 0, 0))],
        out_specs=pl.BlockSpec((t, LANES), lambda p, i: (i, p)),
        out_shape=jax.ShapeDtypeStruct((n, width), BF16),
        compiler_params=_cparams("parallel", "parallel"),
        name="band_prompt",
    )(q, k, k, k, v, v, v, bias)


def _band_sample_kernel(q_ref, kn_ref, vn_ref, ck_ref, cv_ref, bc_ref, bn_ref, o_ref, *, nh):
    qrows = _stack_heads(q_ref[0], nh)
    lc = _dot(qrows, ck_ref[...].astype(BF16)) + bc_ref[...]
    ln = _dot_nt(qrows, kn_ref[0]) + bn_ref[...]
    m = jnp.maximum(jnp.max(lc, axis=1, keepdims=True), jnp.max(ln, axis=1, keepdims=True))
    pc = jnp.exp(lc - m)
    pn = jnp.exp(ln - m)
    l = jnp.sum(pc, axis=1, keepdims=True) + jnp.sum(pn, axis=1, keepdims=True)
    acc = (_dot_nt(pc.astype(BF16), cv_ref[...].astype(BF16)) + _dot(pn.astype(BF16), vn_ref[0])) / l
    o_ref[0] = _unstack_heads(acc, nh).astype(o_ref.dtype)


def _band_sample(q, kn, vn, cache_kt, cache_vt, layer, rel_table, past_len):
    b, t, width = q.shape
    buf_len = cache_kt.shape[3]
    nh = width // HEAD_DIM
    bias_c = _rel_bias(rel_table, t, buf_len, past_len, past_len - buf_len).reshape(nh * t, buf_len)
    bias_n = _rel_bias(rel_table, t, t, past_len, past_len).reshape(nh * t, t)
    new_spec = pl.BlockSpec((1, t, width), lambda bi: (bi, 0, 0))
    cache_spec = pl.BlockSpec((None, None, width, buf_len), lambda bi: (layer, bi, 0, 0))
    return pl.pallas_call(
        functools.partial(_band_sample_kernel, nh=nh),
        grid=(b,),
        in_specs=[new_spec, new_spec, new_spec, cache_spec, cache_spec,
                  _full_spec(bias_c.shape), _full_spec(bias_n.shape)],
        out_specs=pl.BlockSpec((1, t, width), lambda bi: (bi, 0, 0)),
        out_shape=jax.ShapeDtypeStruct((b, t, width), BF16),
        compiler_params=_cparams("parallel"),
        name="band_sample",
    )(q, kn, vn, cache_kt, cache_vt, bias_c, bias_n)


def _key_decay_rows(dk, t):
    n_heads, tk = dk.shape
    return jnp.concatenate([jnp.broadcast_to(dk[h:h + 1, :], (t, tk)) for h in range(n_heads)], axis=0)


def _fox_prompt_kernel(q_ref, dq_ref, k_ref, v_ref, dk_ref, o_ref, m_ref, l_ref, acc_ref, *, t):
    i = pl.program_id(1)
    nh = HEADS_PER_SLAB
    qrows = _stack_heads(q_ref[...], nh)
    dq = jnp.concatenate([dq_ref[:, h:h + 1] for h in range(nh)], axis=0)
    m_ref[...] = jnp.full(m_ref.shape, MASKED, F32)
    l_ref[...] = jnp.zeros(l_ref.shape, F32)
    acc_ref[...] = jnp.zeros(acc_ref.shape, F32)

    def tile(j, mask):
        s = pl.multiple_of(j * t, t)
        z = _dot_nt(qrows, k_ref[pl.ds(s, t), :])
        logits = (z + dq) - _key_decay_rows(dk_ref[:, pl.ds(s, t)], t)
        if mask is not None:
            logits = jnp.where(mask, logits, MASKED)
        m, l, acc = _softmax_tile(logits, v_ref[pl.ds(s, t), :], m_ref[...], l_ref[...], acc_ref[...])
        m_ref[...] = m
        l_ref[...] = l
        acc_ref[...] = acc

    def body(j, carry):
        tile(j, None)
        return carry

    lax.fori_loop(0, i, body, 0)
    tq_idx, c_idx = _row_query_index(nh * t, t, t)
    tile(i, c_idx <= tq_idx)
    o_ref[...] = _unstack_heads(acc_ref[...] / l_ref[...], nh).astype(o_ref.dtype)


def _fox_prompt_bounded_kernel(shift_ref, q_ref, dq_ref, k_ref, v_ref, dk_ref, o_ref, dqrep_ref, acc_ref, *, t):
    i = pl.program_id(1)
    nh = HEADS_PER_SLAB
    qrows = _stack_heads(q_ref[...], nh)
    dq = jnp.concatenate([dq_ref[:, h:h + 1] for h in range(nh)], axis=0) - shift_ref[0, 0]
    dqrep_ref[...] = jnp.broadcast_to(dq, (nh * t, LANES))
    acc_ref[...] = jnp.zeros(acc_ref.shape, F32)
    ones = jnp.ones((t, LANES), BF16)
    row = lax.broadcasted_iota(jnp.int32, (t, LANES), 0)
    lane = lax.broadcasted_iota(jnp.int32, (t, LANES), 1)

    def tile(j, causal):
        s = pl.multiple_of(j * t, t)
        z = _dot_nt(qrows, k_ref[pl.ds(s, t), :])
        halves = []
        for h in range(nh):
            dqh = dqrep_ref[h * t:(h + 1) * t, :]
            cols = []
            for c in range(t // LANES):
                lg = (z[h * t:(h + 1) * t, c * LANES:(c + 1) * LANES] + dqh) \
                    - dk_ref[h:h + 1, pl.ds(s + c * LANES, LANES)]
                if causal:
                    lg = jnp.where(lane + c * LANES <= row, lg, MASKED)
                cols.append(jnp.exp(lg).astype(BF16))
            halves.append(jnp.concatenate(cols, axis=1))
        p = jnp.concatenate(halves, axis=0)
        v_ones = jnp.concatenate([v_ref[pl.ds(s, t), :], ones], axis=1)
        acc_ref[...] += _dot(p, v_ones)

    def body(j, carry):
        tile(j, False)
        return carry

    lax.fori_loop(0, i, body, 0)
    tile(i, True)
    acc = acc_ref[...]
    o_ref[...] = _unstack_heads(acc[:, :LANES] / acc[:, LANES:], nh).astype(o_ref.dtype)


MAX_FIXED_SHIFT = 30.0


def _fox_prompt(q, k, v, cum, n, logit_bound, t_exact=256, t_bounded=512):
    width = q.shape[1]
    slabs = width // LANES
    nh = HEADS_PER_SLAB
    dq = cum.reshape(n, slabs, nh).transpose(1, 0, 2)
    dk = cum.reshape(n, slabs, nh).transpose(1, 2, 0)

    def specs(t):
        return ([pl.BlockSpec((t, LANES), lambda p, i: (i, p)),
                 pl.BlockSpec((None, t, nh), lambda p, i: (p, i, 0)),
                 pl.BlockSpec((n, LANES), lambda p, i: (0, p)),
                 pl.BlockSpec((n, LANES), lambda p, i: (0, p)),
                 pl.BlockSpec((None, nh, n), lambda p, i: (p, 0, 0))],
                pl.BlockSpec((t, LANES), lambda p, i: (i, p)))

    def exact(_):
        t = min(t_exact, n)
        in_specs, out_spec = specs(t)
        return pl.pallas_call(
            functools.partial(_fox_prompt_kernel, t=t),
            grid=(slabs, n // t),
            in_specs=in_specs, out_specs=out_spec,
            out_shape=jax.ShapeDtypeStruct((n, width), BF16),
            scratch_shapes=[pltpu.VMEM((nh * t, 1), F32), pltpu.VMEM((nh * t, 1), F32),
                            pltpu.VMEM((nh * t, LANES), F32)],
            compiler_params=_cparams("parallel", "parallel"),
            name="fox_prompt_running_max",
        )(q, dq, k, v, dk)

    def bounded(shift):
        t = min(t_bounded, n)
        in_specs, out_spec = specs(t)
        return pl.pallas_call(
            functools.partial(_fox_prompt_bounded_kernel, t=t),
            grid=(slabs, n // t),
            in_specs=[pl.BlockSpec(memory_space=pltpu.SMEM)] + in_specs, out_specs=out_spec,
            out_shape=jax.ShapeDtypeStruct((n, width), BF16),
            scratch_shapes=[pltpu.VMEM((nh * t, LANES), F32), pltpu.VMEM((nh * t, 2 * LANES), F32)],
            compiler_params=_cparams("parallel", "parallel"),
            name="fox_prompt",
        )(shift.reshape(1, 1), q, dq, k, v, dk)

    return lax.cond(logit_bound < MAX_FIXED_SHIFT, bounded, exact, logit_bound)


def _fox_sample_kernel(q_ref, dq_ref, kn_ref, vn_ref, dkn_ref, ck_ref, cv_ref, dkc_ref, o_ref,
                       qrows_ref, m_ref, l_ref, acc_ref, *, nh):
    jj = pl.program_id(1)
    t = q_ref.shape[1]
    dq = dq_ref[0]

    @pl.when(jj == 0)
    def _():
        qrows = _stack_heads(q_ref[0], nh)
        qrows_ref[...] = qrows
        tq_idx, c_idx = _row_query_index(nh * t, t, t)
        logits = (_dot_nt(qrows, kn_ref[0]) + dq) - _key_decay_rows(dkn_ref[0], t)
        logits = jnp.where(c_idx <= tq_idx, logits, MASKED)
        m, l, acc = _softmax_tile(logits, vn_ref[0], jnp.full(m_ref.shape, MASKED, F32),
                                  jnp.zeros(l_ref.shape, F32), jnp.zeros(acc_ref.shape, F32))
        m_ref[...] = m
        l_ref[...] = l
        acc_ref[...] = acc

    logits = (_dot(qrows_ref[...], ck_ref[...].astype(BF16)) + dq) - _key_decay_rows(dkc_ref[0], t)
    m, l, acc = _softmax_tile(logits, cv_ref[...].astype(BF16), m_ref[...], l_ref[...], acc_ref[...],
                              feature_major=True)
    m_ref[...] = m
    l_ref[...] = l
    acc_ref[...] = acc

    @pl.when(jj == pl.num_programs(1) - 1)
    def _():
        o_ref[0] = _unstack_heads(acc_ref[...] / l_ref[...], nh).astype(o_ref.dtype)


def _fox_sample(q, kn, vn, cache_kt, cache_vt, layer, cum_cache, cum_new, tk=512):
    b, t, width = q.shape
    past = cache_kt.shape[3]
    tk = min(tk, past)
    nblk = past // tk
    nh = width // HEAD_DIM
    dq = cum_new.reshape(b, nh * t, 1)
    new_spec = pl.BlockSpec((1, t, width), lambda bi, jj: (bi, 0, 0))
    cache_spec = pl.BlockSpec((None, None, width, tk), lambda bi, jj: (layer, bi, 0, jj))
    return pl.pallas_call(
        functools.partial(_fox_sample_kernel, nh=nh),
        grid=(b, nblk),
        in_specs=[new_spec, pl.BlockSpec((1, nh * t, 1), lambda bi, jj: (bi, 0, 0)),
                  new_spec, new_spec, pl.BlockSpec((1, nh, t), lambda bi, jj: (bi, 0, 0)),
                  cache_spec, cache_spec, pl.BlockSpec((1, nh, tk), lambda bi, jj: (bi, 0, jj))],
        out_specs=pl.BlockSpec((1, t, width), lambda bi, jj: (bi, 0, 0)),
        out_shape=jax.ShapeDtypeStruct((b, t, width), BF16),
        scratch_shapes=[pltpu.VMEM((nh * t, width), BF16), pltpu.VMEM((nh * t, 1), F32),
                        pltpu.VMEM((nh * t, 1), F32), pltpu.VMEM((nh * t, width), F32)],
        compiler_params=_cparams("parallel", "arbitrary"),
        name="fox_sample",
    )(q, dq, kn, vn, cum_new, cache_kt, cache_vt, cum_cache)


def _pad_to(x, multiple, axis):
    size = x.shape[axis]
    target = -(-size // multiple) * multiple
    if target == size:
        return x
    pad = [(0, 0)] * x.ndim
    pad[axis] = (0, target - size)
    return jnp.pad(x, pad)


def kernel(x_prompt, x_sample, cache_a_k, cache_a_v, cache_b_k, cache_b_v, cache_c_k, cache_c_v, cache_c_logf, p_prompt, p_sample, g_mix, w_in_even, g_qb, g_kb, rel_bias, w_in_odd, b_forget, g_qc, g_kc, w_out, g_mlp, w_ff1, w_ff2, g_ple, w_ple_gate, w_ple_proj):
    bp, n, d = x_prompt.shape
    bs, ts, _ = x_sample.shape
    assert bp == 1, "prompt kernels assume a single prompt stream"
    depth = g_mix.shape[0]
    past = cache_a_k.shape[2]
    buf_len = cache_b_k.shape[2]
    ha, hb, hc = cache_a_k.shape[3], cache_b_k.shape[3], cache_c_k.shape[3]
    wa, wb, wc = ha * HEAD_DIM, hb * HEAD_DIM, hc * HEAD_DIM
    rows_s = bs * ts

    blk = jnp.arange(NORM_SLAB) // HEAD_DIM
    gmat = jnp.where(blk[:, None] == blk[None, :], 1.0 / HEAD_DIM, 0.0).astype(BF16)

    cache_a_kt, cache_a_vt = _feature_major(cache_a_k), _feature_major(cache_a_v)
    cache_b_kt, cache_b_vt = _feature_major(cache_b_k), _feature_major(cache_b_v)
    cache_c_kt, cache_c_vt = _feature_major(cache_c_k), _feature_major(cache_c_v)

    h = jnp.concatenate([x_prompt.reshape(n, d), x_sample.reshape(rows_s, d)], axis=0)
    p_all = jnp.concatenate([p_prompt.reshape(depth, n, -1), p_sample.reshape(depth, rows_s, -1)], axis=1)

    st = {name: [] for name in ("pa_k", "pa_v", "pb_k", "pb_v", "pc_k", "pc_v", "pc_f",
                                "sa_k", "sa_v", "sb_k", "sb_v", "sc_k", "sc_v", "sc_f")}
    for i in range(depth):
        j = i // 2
        g_row = g_mix[i].reshape(1, d)
        if i % 2 == 0:
            gq_row = jnp.tile(g_qb[j], hb).reshape(1, wb)
            gk_row = jnp.tile(g_kb[j], hb).reshape(1, wb)
            qa, ka16, va16, qb, kb16, vb16, ka, va, kb, vb = _proj_even(
                h, g_row, w_in_even[j].astype(BF16), gq_row, gk_row, gmat, wa, wb)
            mix_a_p = _stick_prompt(qa, ka16, va16, n)
            mix_b_p = _band_prompt(qb, kb16, vb16, rel_bias[j], n)
            s3 = lambda a: a[n:].reshape(bs, ts, a.shape[1])
            mix_a_s = _stick_sample(s3(qa), s3(ka16), s3(va16), cache_a_kt, cache_a_vt, j)
            mix_b_s = _band_sample(s3(qb), s3(kb16), s3(vb16), cache_b_kt, cache_b_vt, j, rel_bias[j], past)
            mix = jnp.concatenate([jnp.concatenate([mix_a_p, mix_b_p], axis=1),
                                   jnp.concatenate([mix_a_s.reshape(rows_s, wa), mix_b_s.reshape(rows_s, wb)],
                                                   axis=1)], axis=0)
            st["pa_k"].append(ka[:n].reshape(1, n, ha, HEAD_DIM))
            st["pa_v"].append(va[:n].reshape(1, n, ha, HEAD_DIM))
            keep = min(BAND, n)
            st["pb_k"].append(kb[n - keep:n].reshape(1, keep, hb, HEAD_DIM))
            st["pb_v"].append(vb[n - keep:n].reshape(1, keep, hb, HEAD_DIM))
            st["sa_k"].append(ka[n:].reshape(bs, ts, ha, HEAD_DIM))
            st["sa_v"].append(va[n:].reshape(bs, ts, ha, HEAD_DIM))
            st["sb_k"].append(jnp.concatenate([cache_b_k[j], kb[n:].reshape(bs, ts, hb, HEAD_DIM)], axis=1)[:, ts:])
            st["sb_v"].append(jnp.concatenate([cache_b_v[j], vb[n:].reshape(bs, ts, hb, HEAD_DIM)], axis=1)[:, ts:])
        else:
            gq_row = jnp.tile(g_qc[j], hc).reshape(1, wc)
            gk_row = jnp.tile(g_kc[j], hc).reshape(1, wc)
            w_pad = _pad_to(w_in_odd[j], LANES, 1).astype(BF16)
            bf_row = _pad_to(b_forget[j].reshape(1, hc), LANES, 1)
            q, k16, v16, k, v, lf = _proj_odd(h, g_row, w_pad, gq_row, gk_row, gmat, bf_row, wc)
            log_f = lf[:, :hc]
            cum_p = _cumsum_lanes(log_f[:n].T)
            logit_bound = 1.01 * HEAD_DIM * SCALE * jnp.max(jnp.abs(g_qc[j])) * jnp.max(jnp.abs(g_kc[j]))
            mix_p = _fox_prompt(q, k16, v16, cum_p.T, n, logit_bound.astype(F32))
            lf_new = log_f[n:].reshape(bs, ts, hc).transpose(0, 2, 1)
            lf_all = jnp.concatenate([cache_c_logf[j].transpose(0, 2, 1), lf_new], axis=2)
            cum_s = _cumsum_lanes(lf_all.reshape(bs * hc, past + ts)).reshape(bs, hc, past + ts)
            s3 = lambda a: a[n:].reshape(bs, ts, a.shape[1])
            mix_s = _fox_sample(s3(q), s3(k16), s3(v16), cache_c_kt, cache_c_vt, j,
                                cum_s[:, :, :past], cum_s[:, :, past:past + ts])
            mix = jnp.concatenate([mix_p, mix_s.reshape(rows_s, wc)], axis=0)
            st["pc_k"].append(k[:n].reshape(1, n, hc, HEAD_DIM))
            st["pc_v"].append(v[:n].reshape(1, n, hc, HEAD_DIM))
            st["pc_f"].append(log_f[:n].reshape(1, n, hc))
            st["sc_k"].append(k[n:].reshape(bs, ts, hc, HEAD_DIM))
            st["sc_v"].append(v[n:].reshape(bs, ts, hc, HEAD_DIM))
            st["sc_f"].append(log_f[n:].reshape(bs, ts, hc))
        h = _post(h, mix, w_out[i].astype(BF16), g_mlp[i].reshape(1, d), w_ff1[i].astype(BF16),
                  w_ff2[i].astype(BF16), g_ple[i].reshape(1, d), w_ple_gate[i].astype(BF16),
                  p_all[i], w_ple_proj[i].astype(BF16))

    y_prompt = h[:n].reshape(1, n, d)
    y_sample = h[n:].reshape(bs, ts, d)
    stk = {name: jnp.stack(vals) for name, vals in st.items()}
    return (y_prompt, y_sample, stk["pa_k"], stk["pa_v"], stk["pb_k"], stk["pb_v"],
            stk["pc_k"], stk["pc_v"], stk["pc_f"], stk["sa_k"], stk["sa_v"], stk["sb_k"], stk["sb_v"],
            stk["sc_k"], stk["sc_v"], stk["sc_f"])
```

```python
import functools

import numpy as np
import jax
import jax.numpy as jnp
from jax import lax
from jax.experimental import pallas as pl
from jax.experimental.pallas import tpu as pltpu

F32 = jnp.float32
BF16 = jnp.bfloat16

HEAD_DIM = 64
CHUNK = 64
BAND_CHUNKS = 8
BAND = BAND_CHUNKS * CHUNK
REL_CLIP = 128
EPS = 1e-6
SCALE = HEAD_DIM ** -0.5
LANES = 128
HEADS_PER_SLAB = LANES // HEAD_DIM
NORM_SLAB = 256
EXP_ZERO_BELOW = -104.0
MASKED = -1e30
VMEM_LIMIT = 56 * 1024 * 1024


def _cparams(*sem):
    return pltpu.CompilerParams(dimension_semantics=sem, vmem_limit_bytes=VMEM_LIMIT)


def _split2(x):
    hi = x.astype(BF16)
    lo = (x - hi.astype(F32)).astype(BF16)
    return hi, lo


def _split3(x):
    hi = x.astype(BF16)
    r = x - hi.astype(F32)
    mid = r.astype(BF16)
    lo = (r - mid.astype(F32)).astype(BF16)
    return hi, mid, lo


def _dot(a, b):
    return jnp.dot(a, b, preferred_element_type=F32)


def _dot_nt(a, b):
    return lax.dot_general(a, b, (((1,), (1,)), ((), ())), preferred_element_type=F32)


def _dot_split2(x, m):
    hi, lo = _split2(x)
    return _dot(hi, m) + _dot(lo, m)


def _dot_split3(x, m):
    hi, mid, lo = _split3(x)
    return (_dot(hi, m) + _dot(mid, m)) + _dot(lo, m)


def _rms_rows(x, g):
    ms = jnp.mean(x * x, axis=-1, keepdims=True)
    return x * lax.rsqrt(ms + EPS) * g


def _head_rms(x, gmat, gain_row):
    width = x.shape[1]
    outs = []
    for s in range(width // NORM_SLAB):
        xs = x[:, NORM_SLAB * s:NORM_SLAB * (s + 1)]
        ms = _dot_split2(xs * xs, gmat)
        outs.append(xs * lax.rsqrt(ms + EPS))
    y = outs[0] if len(outs) == 1 else jnp.concatenate(outs, axis=1)
    return y * gain_row


def _softplus(z):
    return jnp.maximum(z, 0.0) + jnp.log(1.0 + jnp.exp(-jnp.abs(z)))


def _log_sigmoid(z):
    return jnp.minimum(z, 0.0) - jnp.log(1.0 + jnp.exp(-jnp.abs(z)))


def _proj_even_kernel(h_ref, g_ref, w_ref, gq_ref, gk_ref, gmat_ref,
                      qa_ref, ka16_ref, va16_ref, qb_ref, kb16_ref, vb16_ref,
                      ka_ref, va_ref, kb_ref, vb_ref, *, wa, wb):
    xn = _rms_rows(h_ref[...], g_ref[...]).astype(BF16)
    proj = _dot(xn, w_ref[...])
    gmat = gmat_ref[...]
    qa = proj[:, 0:wa]
    ka = proj[:, wa:2 * wa]
    va = proj[:, 2 * wa:3 * wa]
    o = 3 * wa
    qb = _head_rms(proj[:, o:o + wb], gmat, gq_ref[...])
    kb = _head_rms(proj[:, o + wb:o + 2 * wb], gmat, gk_ref[...])
    vb = proj[:, o + 2 * wb:o + 3 * wb]
    qa_ref[...] = (qa * SCALE).astype(BF16)
    ka16_ref[...] = ka.astype(BF16)
    va16_ref[...] = va.astype(BF16)
    qb_ref[...] = (qb * SCALE).astype(BF16)
    kb16_ref[...] = kb.astype(BF16)
    vb16_ref[...] = vb.astype(BF16)
    ka_ref[...] = ka.T
    va_ref[...] = va.T
    kb_ref[...] = kb.T
    vb_ref[...] = vb.T


def _proj_odd_kernel(h_ref, g_ref, w_ref, gq_ref, gk_ref, gmat_ref, bf_ref,
                     q_ref, k16_ref, v16_ref, k_ref, v_ref, lf_ref, *, wc):
    xn = _rms_rows(h_ref[...], g_ref[...]).astype(BF16)
    proj = _dot(xn, w_ref[...])
    gmat = gmat_ref[...]
    q = _head_rms(proj[:, 0:wc], gmat, gq_ref[...])
    k = _head_rms(proj[:, wc:2 * wc], gmat, gk_ref[...])
    v = proj[:, 2 * wc:3 * wc]
    f = proj[:, 3 * wc:3 * wc + LANES]
    q_ref[...] = (q * SCALE).astype(BF16)
    k16_ref[...] = k.astype(BF16)
    v16_ref[...] = v.astype(BF16)
    k_ref[...] = k.T
    v_ref[...] = v.T
    lf_ref[...] = _log_sigmoid(f + bf_ref[...])


def _post1_kernel(h_ref, mix_ref, wo_ref, g_ref, w1_ref, h1_ref, hid_ref, *, ff_chunk):
    h1 = h_ref[...] + _dot(mix_ref[...], wo_ref[...])
    h1_ref[...] = h1
    xn = _rms_rows(h1, g_ref[...]).astype(BF16)
    d_ff = w1_ref.shape[1]
    for c in range(d_ff // ff_chunk):
        sl = slice(c * ff_chunk, (c + 1) * ff_chunk)
        a = jnp.maximum(_dot(xn, w1_ref[:, sl]), 0.0)
        hid_ref[:, sl] = (a * a).astype(BF16)


def _post2_kernel(h1_ref, hid_ref, w2_ref, g_ref, wg_ref, p_ref, wp_ref, out_ref):
    h2 = h1_ref[...] + _dot(hid_ref[...], w2_ref[...])
    xg = _rms_rows(h2, g_ref[...]).astype(BF16)
    gate = jax.nn.sigmoid(_dot(xg, wg_ref[...]))
    out_ref[...] = h2 + gate * _dot(p_ref[...].astype(BF16), wp_ref[...])


def _row_spec(tm, width):
    return pl.BlockSpec((tm, width), lambda i: (i, 0))


def _col_spec(tm, width):
    return pl.BlockSpec((width, tm), lambda i: (0, i))


def _full_spec(shape):
    return pl.BlockSpec(shape, lambda i: (0,) * len(shape))


ROW_TILE = 256


def _row_tile(rows):
    assert rows % ROW_TILE == 0, rows
    return ROW_TILE


def _proj_even(h, g, w, gq_row, gk_row, gmat, wa, wb):
    rows, d = h.shape
    tm = _row_tile(rows)
    widths = [wa, wa, wa, wb, wb, wb, wa, wa, wb, wb]
    dtypes = [BF16] * 6 + [F32] * 4
    transposed = [False] * 6 + [True] * 4
    return pl.pallas_call(
        functools.partial(_proj_even_kernel, wa=wa, wb=wb),
        grid=(rows // tm,),
        in_specs=[_row_spec(tm, d), _full_spec(g.shape), _full_spec(w.shape),
                  _full_spec(gq_row.shape), _full_spec(gk_row.shape), _full_spec(gmat.shape)],
        out_specs=[_col_spec(tm, wd) if tr else _row_spec(tm, wd) for wd, tr in zip(widths, transposed)],
        out_shape=[jax.ShapeDtypeStruct((wd, rows) if tr else (rows, wd), dt)
                   for wd, dt, tr in zip(widths, dtypes, transposed)],
        compiler_params=_cparams("parallel"),
        name="proj_even",
    )(h, g, w, gq_row, gk_row, gmat)


def _proj_odd(h, g, w, gq_row, gk_row, gmat, bf_row, wc):
    rows, d = h.shape
    tm = _row_tile(rows)
    widths = [wc, wc, wc, wc, wc, LANES]
    dtypes = [BF16] * 3 + [F32] * 3
    transposed = [False] * 3 + [True, True, False]
    return pl.pallas_call(
        functools.partial(_proj_odd_kernel, wc=wc),
        grid=(rows // tm,),
        in_specs=[_row_spec(tm, d), _full_spec(g.shape), _full_spec(w.shape),
                  _full_spec(gq_row.shape), _full_spec(gk_row.shape), _full_spec(gmat.shape),
                  _full_spec(bf_row.shape)],
        out_specs=[_col_spec(tm, wd) if tr else _row_spec(tm, wd) for wd, tr in zip(widths, transposed)],
        out_shape=[jax.ShapeDtypeStruct((wd, rows) if tr else (rows, wd), dt)
                   for wd, dt, tr in zip(widths, dtypes, transposed)],
        compiler_params=_cparams("parallel"),
        name="proj_odd",
    )(h, g, w, gq_row, gk_row, gmat, bf_row)


def _post(h, mix, wo, g_mlp, w1, w2, g_ple, wg, p, wp):
    rows, d = h.shape
    tm = _row_tile(rows)
    d_ff = w1.shape[1]
    ff_chunk = 1024 if d_ff % 1024 == 0 else d_ff
    h1, hid = pl.pallas_call(
        functools.partial(_post1_kernel, ff_chunk=ff_chunk),
        grid=(rows // tm,),
        in_specs=[_row_spec(tm, d), _row_spec(tm, mix.shape[1]), _full_spec(wo.shape),
                  _full_spec(g_mlp.shape), _full_spec(w1.shape)],
        out_specs=[_row_spec(tm, d), _row_spec(tm, d_ff)],
        out_shape=[jax.ShapeDtypeStruct((rows, d), F32), jax.ShapeDtypeStruct((rows, d_ff), BF16)],
        compiler_params=_cparams("parallel"),
        name="post_attn_mlp_up",
    )(h, mix, wo, g_mlp, w1)
    return pl.pallas_call(
        _post2_kernel,
        grid=(rows // tm,),
        in_specs=[_row_spec(tm, d), _row_spec(tm, d_ff), _full_spec(w2.shape), _full_spec(g_ple.shape),
                  _full_spec(wg.shape), _row_spec(tm, p.shape[1]), _full_spec(wp.shape)],
        out_specs=_row_spec(tm, d),
        out_shape=jax.ShapeDtypeStruct((rows, d), F32),
        compiler_params=_cparams("parallel"),
        name="mlp_down_ple",
    )(h1, hid, w2, g_ple, wg, p, wp)


def _cumsum_kernel(x_ref, o_ref):
    gs, nb, _ = x_ref.shape
    x2 = x_ref[...].reshape(gs * nb, LANES)
    r = lax.broadcasted_iota(jnp.int32, (LANES, LANES), 0)
    c = lax.broadcasted_iota(jnp.int32, (LANES, LANES), 1)
    upper = jnp.where(r <= c, 1.0, 0.0).astype(BF16)
    ones = jnp.ones((LANES, LANES), BF16)
    within = _dot_split3(x2, upper).reshape(gs, nb, LANES)
    total = _dot_split3(x2, ones).reshape(gs, nb, LANES)
    rb = lax.broadcasted_iota(jnp.int32, (nb, nb), 0)
    cb = lax.broadcasted_iota(jnp.int32, (nb, nb), 1)
    strict_lower = jnp.where(cb < rb, 1.0, 0.0).astype(BF16)
    for g in range(gs):
        hi, mid, lo = _split3(total[g])
        offs = (_dot(strict_lower, hi) + _dot(strict_lower, mid)) + _dot(strict_lower, lo)
        o_ref[g] = within[g] + offs


def _cumsum_lanes(x):
    n_in = x.shape[1]
    x = _pad_to(x, LANES * LANES, 1)
    groups, n = x.shape
    nb = n // LANES
    gs = min(groups, 16)
    out = pl.pallas_call(
        _cumsum_kernel,
        grid=(groups // gs,),
        in_specs=[pl.BlockSpec((gs, nb, LANES), lambda i: (i, 0, 0))],
        out_specs=pl.BlockSpec((gs, nb, LANES), lambda i: (i, 0, 0)),
        out_shape=jax.ShapeDtypeStruct((groups, nb, LANES), F32),
        compiler_params=_cparams("parallel"),
        name="logf_cumsum",
    )(x.reshape(groups, nb, LANES))
    return out.reshape(groups, n)[:, :n_in]


def _neg_suffix_matrix(tk):
    r = lax.broadcasted_iota(jnp.int32, (tk, tk), 0)
    c = lax.broadcasted_iota(jnp.int32, (tk, tk), 1)
    return jnp.where(r >= c, -1.0, 0.0).astype(BF16)


def _stick_tile(qrows, kb, vb, nsuf, carry, acc, mask, feature_major=False):
    z = _dot(qrows, kb) if feature_major else _dot_nt(qrows, kb)
    sp = _softplus(z)
    if mask is not None:
        sp = jnp.where(mask, sp, 0.0)
    incl = _dot_split2(sp, nsuf)
    w = jnp.exp(z + incl + carry)
    if mask is not None:
        w = jnp.where(mask, w, 0.0)
    wb = w.astype(BF16)
    acc = acc + (_dot_nt(wb, vb) if feature_major else _dot(wb, vb))
    carry = carry + incl[:, 0:1]
    return carry, acc


def _softmax_tile(logits, vb, m, l, acc, feature_major=False):
    m_new = jnp.maximum(m, jnp.max(logits, axis=1, keepdims=True))
    alpha = jnp.exp(m - m_new)
    p = jnp.exp(logits - m_new)
    l = alpha * l + jnp.sum(p, axis=1, keepdims=True)
    pb = p.astype(BF16)
    acc = alpha * acc + (_dot_nt(pb, vb) if feature_major else _dot(pb, vb))
    return m_new, l, acc


def _stack_heads(q, n_heads):
    t, width = q.shape
    rep = jnp.concatenate([q] * n_heads, axis=0) if n_heads > 1 else q
    row = lax.broadcasted_iota(jnp.int32, (n_heads * t, width), 0)
    lane = lax.broadcasted_iota(jnp.int32, (n_heads * t, width), 1)
    lo = (row // t) * HEAD_DIM
    keep = (lane >= lo) & (lane < lo + HEAD_DIM)
    return jnp.where(keep, rep, jnp.zeros_like(rep))


def _unstack_heads(acc, n_heads):
    rows, width = acc.shape
    t = rows // n_heads
    lane = lax.broadcasted_iota(jnp.int32, (t, width), 1)
    out = jnp.zeros((t, width), acc.dtype)
    for h in range(n_heads):
        sel = (lane >= h * HEAD_DIM) & (lane < (h + 1) * HEAD_DIM)
        out = jnp.where(sel, acc[h * t:(h + 1) * t, :], out)
    return out


def _row_query_index(rows, cols, t):
    r = lax.broadcasted_iota(jnp.int32, (rows, cols), 0)
    c = lax.broadcasted_iota(jnp.int32, (rows, cols), 1)
    return r % t, c


def _stick_prompt_kernel(q_ref, k_ref, v_ref, o_ref, carry_ref, acc_ref, *, t):
    i = pl.program_id(1)
    nh = HEADS_PER_SLAB
    qrows = _stack_heads(q_ref[...], nh)
    nsuf = _neg_suffix_matrix(t)
    tq_idx, c_idx = _row_query_index(nh * t, t, t)
    start = pl.multiple_of(i * t, t)
    carry, acc = _stick_tile(qrows, k_ref[pl.ds(start, t), :], v_ref[pl.ds(start, t), :], nsuf,
                             jnp.zeros((nh * t, 1), F32), jnp.zeros((nh * t, LANES), F32),
                             c_idx < tq_idx)
    carry_ref[...] = carry
    acc_ref[...] = acc

    def cond(state):
        j, carry_max = state
        return jnp.logical_and(j >= 0, carry_max > EXP_ZERO_BELOW)

    def body(state):
        j, _ = state
        s = pl.multiple_of(j * t, t)
        carry, acc = _stick_tile(qrows, k_ref[pl.ds(s, t), :], v_ref[pl.ds(s, t), :], nsuf,
                                 carry_ref[...], acc_ref[...], None)
        carry_ref[...] = carry
        acc_ref[...] = acc
        return j - 1, jnp.max(carry)

    lax.while_loop(cond, body, (i - 1, jnp.max(carry)))
    o_ref[...] = _unstack_heads(acc_ref[...], nh).astype(o_ref.dtype)


def _stick_prompt(q, k, v, n, t=256):
    width = q.shape[1]
    t = min(t, n)
    slabs = width // LANES
    return pl.pallas_call(
        functools.partial(_stick_prompt_kernel, t=t),
        grid=(slabs, n // t),
        in_specs=[pl.BlockSpec((t, LANES), lambda p, i: (i, p)),
                  pl.BlockSpec((n, LANES), lambda p, i: (0, p)),
                  pl.BlockSpec((n, LANES), lambda p, i: (0, p))],
        out_specs=pl.BlockSpec((t, LANES), lambda p, i: (i, p)),
        out_shape=jax.ShapeDtypeStruct((n, width), BF16),
        scratch_shapes=[pltpu.VMEM((HEADS_PER_SLAB * t, 1), F32),
                        pltpu.VMEM((HEADS_PER_SLAB * t, LANES), F32)],
        compiler_params=_cparams("parallel", "parallel"),
        name="stick_prompt",
    )(q, k, v)


def _stick_sample_kernel(q_ref, kn_ref, vn_ref, ck_ref, cv_ref, nsuf_ref, o_ref, qrows_ref, carry_ref, acc_ref,
                         *, nh):
    jj = pl.program_id(1)
    t = q_ref.shape[1]

    @pl.when(jj == 0)
    def _():
        qrows = _stack_heads(q_ref[0], nh)
        qrows_ref[...] = qrows
        tq_idx, c_idx = _row_query_index(nh * t, t, t)
        carry, acc = _stick_tile(qrows, kn_ref[0], vn_ref[0], _neg_suffix_matrix(t),
                                 jnp.zeros((nh * t, 1), F32), jnp.zeros(acc_ref.shape, F32),
                                 c_idx < tq_idx)
        carry_ref[...] = carry
        acc_ref[...] = acc

    carry, acc = _stick_tile(qrows_ref[...], ck_ref[...].astype(BF16), cv_ref[...].astype(BF16),
                             nsuf_ref[...], carry_ref[...], acc_ref[...], None, feature_major=True)
    carry_ref[...] = carry
    acc_ref[...] = acc

    @pl.when(jj == pl.num_programs(1) - 1)
    def _():
        o_ref[0] = _unstack_heads(acc_ref[...], nh).astype(o_ref.dtype)


def _feature_major(cache):
    layers, b, past, heads, hd = cache.shape
    return jnp.transpose(cache, (0, 1, 3, 4, 2)).reshape(layers, b, heads * hd, past)


def _stick_sample(q, kn, vn, cache_kt, cache_vt, layer, tk=512):
    b, t, width = q.shape
    past = cache_kt.shape[3]
    tk = min(tk, past)
    nblk = past // tk
    nh = width // HEAD_DIM
    r = jnp.arange(tk)
    nsuf = jnp.where(r[:, None] >= r[None, :], -1.0, 0.0).astype(BF16)
    new_spec = pl.BlockSpec((1, t, width), lambda bi, jj: (bi, 0, 0))
    cache_spec = pl.BlockSpec((None, None, width, tk), lambda bi, jj: (layer, bi, 0, nblk - 1 - jj))
    return pl.pallas_call(
        functools.partial(_stick_sample_kernel, nh=nh),
        grid=(b, nblk),
        in_specs=[new_spec, new_spec, new_spec, cache_spec, cache_spec,
                  pl.BlockSpec((tk, tk), lambda bi, jj: (0, 0))],
        out_specs=pl.BlockSpec((1, t, width), lambda bi, jj: (bi, 0, 0)),
        out_shape=jax.ShapeDtypeStruct((b, t, width), BF16),
        scratch_shapes=[pltpu.VMEM((nh * t, width), BF16),
                        pltpu.VMEM((nh * t, 1), F32),
                        pltpu.VMEM((nh * t, width), F32)],
        compiler_params=_cparams("parallel", "arbitrary"),
        name="stick_sample",
    )(q, kn, vn, cache_kt, cache_vt, nsuf)


def _band_prompt_kernel(q_ref, k0_ref, k1_ref, k2_ref, v0_ref, v1_ref, v2_ref, bias_ref, o_ref,
                        *, t, nprev, fixed_shift):
    i = pl.program_id(1)
    nh = HEADS_PER_SLAB
    qrows = _stack_heads(q_ref[...], nh)
    krefs = (k0_ref, k1_ref, k2_ref)[3 - (nprev + 1):]
    vrefs = (v0_ref, v1_ref, v2_ref)[3 - (nprev + 1):]
    kcat = jnp.concatenate([r[...] for r in krefs], axis=0)
    vcat = jnp.concatenate([r[...] for r in vrefs], axis=0)
    logits = _dot_nt(qrows, kcat) + bias_ref[0]
    c_idx = lax.broadcasted_iota(jnp.int32, logits.shape, 1)
    logits = jnp.where(c_idx >= (nprev - i) * t, logits, MASKED)
    if fixed_shift:
        v_ones = jnp.concatenate([vcat, jnp.ones(vcat.shape, BF16)], axis=1)
        acc = _dot(jnp.exp(logits).astype(BF16), v_ones)
        acc = acc[:, :LANES] / acc[:, LANES:]
    else:
        m = jnp.max(logits, axis=1, keepdims=True)
        p = jnp.exp(logits - m)
        l = jnp.sum(p, axis=1, keepdims=True)
        acc = _dot(p.astype(BF16), vcat) / l
    o_ref[...] = _unstack_heads(acc, nh).astype(o_ref.dtype)


def _rel_bias(rel_table, nq, nk, q0, k0):
    m_len = nq + nk - 1
    u = np.arange(m_len)
    d = np.where(u < nk, (q0 - k0) - u, (q0 - k0) + (m_len - u))
    g = jnp.take(rel_table.astype(F32), np.clip(d, -REL_CLIP, REL_CLIP) + REL_CLIP, axis=1)
    n_heads = g.shape[0]
    bias = jnp.tile(g, (1, nq))[:, :nq * (m_len - 1)].reshape(n_heads, nq, m_len - 1)[:, :, :nk]
    qp = q0 + np.arange(nq)[:, None]
    kp = k0 + np.arange(nk)[None, :]
    qc, kc = qp // CHUNK, kp // CHUNK
    visible = (kp >= 0) & (kc <= qc) & (kc >= qc - BAND_CHUNKS)
    return bias, jnp.asarray(visible)[None]


def _band_prompt(q, k, v, rel_table, n, logit_bound, t=256):
    width = q.shape[1]
    t = min(t, n)
    assert t % CHUNK == 0
    nprev = min(-(-BAND // t), 2)
    assert nprev * t >= BAND
    slabs = width // LANES
    bias, visible = _rel_bias(rel_table, t, (nprev + 1) * t, nprev * t, 0)
    bias_shape = (slabs, HEADS_PER_SLAB * t, (nprev + 1) * t)

    def kv_spec(back):
        return pl.BlockSpec((t, LANES), lambda p, i: (jnp.maximum(i - back, 0), p))

    def call(masked_bias, fixed_shift):
        return pl.pallas_call(
            functools.partial(_band_prompt_kernel, t=t, nprev=nprev, fixed_shift=fixed_shift),
            grid=(slabs, n // t),
            in_specs=[pl.BlockSpec((t, LANES), lambda p, i: (i, p)),
                      kv_spec(2), kv_spec(1), kv_spec(0), kv_spec(2), kv_spec(1), kv_spec(0),
                      pl.BlockSpec((1,) + bias_shape[1:], lambda p, i: (p, 0, 0))],
            out_specs=pl.BlockSpec((t, LANES), lambda p, i: (i, p)),
            out_shape=jax.ShapeDtypeStruct((n, width), BF16),
            compiler_params=_cparams("parallel", "parallel"),
            name="band_prompt" if fixed_shift else "band_prompt_row_max",
        )(q, k, k, k, v, v, v, masked_bias.reshape(bias_shape))

    tab_max, tab_min = jnp.max(rel_table), jnp.min(rel_table)
    shift = logit_bound + tab_max
    spread = 2.0 * logit_bound + (tab_max - tab_min)
    return lax.cond(spread < 2.0 * MAX_FIXED_SHIFT,
                    lambda: call(jnp.where(visible, bias - shift, MASKED), True),
                    lambda: call(jnp.where(visible, bias, MASKED), False))


def _band_sample_kernel(q_ref, kn_ref, vn_ref, ck_ref, cv_ref, bc_ref, bn_ref, o_ref, *, nh):
    qrows = _stack_heads(q_ref[0], nh)
    lc = _dot(qrows, ck_ref[...].astype(BF16)) + bc_ref[...]
    ln = _dot_nt(qrows, kn_ref[0]) + bn_ref[...]
    m = jnp.maximum(jnp.max(lc, axis=1, keepdims=True), jnp.max(ln, axis=1, keepdims=True))
    pc = jnp.exp(lc - m)
    pn = jnp.exp(ln - m)
    l = jnp.sum(pc, axis=1, keepdims=True) + jnp.sum(pn, axis=1, keepdims=True)
    acc = (_dot_nt(pc.astype(BF16), cv_ref[...].astype(BF16)) + _dot(pn.astype(BF16), vn_ref[0])) / l
    o_ref[0] = _unstack_heads(acc, nh).astype(o_ref.dtype)


def _band_sample(q, kn, vn, cache_kt, cache_vt, layer, rel_table, past_len):
    b, t, width = q.shape
    buf_len = cache_kt.shape[3]
    nh = width // HEAD_DIM
    def masked_bias(nk, k0):
        bias, visible = _rel_bias(rel_table, t, nk, past_len, k0)
        return jnp.where(visible, bias, MASKED).reshape(nh * t, nk)

    bias_c = masked_bias(buf_len, past_len - buf_len)
    bias_n = masked_bias(t, past_len)
    new_spec = pl.BlockSpec((1, t, width), lambda bi: (bi, 0, 0))
    cache_spec = pl.BlockSpec((None, None, width, buf_len), lambda bi: (layer, bi, 0, 0))
    return pl.pallas_call(
        functools.partial(_band_sample_kernel, nh=nh),
        grid=(b,),
        in_specs=[new_spec, new_spec, new_spec, cache_spec, cache_spec,
                  _full_spec(bias_c.shape), _full_spec(bias_n.shape)],
        out_specs=pl.BlockSpec((1, t, width), lambda bi: (bi, 0, 0)),
        out_shape=jax.ShapeDtypeStruct((b, t, width), BF16),
        compiler_params=_cparams("parallel"),
        name="band_sample",
    )(q, kn, vn, cache_kt, cache_vt, bias_c, bias_n)


def _key_decay_rows(dk, t):
    n_heads, tk = dk.shape
    return jnp.concatenate([jnp.broadcast_to(dk[h:h + 1, :], (t, tk)) for h in range(n_heads)], axis=0)


def _fox_prompt_kernel(q_ref, dq_ref, k_ref, v_ref, dk_ref, o_ref, m_ref, l_ref, acc_ref, *, t):
    i = pl.program_id(1)
    nh = HEADS_PER_SLAB
    qrows = _stack_heads(q_ref[...], nh)
    dq = jnp.concatenate([dq_ref[:, h:h + 1] for h in range(nh)], axis=0)
    m_ref[...] = jnp.full(m_ref.shape, MASKED, F32)
    l_ref[...] = jnp.zeros(l_ref.shape, F32)
    acc_ref[...] = jnp.zeros(acc_ref.shape, F32)

    def tile(j, mask):
        s = pl.multiple_of(j * t, t)
        z = _dot_nt(qrows, k_ref[pl.ds(s, t), :])
        logits = (z + dq) - _key_decay_rows(dk_ref[:, pl.ds(s, t)], t)
        if mask is not None:
            logits = jnp.where(mask, logits, MASKED)
        m, l, acc = _softmax_tile(logits, v_ref[pl.ds(s, t), :], m_ref[...], l_ref[...], acc_ref[...])
        m_ref[...] = m
        l_ref[...] = l
        acc_ref[...] = acc

    def body(j, carry):
        tile(j, None)
        return carry

    lax.fori_loop(0, i, body, 0)
    tq_idx, c_idx = _row_query_index(nh * t, t, t)
    tile(i, c_idx <= tq_idx)
    o_ref[...] = _unstack_heads(acc_ref[...] / l_ref[...], nh).astype(o_ref.dtype)


def _fox_prompt_bounded_kernel(shift_ref, first_ref, q_ref, dq_ref, k_ref, v_ref, dk_ref, o_ref,
                               dqrep_ref, acc_ref, z_ref, *, tq, tk):
    slab = pl.program_id(0)
    i = pl.program_id(1)
    ratio = tq // tk
    q = q_ref[...]
    ones = jnp.ones((tk, LANES), BF16)
    row = lax.broadcasted_iota(jnp.int32, (tq, LANES), 0)
    lane = lax.broadcasted_iota(jnp.int32, (tq, LANES), 1)
    outs = []
    for hh in range(HEADS_PER_SLAB):
        qm = jnp.where((lane >= hh * HEAD_DIM) & (lane < (hh + 1) * HEAD_DIM), q, jnp.zeros_like(q))
        dqrep_ref[...] = jnp.broadcast_to(dq_ref[:, hh:hh + 1] - shift_ref[0, 0], (tq, LANES))
        acc_ref[...] = jnp.zeros(acc_ref.shape, F32)

        def scores(j, qm=qm):
            s = pl.multiple_of(j * tk, tk)
            return _dot_nt(qm, k_ref[pl.ds(s, tk), :])

        def tile(j, causal, prefetch, hh=hh, scores=scores):
            s = pl.multiple_of(j * tk, tk)
            z = z_ref[...]
            dqh = dqrep_ref[...]
            cols = []
            for c in range(tk // LANES):
                lg = (z[:, c * LANES:(c + 1) * LANES] + dqh) - dk_ref[hh:hh + 1, pl.ds(s + c * LANES, LANES)]
                if causal:
                    lg = jnp.where(s + c * LANES + lane <= i * tq + row, lg, MASKED)
                cols.append(jnp.exp(lg).astype(BF16))
            if prefetch:
                z_next = scores(j + 1)
            v_ones = jnp.concatenate([v_ref[pl.ds(s, tk), :], ones], axis=1)
            acc_ref[...] += _dot(jnp.concatenate(cols, axis=1), v_ones)
            if prefetch:
                z_ref[...] = z_next

        start = first_ref[slab * HEADS_PER_SLAB + hh, i]
        z_ref[...] = scores(start)

        def body(j, carry, tile=tile):
            tile(j, False, True)
            return carry

        lax.fori_loop(start, i * ratio, body, 0)
        for d in range(ratio):
            tile(i * ratio + d, True, d + 1 < ratio)
        acc = acc_ref[...]
        outs.append(acc[:, :LANES] / acc[:, LANES:])
    o_ref[...] = jnp.where(lane < HEAD_DIM, outs[0], outs[1]).astype(o_ref.dtype)


MAX_FIXED_SHIFT = 30.0


def _fox_prompt(q, k, v, cum, n, logit_bound, t_exact=256, tq_bounded=1024, tk_bounded=512):
    width = q.shape[1]
    slabs = width // LANES
    nh = HEADS_PER_SLAB
    dq = cum.reshape(n, slabs, nh).transpose(1, 0, 2)
    dk = cum.reshape(n, slabs, nh).transpose(1, 2, 0)

    def specs(t):
        return ([pl.BlockSpec((t, LANES), lambda p, i: (i, p)),
                 pl.BlockSpec((None, t, nh), lambda p, i: (p, i, 0)),
                 pl.BlockSpec((n, LANES), lambda p, i: (0, p)),
                 pl.BlockSpec((n, LANES), lambda p, i: (0, p)),
                 pl.BlockSpec((None, nh, n), lambda p, i: (p, 0, 0))],
                pl.BlockSpec((t, LANES), lambda p, i: (i, p)))

    def exact(_):
        t = min(t_exact, n)
        in_specs, out_spec = specs(t)
        return pl.pallas_call(
            functools.partial(_fox_prompt_kernel, t=t),
            grid=(slabs, n // t),
            in_specs=in_specs, out_specs=out_spec,
            out_shape=jax.ShapeDtypeStruct((n, width), BF16),
            scratch_shapes=[pltpu.VMEM((nh * t, 1), F32), pltpu.VMEM((nh * t, 1), F32),
                            pltpu.VMEM((nh * t, LANES), F32)],
            compiler_params=_cparams("parallel", "parallel"),
            name="fox_prompt_running_max",
        )(q, dq, k, v, dk)

    def bounded(shift):
        tq = min(tq_bounded, n)
        tk = min(tk_bounded, n)
        in_specs, out_spec = specs(tq)
        cum_t = cum.T
        decay = cum_t[:, ::tq][:, :, None] - cum_t[:, tk - 1::tk][:, None, :]
        before = jnp.arange(n // tk)[None, None, :] < (jnp.arange(n // tq) * (tq // tk))[None, :, None]
        dead = jnp.logical_and(decay < EXP_ZERO_BELOW, before)
        first = jnp.sum(jnp.cumprod(dead.astype(jnp.int32), axis=2), axis=2).astype(jnp.int32)
        smem = pl.BlockSpec(memory_space=pltpu.SMEM)
        return pl.pallas_call(
            functools.partial(_fox_prompt_bounded_kernel, tq=tq, tk=tk),
            grid=(slabs, n // tq),
            in_specs=[smem, smem] + in_specs, out_specs=out_spec,
            out_shape=jax.ShapeDtypeStruct((n, width), BF16),
            scratch_shapes=[pltpu.VMEM((tq, LANES), F32), pltpu.VMEM((tq, 2 * LANES), F32),
                            pltpu.VMEM((tq, tk), F32)],
            compiler_params=_cparams("parallel", "parallel"),
            name="fox_prompt",
        )(shift.reshape(1, 1), first, q, dq, k, v, dk)

    return lax.cond(logit_bound < MAX_FIXED_SHIFT, bounded, exact, logit_bound)


def _fox_sample_kernel(q_ref, dq_ref, kn_ref, vn_ref, dkn_ref, ck_ref, cv_ref, dkc_ref, o_ref,
                       qrows_ref, m_ref, l_ref, acc_ref, *, nh):
    jj = pl.program_id(1)
    t = q_ref.shape[1]
    dq = dq_ref[0]

    @pl.when(jj == 0)
    def _():
        qrows = _stack_heads(q_ref[0], nh)
        qrows_ref[...] = qrows
        tq_idx, c_idx = _row_query_index(nh * t, t, t)
        logits = (_dot_nt(qrows, kn_ref[0]) + dq) - _key_decay_rows(dkn_ref[0], t)
        logits = jnp.where(c_idx <= tq_idx, logits, MASKED)
        m, l, acc = _softmax_tile(logits, vn_ref[0], jnp.full(m_ref.shape, MASKED, F32),
                                  jnp.zeros(l_ref.shape, F32), jnp.zeros(acc_ref.shape, F32))
        m_ref[...] = m
        l_ref[...] = l
        acc_ref[...] = acc

    logits = (_dot(qrows_ref[...], ck_ref[...].astype(BF16)) + dq) - _key_decay_rows(dkc_ref[0], t)
    m, l, acc = _softmax_tile(logits, cv_ref[...].astype(BF16), m_ref[...], l_ref[...], acc_ref[...],
                              feature_major=True)
    m_ref[...] = m
    l_ref[...] = l
    acc_ref[...] = acc

    @pl.when(jj == pl.num_programs(1) - 1)
    def _():
        o_ref[0] = _unstack_heads(acc_ref[...] / l_ref[...], nh).astype(o_ref.dtype)


def _fox_sample(q, kn, vn, cache_kt, cache_vt, layer, cum_cache, cum_new, tk=512):
    b, t, width = q.shape
    past = cache_kt.shape[3]
    tk = min(tk, past)
    nblk = past // tk
    nh = width // HEAD_DIM
    dq = cum_new.reshape(b, nh * t, 1)
    new_spec = pl.BlockSpec((1, t, width), lambda bi, jj: (bi, 0, 0))
    cache_spec = pl.BlockSpec((None, None, width, tk), lambda bi, jj: (layer, bi, 0, jj))
    return pl.pallas_call(
        functools.partial(_fox_sample_kernel, nh=nh),
        grid=(b, nblk),
        in_specs=[new_spec, pl.BlockSpec((1, nh * t, 1), lambda bi, jj: (bi, 0, 0)),
                  new_spec, new_spec, pl.BlockSpec((1, nh, t), lambda bi, jj: (bi, 0, 0)),
                  cache_spec, cache_spec, pl.BlockSpec((1, nh, tk), lambda bi, jj: (bi, 0, jj))],
        out_specs=pl.BlockSpec((1, t, width), lambda bi, jj: (bi, 0, 0)),
        out_shape=jax.ShapeDtypeStruct((b, t, width), BF16),
        scratch_shapes=[pltpu.VMEM((nh * t, width), BF16), pltpu.VMEM((nh * t, 1), F32),
                        pltpu.VMEM((nh * t, 1), F32), pltpu.VMEM((nh * t, width), F32)],
        compiler_params=_cparams("parallel", "arbitrary"),
        name="fox_sample",
    )(q, dq, kn, vn, cum_new, cache_kt, cache_vt, cum_cache)


def _pad_to(x, multiple, axis):
    size = x.shape[axis]
    target = -(-size // multiple) * multiple
    if target == size:
        return x
    pad = [(0, 0)] * x.ndim
    pad[axis] = (0, target - size)
    return jnp.pad(x, pad)


def kernel(x_prompt, x_sample, cache_a_k, cache_a_v, cache_b_k, cache_b_v, cache_c_k, cache_c_v, cache_c_logf, p_prompt, p_sample, g_mix, w_in_even, g_qb, g_kb, rel_bias, w_in_odd, b_forget, g_qc, g_kc, w_out, g_mlp, w_ff1, w_ff2, g_ple, w_ple_gate, w_ple_proj):
    bp, n, d = x_prompt.shape
    bs, ts, _ = x_sample.shape
    assert bp == 1, "prompt kernels assume a single prompt stream"
    depth = g_mix.shape[0]
    past = cache_a_k.shape[2]
    buf_len = cache_b_k.shape[2]
    ha, hb, hc = cache_a_k.shape[3], cache_b_k.shape[3], cache_c_k.shape[3]
    wa, wb, wc = ha * HEAD_DIM, hb * HEAD_DIM, hc * HEAD_DIM
    rows_s = bs * ts

    blk = jnp.arange(NORM_SLAB) // HEAD_DIM
    gmat = jnp.where(blk[:, None] == blk[None, :], 1.0 / HEAD_DIM, 0.0).astype(BF16)

    cache_a_kt, cache_a_vt = _feature_major(cache_a_k), _feature_major(cache_a_v)
    cache_b_kt, cache_b_vt = _feature_major(cache_b_k), _feature_major(cache_b_v)
    cache_c_kt, cache_c_vt = _feature_major(cache_c_k), _feature_major(cache_c_v)

    rows = n + rows_s
    tail = -rows % ROW_TILE
    h = jnp.concatenate([x_prompt.reshape(n, d), x_sample.reshape(rows_s, d), jnp.zeros((tail, d), F32)], axis=0)
    d_ple = p_prompt.shape[-1]
    p_all = jnp.concatenate([p_prompt.reshape(depth, n, d_ple), p_sample.reshape(depth, rows_s, d_ple),
                             jnp.zeros((depth, tail, d_ple), F32)], axis=1)

    def prompt_state(xt, heads, lo=0):
        return jnp.transpose(xt[:, lo:n].reshape(heads, HEAD_DIM, n - lo), (2, 0, 1))[None]

    def sample_state(xt, heads):
        return jnp.transpose(xt[:, n:rows].reshape(heads, HEAD_DIM, bs, ts), (2, 3, 0, 1))

    st = {name: [] for name in ("pa_k", "pa_v", "pb_k", "pb_v", "pc_k", "pc_v", "pc_f",
                                "sa_k", "sa_v", "sb_k", "sb_v", "sc_k", "sc_v", "sc_f")}
    for i in range(depth):
        j = i // 2
        g_row = g_mix[i].reshape(1, d)
        if i % 2 == 0:
            gq_row = jnp.tile(g_qb[j], hb).reshape(1, wb)
            gk_row = jnp.tile(g_kb[j], hb).reshape(1, wb)
            qa, ka16, va16, qb, kb16, vb16, ka, va, kb, vb = _proj_even(
                h, g_row, w_in_even[j].astype(BF16), gq_row, gk_row, gmat, wa, wb)
            mix_a_p = _stick_prompt(qa, ka16, va16, n)
            band_bound = 1.01 * HEAD_DIM * SCALE * jnp.max(jnp.abs(g_qb[j])) * jnp.max(jnp.abs(g_kb[j]))
            mix_b_p = _band_prompt(qb, kb16, vb16, rel_bias[j], n, band_bound.astype(F32))
            s3 = lambda a: a[n:rows].reshape(bs, ts, a.shape[1])
            mix_a_s = _stick_sample(s3(qa), s3(ka16), s3(va16), cache_a_kt, cache_a_vt, j)
            mix_b_s = _band_sample(s3(qb), s3(kb16), s3(vb16), cache_b_kt, cache_b_vt, j, rel_bias[j], past)
            mix = jnp.concatenate([jnp.concatenate([mix_a_p, mix_b_p], axis=1),
                                   jnp.concatenate([mix_a_s.reshape(rows_s, wa), mix_b_s.reshape(rows_s, wb)],
                                                   axis=1),
                                   jnp.zeros((tail, wa + wb), BF16)], axis=0)
            st["pa_k"].append(prompt_state(ka, ha))
            st["pa_v"].append(prompt_state(va, ha))
            keep = min(BAND, n)
            st["pb_k"].append(prompt_state(kb, hb, n - keep))
            st["pb_v"].append(prompt_state(vb, hb, n - keep))
            st["sa_k"].append(sample_state(ka, ha))
            st["sa_v"].append(sample_state(va, ha))
            st["sb_k"].append(jnp.concatenate([cache_b_k[j], sample_state(kb, hb)], axis=1)[:, ts:])
            st["sb_v"].append(jnp.concatenate([cache_b_v[j], sample_state(vb, hb)], axis=1)[:, ts:])
        else:
            gq_row = jnp.tile(g_qc[j], hc).reshape(1, wc)
            gk_row = jnp.tile(g_kc[j], hc).reshape(1, wc)
            w_pad = _pad_to(w_in_odd[j], LANES, 1).astype(BF16)
            bf_row = _pad_to(b_forget[j].reshape(1, hc), LANES, 1)
            q, k16, v16, k, v, lf = _proj_odd(h, g_row, w_pad, gq_row, gk_row, gmat, bf_row, wc)
            log_f = lf[:, :hc]
            cum_p = _cumsum_lanes(log_f[:n].T)
            logit_bound = 1.01 * HEAD_DIM * SCALE * jnp.max(jnp.abs(g_qc[j])) * jnp.max(jnp.abs(g_kc[j]))
            mix_p = _fox_prompt(q, k16, v16, cum_p.T, n, logit_bound.astype(F32))
            lf_new = log_f[n:rows].reshape(bs, ts, hc).transpose(0, 2, 1)
            lf_all = jnp.concatenate([cache_c_logf[j].transpose(0, 2, 1), lf_new], axis=2)
            cum_s = _cumsum_lanes(lf_all.reshape(bs * hc, past + ts)).reshape(bs, hc, past + ts)
            s3 = lambda a: a[n:rows].reshape(bs, ts, a.shape[1])
            mix_s = _fox_sample(s3(q), s3(k16), s3(v16), cache_c_kt, cache_c_vt, j,
                                cum_s[:, :, :past], cum_s[:, :, past:past + ts])
            mix = jnp.concatenate([mix_p, mix_s.reshape(rows_s, wc), jnp.zeros((tail, wc), BF16)], axis=0)
            st["pc_k"].append(prompt_state(k, hc))
            st["pc_v"].append(prompt_state(v, hc))
            st["pc_f"].append(log_f[:n].reshape(1, n, hc))
            st["sc_k"].append(sample_state(k, hc))
            st["sc_v"].append(sample_state(v, hc))
            st["sc_f"].append(log_f[n:rows].reshape(bs, ts, hc))
        h = _post(h, mix, w_out[i].astype(BF16), g_mlp[i].reshape(1, d), w_ff1[i].astype(BF16),
                  w_ff2[i].astype(BF16), g_ple[i].reshape(1, d), w_ple_gate[i].astype(BF16),
                  p_all[i], w_ple_proj[i].astype(BF16))

    y_prompt = h[:n].reshape(1, n, d)
    y_sample = h[n:rows].reshape(bs, ts, d)
    stk = {name: jnp.stack(vals) for name, vals in st.items()}
    return (y_prompt, y_sample, stk["pa_k"], stk["pa_v"], stk["pb_k"], stk["pb_v"],
            stk["pc_k"], stk["pc_v"], stk["pc_f"], stk["sa_k"], stk["sa_v"], stk["sb_k"], stk["sb_v"],
            stk["sc_k"], stk["sc_v"], stk["sc_f"])
```

```python
import functools

import numpy as np
import jax
import jax.numpy as jnp
from jax import lax
from jax.experimental import pallas as pl
from jax.experimental.pallas import tpu as pltpu

F32 = jnp.float32
BF16 = jnp.bfloat16

HEAD_DIM = 64
CHUNK = 64
BAND_CHUNKS = 8
BAND = BAND_CHUNKS * CHUNK
REL_CLIP = 128
EPS = 1e-6
SCALE = HEAD_DIM ** -0.5
LANES = 128
HEADS_PER_SLAB = LANES // HEAD_DIM
NORM_SLAB = 256
EXP_ZERO_BELOW = -104.0
MASKED = -1e30
VMEM_LIMIT = 56 * 1024 * 1024


def _cparams(*sem):
    return pltpu.CompilerParams(dimension_semantics=sem, vmem_limit_bytes=VMEM_LIMIT)


def _split2(x):
    hi = x.astype(BF16)
    lo = (x - hi.astype(F32)).astype(BF16)
    return hi, lo


def _split3(x):
    hi = x.astype(BF16)
    r = x - hi.astype(F32)
    mid = r.astype(BF16)
    lo = (r - mid.astype(F32)).astype(BF16)
    return hi, mid, lo


def _dot(a, b):
    return jnp.dot(a, b, preferred_element_type=F32)


def _dot_nt(a, b):
    return lax.dot_general(a, b, (((1,), (1,)), ((), ())), preferred_element_type=F32)


def _dot_split2(x, m):
    hi, lo = _split2(x)
    return _dot(hi, m) + _dot(lo, m)


def _dot_split3(x, m):
    hi, mid, lo = _split3(x)
    return (_dot(hi, m) + _dot(mid, m)) + _dot(lo, m)


def _rms_rows(x, g):
    ms = jnp.mean(x * x, axis=-1, keepdims=True)
    return x * lax.rsqrt(ms + EPS) * g


def _head_rms(x, gmat, gain_row):
    width = x.shape[1]
    outs = []
    for s in range(width // NORM_SLAB):
        xs = x[:, NORM_SLAB * s:NORM_SLAB * (s + 1)]
        ms = _dot_split2(xs * xs, gmat)
        outs.append(xs * lax.rsqrt(ms + EPS))
    y = outs[0] if len(outs) == 1 else jnp.concatenate(outs, axis=1)
    return y * gain_row


def _softplus(z):
    return jnp.maximum(z, 0.0) + jnp.log(1.0 + jnp.exp(-jnp.abs(z)))


def _log_sigmoid(z):
    return jnp.minimum(z, 0.0) - jnp.log(1.0 + jnp.exp(-jnp.abs(z)))


def _store_state(values, prompt_refs, sample_refs, prompt_tiles):
    step = pl.program_id(0)

    @pl.when(step < prompt_tiles)
    def _():
        for value, ref in zip(values, prompt_refs):
            ref[...] = value.T

    @pl.when(step >= prompt_tiles)
    def _():
        for value, ref in zip(values, sample_refs):
            ref[...] = value.T


def _proj_even_kernel(h_ref, g_ref, w_ref, gq_ref, gk_ref, gmat_ref,
                      qa_ref, ka16_ref, va16_ref, qb_ref, kb16_ref, vb16_ref,
                      ka_p_ref, va_p_ref, kb_p_ref, vb_p_ref, ka_s_ref, va_s_ref, kb_s_ref, vb_s_ref,
                      *, wa, wb, prompt_tiles):
    xn = _rms_rows(h_ref[...], g_ref[...]).astype(BF16)
    proj = _dot(xn, w_ref[...])
    gmat = gmat_ref[...]
    qa = proj[:, 0:wa]
    ka = proj[:, wa:2 * wa]
    va = proj[:, 2 * wa:3 * wa]
    o = 3 * wa
    qb = _head_rms(proj[:, o:o + wb], gmat, gq_ref[...])
    kb = _head_rms(proj[:, o + wb:o + 2 * wb], gmat, gk_ref[...])
    vb = proj[:, o + 2 * wb:o + 3 * wb]
    qa_ref[...] = (qa * SCALE).astype(BF16)
    ka16_ref[...] = ka.astype(BF16)
    va16_ref[...] = va.astype(BF16)
    qb_ref[...] = (qb * SCALE).astype(BF16)
    kb16_ref[...] = kb.astype(BF16)
    vb16_ref[...] = vb.astype(BF16)
    _store_state([ka, va, kb, vb], [ka_p_ref, va_p_ref, kb_p_ref, vb_p_ref],
                 [ka_s_ref, va_s_ref, kb_s_ref, vb_s_ref], prompt_tiles)


def _proj_odd_kernel(h_ref, g_ref, w_ref, gq_ref, gk_ref, gmat_ref, bf_ref,
                     q_ref, k16_ref, v16_ref, lf_ref, k_p_ref, v_p_ref, k_s_ref, v_s_ref, *, wc, prompt_tiles):
    xn = _rms_rows(h_ref[...], g_ref[...]).astype(BF16)
    proj = _dot(xn, w_ref[...])
    gmat = gmat_ref[...]
    q = _head_rms(proj[:, 0:wc], gmat, gq_ref[...])
    k = _head_rms(proj[:, wc:2 * wc], gmat, gk_ref[...])
    v = proj[:, 2 * wc:3 * wc]
    f = proj[:, 3 * wc:3 * wc + LANES]
    q_ref[...] = (q * SCALE).astype(BF16)
    k16_ref[...] = k.astype(BF16)
    v16_ref[...] = v.astype(BF16)
    _store_state([k, v], [k_p_ref, v_p_ref], [k_s_ref, v_s_ref], prompt_tiles)
    lf_ref[...] = _log_sigmoid(f + bf_ref[...])


def _post1_kernel(h_ref, mix_ref, wo_ref, g_ref, w1_ref, h1_ref, hid_ref, *, ff_chunk):
    h1 = h_ref[...] + _dot(mix_ref[...], wo_ref[...])
    h1_ref[...] = h1
    xn = _rms_rows(h1, g_ref[...]).astype(BF16)
    d_ff = w1_ref.shape[1]
    for c in range(d_ff // ff_chunk):
        sl = slice(c * ff_chunk, (c + 1) * ff_chunk)
        a = jnp.maximum(_dot(xn, w1_ref[:, sl]), 0.0)
        hid_ref[:, sl] = (a * a).astype(BF16)


def _post2_kernel(h1_ref, hid_ref, w2_ref, g_ref, wg_ref, p_ref, wp_ref, out_ref):
    h2 = h1_ref[...] + _dot(hid_ref[...], w2_ref[...])
    xg = _rms_rows(h2, g_ref[...]).astype(BF16)
    gate = jax.nn.sigmoid(_dot(xg, wg_ref[...]))
    out_ref[...] = h2 + gate * _dot(p_ref[...].astype(BF16), wp_ref[...])


def _row_spec(tm, width):
    return pl.BlockSpec((tm, width), lambda i: (i, 0))


def _full_spec(shape):
    return pl.BlockSpec(shape, lambda i: (0,) * len(shape))


ROW_TILE = 512


def _row_tile(rows):
    assert rows % ROW_TILE == 0, rows
    return ROW_TILE


def _proj_call(body, name, inputs, n, row_widths, row_dtypes, state_widths):
    h = inputs[0]
    rows, d = h.shape
    tm = _row_tile(rows)
    assert n % tm == 0, (n, tm)
    prompt_tiles = n // tm
    prompt_spec = lambda wd: pl.BlockSpec((wd, tm), lambda i: (0, jnp.minimum(i, prompt_tiles - 1)))
    sample_spec = lambda wd: pl.BlockSpec((wd, tm), lambda i: (0, jnp.maximum(i - prompt_tiles, 0)))
    return pl.pallas_call(
        functools.partial(body, prompt_tiles=prompt_tiles),
        grid=(rows // tm,),
        in_specs=[_row_spec(tm, d)] + [_full_spec(x.shape) for x in inputs[1:]],
        out_specs=([_row_spec(tm, wd) for wd in row_widths] + [prompt_spec(wd) for wd in state_widths]
                   + [sample_spec(wd) for wd in state_widths]),
        out_shape=([jax.ShapeDtypeStruct((rows, wd), dt) for wd, dt in zip(row_widths, row_dtypes)]
                   + [jax.ShapeDtypeStruct((wd, n), F32) for wd in state_widths]
                   + [jax.ShapeDtypeStruct((wd, rows - n), F32) for wd in state_widths]),
        compiler_params=_cparams("arbitrary"),
        name=name,
    )(*inputs)


def _proj_even(h, g, w, gq_row, gk_row, gmat, wa, wb, n):
    return _proj_call(functools.partial(_proj_even_kernel, wa=wa, wb=wb), "proj_even",
                      (h, g, w, gq_row, gk_row, gmat), n,
                      [wa, wa, wa, wb, wb, wb], [BF16] * 6, [wa, wa, wb, wb])


def _proj_odd(h, g, w, gq_row, gk_row, gmat, bf_row, wc, n):
    return _proj_call(functools.partial(_proj_odd_kernel, wc=wc), "proj_odd",
                      (h, g, w, gq_row, gk_row, gmat, bf_row), n,
                      [wc, wc, wc, LANES], [BF16] * 3 + [F32], [wc, wc])


def _post(h, mix, wo, g_mlp, w1, w2, g_ple, wg, p, wp):
    rows, d = h.shape
    tm = _row_tile(rows)
    d_ff = w1.shape[1]
    ff_chunk = 1024 if d_ff % 1024 == 0 else d_ff
    h1, hid = pl.pallas_call(
        functools.partial(_post1_kernel, ff_chunk=ff_chunk),
        grid=(rows // tm,),
        in_specs=[_row_spec(tm, d), _row_spec(tm, mix.shape[1]), _full_spec(wo.shape),
                  _full_spec(g_mlp.shape), _full_spec(w1.shape)],
        out_specs=[_row_spec(tm, d), _row_spec(tm, d_ff)],
        out_shape=[jax.ShapeDtypeStruct((rows, d), F32), jax.ShapeDtypeStruct((rows, d_ff), BF16)],
        compiler_params=_cparams("parallel"),
        name="post_attn_mlp_up",
    )(h, mix, wo, g_mlp, w1)
    return pl.pallas_call(
        _post2_kernel,
        grid=(rows // tm,),
        in_specs=[_row_spec(tm, d), _row_spec(tm, d_ff), _full_spec(w2.shape), _full_spec(g_ple.shape),
                  _full_spec(wg.shape), _row_spec(tm, p.shape[1]), _full_spec(wp.shape)],
        out_specs=_row_spec(tm, d),
        out_shape=jax.ShapeDtypeStruct((rows, d), F32),
        compiler_params=_cparams("parallel"),
        name="mlp_down_ple",
    )(h1, hid, w2, g_ple, wg, p, wp)


def _cumsum_kernel(x_ref, o_ref):
    gs, nb, _ = x_ref.shape
    x2 = x_ref[...].reshape(gs * nb, LANES)
    r = lax.broadcasted_iota(jnp.int32, (LANES, LANES), 0)
    c = lax.broadcasted_iota(jnp.int32, (LANES, LANES), 1)
    upper = jnp.where(r <= c, 1.0, 0.0).astype(BF16)
    ones = jnp.ones((LANES, LANES), BF16)
    within = _dot_split3(x2, upper).reshape(gs, nb, LANES)
    total = _dot_split3(x2, ones).reshape(gs, nb, LANES)
    rb = lax.broadcasted_iota(jnp.int32, (nb, nb), 0)
    cb = lax.broadcasted_iota(jnp.int32, (nb, nb), 1)
    strict_lower = jnp.where(cb < rb, 1.0, 0.0).astype(BF16)
    for g in range(gs):
        hi, mid, lo = _split3(total[g])
        offs = (_dot(strict_lower, hi) + _dot(strict_lower, mid)) + _dot(strict_lower, lo)
        o_ref[g] = within[g] + offs


def _cumsum_lanes(x):
    n_in = x.shape[1]
    x = _pad_to(x, LANES * LANES, 1)
    groups, n = x.shape
    nb = n // LANES
    gs = min(groups, 16)
    out = pl.pallas_call(
        _cumsum_kernel,
        grid=(groups // gs,),
        in_specs=[pl.BlockSpec((gs, nb, LANES), lambda i: (i, 0, 0))],
        out_specs=pl.BlockSpec((gs, nb, LANES), lambda i: (i, 0, 0)),
        out_shape=jax.ShapeDtypeStruct((groups, nb, LANES), F32),
        compiler_params=_cparams("parallel"),
        name="logf_cumsum",
    )(x.reshape(groups, nb, LANES))
    return out.reshape(groups, n)[:, :n_in]


def _neg_suffix_matrix(tk):
    r = lax.broadcasted_iota(jnp.int32, (tk, tk), 0)
    c = lax.broadcasted_iota(jnp.int32, (tk, tk), 1)
    return jnp.where(r >= c, -1.0, 0.0).astype(BF16)


def _stick_tile(qrows, kb, vb, nsuf, carry, acc, mask, feature_major=False):
    z = _dot(qrows, kb) if feature_major else _dot_nt(qrows, kb)
    sp = _softplus(z)
    if mask is not None:
        sp = jnp.where(mask, sp, 0.0)
    incl = _dot_split2(sp, nsuf)
    w = jnp.exp(z + incl + carry)
    if mask is not None:
        w = jnp.where(mask, w, 0.0)
    wb = w.astype(BF16)
    acc = acc + (_dot_nt(wb, vb) if feature_major else _dot(wb, vb))
    carry = carry + incl[:, 0:1]
    return carry, acc


def _softmax_tile(logits, vb, m, l, acc, feature_major=False):
    m_new = jnp.maximum(m, jnp.max(logits, axis=1, keepdims=True))
    alpha = jnp.exp(m - m_new)
    p = jnp.exp(logits - m_new)
    l = alpha * l + jnp.sum(p, axis=1, keepdims=True)
    pb = p.astype(BF16)
    acc = alpha * acc + (_dot_nt(pb, vb) if feature_major else _dot(pb, vb))
    return m_new, l, acc


def _stack_heads(q, n_heads):
    t, width = q.shape
    rep = jnp.concatenate([q] * n_heads, axis=0) if n_heads > 1 else q
    row = lax.broadcasted_iota(jnp.int32, (n_heads * t, width), 0)
    lane = lax.broadcasted_iota(jnp.int32, (n_heads * t, width), 1)
    lo = (row // t) * HEAD_DIM
    keep = (lane >= lo) & (lane < lo + HEAD_DIM)
    return jnp.where(keep, rep, jnp.zeros_like(rep))


def _unstack_heads(acc, n_heads):
    rows, width = acc.shape
    t = rows // n_heads
    lane = lax.broadcasted_iota(jnp.int32, (t, width), 1)
    out = jnp.zeros((t, width), acc.dtype)
    for h in range(n_heads):
        sel = (lane >= h * HEAD_DIM) & (lane < (h + 1) * HEAD_DIM)
        out = jnp.where(sel, acc[h * t:(h + 1) * t, :], out)
    return out


def _row_query_index(rows, cols, t):
    r = lax.broadcasted_iota(jnp.int32, (rows, cols), 0)
    c = lax.broadcasted_iota(jnp.int32, (rows, cols), 1)
    return r % t, c


def _stick_prompt_kernel(q_ref, k_ref, v_ref, o_ref, carry_ref, acc_ref, *, t):
    i = pl.program_id(1)
    nh = HEADS_PER_SLAB
    qrows = _stack_heads(q_ref[...], nh)
    nsuf = _neg_suffix_matrix(t)
    tq_idx, c_idx = _row_query_index(nh * t, t, t)

    def block(j):
        s = pl.multiple_of(j * t, t)
        return k_ref[pl.ds(s, t), :], v_ref[pl.ds(s, t), :]

    def diagonal():
        return _stick_tile(qrows, *block(i), nsuf, jnp.zeros((nh * t, 1), F32),
                           jnp.zeros((nh * t, LANES), F32), c_idx < tq_idx)

    @pl.when(i == 0)
    def _():
        carry, acc = diagonal()
        carry_ref[...] = carry
        acc_ref[...] = acc

    @pl.when(i > 0)
    def _():
        carry, acc = diagonal()
        carry, acc = _stick_tile(qrows, *block(i - 1), nsuf, carry, acc, None)
        carry_ref[...] = carry
        acc_ref[...] = acc

    def cond(state):
        j, carry_max = state
        return jnp.logical_and(j >= 0, carry_max > EXP_ZERO_BELOW)

    def body(state):
        j, _ = state
        carry, acc = _stick_tile(qrows, *block(j), nsuf, carry_ref[...], acc_ref[...], None)
        carry_ref[...] = carry
        acc_ref[...] = acc
        return j - 1, jnp.max(carry)

    lax.while_loop(cond, body, (i - 2, jnp.max(carry_ref[...])))
    o_ref[...] = _unstack_heads(acc_ref[...], nh).astype(o_ref.dtype)


def _stick_prompt(q, k, v, n, t=256):
    width = q.shape[1]
    t = min(t, n)
    slabs = width // LANES
    return pl.pallas_call(
        functools.partial(_stick_prompt_kernel, t=t),
        grid=(slabs, n // t),
        in_specs=[pl.BlockSpec((t, LANES), lambda p, i: (i, p)),
                  pl.BlockSpec((n, LANES), lambda p, i: (0, p)),
                  pl.BlockSpec((n, LANES), lambda p, i: (0, p))],
        out_specs=pl.BlockSpec((t, LANES), lambda p, i: (i, p)),
        out_shape=jax.ShapeDtypeStruct((n, width), BF16),
        scratch_shapes=[pltpu.VMEM((HEADS_PER_SLAB * t, 1), F32),
                        pltpu.VMEM((HEADS_PER_SLAB * t, LANES), F32)],
        compiler_params=_cparams("parallel", "parallel"),
        name="stick_prompt",
    )(q, k, v)


def _stick_sample_kernel(q_ref, kn_ref, vn_ref, ck_ref, cv_ref, nsuf_ref, o_ref, qrows_ref, carry_ref, acc_ref,
                         *, nh):
    jj = pl.program_id(1)
    t = q_ref.shape[1]

    @pl.when(jj == 0)
    def _():
        qrows = _stack_heads(q_ref[0], nh)
        qrows_ref[...] = qrows
        tq_idx, c_idx = _row_query_index(nh * t, t, t)
        carry, acc = _stick_tile(qrows, kn_ref[0], vn_ref[0], _neg_suffix_matrix(t),
                                 jnp.zeros((nh * t, 1), F32), jnp.zeros(acc_ref.shape, F32),
                                 c_idx < tq_idx)
        carry_ref[...] = carry
        acc_ref[...] = acc

    carry, acc = _stick_tile(qrows_ref[...], ck_ref[...].astype(BF16), cv_ref[...].astype(BF16),
                             nsuf_ref[...], carry_ref[...], acc_ref[...], None, feature_major=True)
    carry_ref[...] = carry
    acc_ref[...] = acc

    @pl.when(jj == pl.num_programs(1) - 1)
    def _():
        o_ref[0] = _unstack_heads(acc_ref[...], nh).astype(o_ref.dtype)


def _feature_major(cache):
    layers, b, past, heads, hd = cache.shape
    return jnp.transpose(cache, (0, 1, 3, 4, 2)).reshape(layers, b, heads * hd, past)


def _stick_sample(q, kn, vn, cache_kt, cache_vt, layer, tk=512):
    b, t, width = q.shape
    past = cache_kt.shape[3]
    tk = min(tk, past)
    nblk = past // tk
    nh = width // HEAD_DIM
    r = jnp.arange(tk)
    nsuf = jnp.where(r[:, None] >= r[None, :], -1.0, 0.0).astype(BF16)
    new_spec = pl.BlockSpec((1, t, width), lambda bi, jj: (bi, 0, 0))
    cache_spec = pl.BlockSpec((None, None, width, tk), lambda bi, jj: (layer, bi, 0, nblk - 1 - jj))
    return pl.pallas_call(
        functools.partial(_stick_sample_kernel, nh=nh),
        grid=(b, nblk),
        in_specs=[new_spec, new_spec, new_spec, cache_spec, cache_spec,
                  pl.BlockSpec((tk, tk), lambda bi, jj: (0, 0))],
        out_specs=pl.BlockSpec((1, t, width), lambda bi, jj: (bi, 0, 0)),
        out_shape=jax.ShapeDtypeStruct((b, t, width), BF16),
        scratch_shapes=[pltpu.VMEM((nh * t, width), BF16),
                        pltpu.VMEM((nh * t, 1), F32),
                        pltpu.VMEM((nh * t, width), F32)],
        compiler_params=_cparams("parallel", "arbitrary"),
        name="stick_sample",
    )(q, kn, vn, cache_kt, cache_vt, nsuf)


def _band_prompt_kernel(q_ref, k0_ref, k1_ref, k2_ref, v0_ref, v1_ref, v2_ref, bias_ref, o_ref,
                        *, t, nprev, fixed_shift):
    i = pl.program_id(1)
    nh = HEADS_PER_SLAB
    qrows = _stack_heads(q_ref[...], nh)
    krefs = (k0_ref, k1_ref, k2_ref)[3 - (nprev + 1):]
    vrefs = (v0_ref, v1_ref, v2_ref)[3 - (nprev + 1):]
    kcat = jnp.concatenate([r[...] for r in krefs], axis=0)
    vcat = jnp.concatenate([r[...] for r in vrefs], axis=0)
    logits = _dot_nt(qrows, kcat) + bias_ref[0]
    c_idx = lax.broadcasted_iota(jnp.int32, logits.shape, 1)
    logits = jnp.where(c_idx >= (nprev - i) * t, logits, MASKED)
    if fixed_shift:
        v_ones = jnp.concatenate([vcat, jnp.ones(vcat.shape, BF16)], axis=1)
        acc = _dot(jnp.exp(logits).astype(BF16), v_ones)
        acc = acc[:, :LANES] / acc[:, LANES:]
    else:
        m = jnp.max(logits, axis=1, keepdims=True)
        p = jnp.exp(logits - m)
        l = jnp.sum(p, axis=1, keepdims=True)
        acc = _dot(p.astype(BF16), vcat) / l
    o_ref[...] = _unstack_heads(acc, nh).astype(o_ref.dtype)


def _rel_bias(rel_table, nq, nk, q0, k0):
    m_len = nq + nk - 1
    u = np.arange(m_len)
    d = np.where(u < nk, (q0 - k0) - u, (q0 - k0) + (m_len - u))
    g = jnp.take(rel_table.astype(F32), np.clip(d, -REL_CLIP, REL_CLIP) + REL_CLIP, axis=1)
    n_heads = g.shape[0]
    bias = jnp.tile(g, (1, nq))[:, :nq * (m_len - 1)].reshape(n_heads, nq, m_len - 1)[:, :, :nk]
    qp = q0 + np.arange(nq)[:, None]
    kp = k0 + np.arange(nk)[None, :]
    qc, kc = qp // CHUNK, kp // CHUNK
    visible = (kp >= 0) & (kc <= qc) & (kc >= qc - BAND_CHUNKS)
    return bias, jnp.asarray(visible)[None]


def _band_prompt(q, k, v, rel_table, n, logit_bound, t=256):
    width = q.shape[1]
    t = min(t, n)
    assert t % CHUNK == 0
    nprev = min(-(-BAND // t), 2)
    assert nprev * t >= BAND
    slabs = width // LANES
    bias, visible = _rel_bias(rel_table, t, (nprev + 1) * t, nprev * t, 0)
    bias_shape = (slabs, HEADS_PER_SLAB * t, (nprev + 1) * t)

    def kv_spec(back):
        return pl.BlockSpec((t, LANES), lambda p, i: (jnp.maximum(i - back, 0), p))

    def call(masked_bias, fixed_shift):
        return pl.pallas_call(
            functools.partial(_band_prompt_kernel, t=t, nprev=nprev, fixed_shift=fixed_shift),
            grid=(slabs, n // t),
            in_specs=[pl.BlockSpec((t, LANES), lambda p, i: (i, p)),
                      kv_spec(2), kv_spec(1), kv_spec(0), kv_spec(2), kv_spec(1), kv_spec(0),
                      pl.BlockSpec((1,) + bias_shape[1:], lambda p, i: (p, 0, 0))],
            out_specs=pl.BlockSpec((t, LANES), lambda p, i: (i, p)),
            out_shape=jax.ShapeDtypeStruct((n, width), BF16),
            compiler_params=_cparams("parallel", "parallel"),
            name="band_prompt" if fixed_shift else "band_prompt_row_max",
        )(q, k, k, k, v, v, v, masked_bias.reshape(bias_shape))

    tab_max, tab_min = jnp.max(rel_table), jnp.min(rel_table)
    shift = logit_bound + tab_max
    spread = 2.0 * logit_bound + (tab_max - tab_min)
    return lax.cond(spread < 2.0 * MAX_FIXED_SHIFT,
                    lambda: call(jnp.where(visible, bias - shift, MASKED), True),
                    lambda: call(jnp.where(visible, bias, MASKED), False))


def _band_sample_kernel(q_ref, kn_ref, vn_ref, ck_ref, cv_ref, bc_ref, bn_ref, o_ref, *, nh):
    qrows = _stack_heads(q_ref[0], nh)
    lc = _dot(qrows, ck_ref[...].astype(BF16)) + bc_ref[...]
    ln = _dot_nt(qrows, kn_ref[0]) + bn_ref[...]
    m = jnp.maximum(jnp.max(lc, axis=1, keepdims=True), jnp.max(ln, axis=1, keepdims=True))
    pc = jnp.exp(lc - m)
    pn = jnp.exp(ln - m)
    l = jnp.sum(pc, axis=1, keepdims=True) + jnp.sum(pn, axis=1, keepdims=True)
    acc = (_dot_nt(pc.astype(BF16), cv_ref[...].astype(BF16)) + _dot(pn.astype(BF16), vn_ref[0])) / l
    o_ref[0] = _unstack_heads(acc, nh).astype(o_ref.dtype)


def _band_sample(q, kn, vn, cache_kt, cache_vt, layer, rel_table, past_len):
    b, t, width = q.shape
    buf_len = cache_kt.shape[3]
    nh = width // HEAD_DIM
    def masked_bias(nk, k0):
        bias, visible = _rel_bias(rel_table, t, nk, past_len, k0)
        return jnp.where(visible, bias, MASKED).reshape(nh * t, nk)

    bias_c = masked_bias(buf_len, past_len - buf_len)
    bias_n = masked_bias(t, past_len)
    new_spec = pl.BlockSpec((1, t, width), lambda bi: (bi, 0, 0))
    cache_spec = pl.BlockSpec((None, None, width, buf_len), lambda bi: (layer, bi, 0, 0))
    return pl.pallas_call(
        functools.partial(_band_sample_kernel, nh=nh),
        grid=(b,),
        in_specs=[new_spec, new_spec, new_spec, cache_spec, cache_spec,
                  _full_spec(bias_c.shape), _full_spec(bias_n.shape)],
        out_specs=pl.BlockSpec((1, t, width), lambda bi: (bi, 0, 0)),
        out_shape=jax.ShapeDtypeStruct((b, t, width), BF16),
        compiler_params=_cparams("parallel"),
        name="band_sample",
    )(q, kn, vn, cache_kt, cache_vt, bias_c, bias_n)


def _key_decay_rows(dk, t):
    n_heads, tk = dk.shape
    return jnp.concatenate([jnp.broadcast_to(dk[h:h + 1, :], (t, tk)) for h in range(n_heads)], axis=0)


def _fox_prompt_kernel(q_ref, dq_ref, k_ref, v_ref, dk_ref, o_ref, m_ref, l_ref, acc_ref, *, t):
    i = pl.program_id(1)
    nh = HEADS_PER_SLAB
    qrows = _stack_heads(q_ref[...], nh)
    dq = jnp.concatenate([dq_ref[:, h:h + 1] for h in range(nh)], axis=0)
    m_ref[...] = jnp.full(m_ref.shape, MASKED, F32)
    l_ref[...] = jnp.zeros(l_ref.shape, F32)
    acc_ref[...] = jnp.zeros(acc_ref.shape, F32)

    def tile(j, mask):
        s = pl.multiple_of(j * t, t)
        z = _dot_nt(qrows, k_ref[pl.ds(s, t), :])
        logits = (z + dq) - _key_decay_rows(dk_ref[:, pl.ds(s, t)], t)
        if mask is not None:
            logits = jnp.where(mask, logits, MASKED)
        m, l, acc = _softmax_tile(logits, v_ref[pl.ds(s, t), :], m_ref[...], l_ref[...], acc_ref[...])
        m_ref[...] = m
        l_ref[...] = l
        acc_ref[...] = acc

    def body(j, carry):
        tile(j, None)
        return carry

    lax.fori_loop(0, i, body, 0)
    tq_idx, c_idx = _row_query_index(nh * t, t, t)
    tile(i, c_idx <= tq_idx)
    o_ref[...] = _unstack_heads(acc_ref[...] / l_ref[...], nh).astype(o_ref.dtype)


def _fox_prompt_bounded_kernel(shift_ref, first_ref, q_ref, dq_ref, k_ref, v_ref, dk_ref, o_ref,
                               dqrep_ref, acc_ref, z_ref, *, tq, tk):
    slab = pl.program_id(0)
    i = pl.program_id(1)
    ratio = tq // tk
    q = q_ref[...]
    ones = jnp.ones((tk, LANES), BF16)
    row = lax.broadcasted_iota(jnp.int32, (tq, LANES), 0)
    lane = lax.broadcasted_iota(jnp.int32, (tq, LANES), 1)
    outs = []
    for hh in range(HEADS_PER_SLAB):
        qm = jnp.where((lane >= hh * HEAD_DIM) & (lane < (hh + 1) * HEAD_DIM), q, jnp.zeros_like(q))
        dqrep_ref[...] = jnp.broadcast_to(dq_ref[:, hh:hh + 1] - shift_ref[0, 0], (tq, LANES))
        acc_ref[...] = jnp.zeros(acc_ref.shape, F32)

        def scores(j, r0, qm=qm):
            s = pl.multiple_of(j * tk, tk)
            return _dot_nt(qm[r0:], k_ref[pl.ds(s, tk), :])

        def tile(j, r0, causal, next_r0, hh=hh, scores=scores):
            s = pl.multiple_of(j * tk, tk)
            z = z_ref[r0:, :]
            dqh = dqrep_ref[r0:, :]
            cols = []
            for c in range(tk // LANES):
                lg = (z[:, c * LANES:(c + 1) * LANES] + dqh) - dk_ref[hh:hh + 1, pl.ds(s + c * LANES, LANES)]
                if causal:
                    key_pos = s + c * LANES + lax.broadcasted_iota(jnp.int32, lg.shape, 1)
                    query_pos = i * tq + r0 + lax.broadcasted_iota(jnp.int32, lg.shape, 0)
                    lg = jnp.where(key_pos <= query_pos, lg, MASKED)
                cols.append(jnp.exp(lg).astype(BF16))
            if next_r0 is not None:
                z_next = scores(j + 1, next_r0)
            v_ones = jnp.concatenate([v_ref[pl.ds(s, tk), :], ones], axis=1)
            acc_ref[r0:, :] += _dot(jnp.concatenate(cols, axis=1), v_ones)
            if next_r0 is not None:
                z_ref[next_r0:, :] = z_next

        start = first_ref[slab * HEADS_PER_SLAB + hh, i]
        z_ref[...] = scores(start, 0)

        def body(j, carry, tile=tile):
            tile(j, 0, False, 0)
            return carry

        lax.fori_loop(start, i * ratio, body, 0)
        for d in range(ratio):
            tile(i * ratio + d, d * tk, True, (d + 1) * tk if d + 1 < ratio else None)
        acc = acc_ref[...]
        outs.append(acc[:, :LANES] / acc[:, LANES:])
    o_ref[...] = jnp.where(lane < HEAD_DIM, outs[0], outs[1]).astype(o_ref.dtype)


MAX_FIXED_SHIFT = 30.0


def _fox_prompt(q, k, v, cum, n, logit_bound, t_exact=256, tq_bounded=1024, tk_bounded=512):
    width = q.shape[1]
    slabs = width // LANES
    nh = HEADS_PER_SLAB
    dq = cum.reshape(n, slabs, nh).transpose(1, 0, 2)
    dk = cum.reshape(n, slabs, nh).transpose(1, 2, 0)

    def specs(t):
        return ([pl.BlockSpec((t, LANES), lambda p, i: (i, p)),
                 pl.BlockSpec((None, t, nh), lambda p, i: (p, i, 0)),
                 pl.BlockSpec((n, LANES), lambda p, i: (0, p)),
                 pl.BlockSpec((n, LANES), lambda p, i: (0, p)),
                 pl.BlockSpec((None, nh, n), lambda p, i: (p, 0, 0))],
                pl.BlockSpec((t, LANES), lambda p, i: (i, p)))

    def exact(_):
        t = min(t_exact, n)
        in_specs, out_spec = specs(t)
        return pl.pallas_call(
            functools.partial(_fox_prompt_kernel, t=t),
            grid=(slabs, n // t),
            in_specs=in_specs, out_specs=out_spec,
            out_shape=jax.ShapeDtypeStruct((n, width), BF16),
            scratch_shapes=[pltpu.VMEM((nh * t, 1), F32), pltpu.VMEM((nh * t, 1), F32),
                            pltpu.VMEM((nh * t, LANES), F32)],
            compiler_params=_cparams("parallel", "parallel"),
            name="fox_prompt_running_max",
        )(q, dq, k, v, dk)

    def bounded(shift):
        tq = min(tq_bounded, n)
        tk = min(tk_bounded, n)
        in_specs, out_spec = specs(tq)
        cum_t = cum.T
        decay = cum_t[:, ::tq][:, :, None] - cum_t[:, tk - 1::tk][:, None, :]
        before = jnp.arange(n // tk)[None, None, :] < (jnp.arange(n // tq) * (tq // tk))[None, :, None]
        dead = jnp.logical_and(decay < EXP_ZERO_BELOW, before)
        first = jnp.sum(jnp.cumprod(dead.astype(jnp.int32), axis=2), axis=2).astype(jnp.int32)
        smem = pl.BlockSpec(memory_space=pltpu.SMEM)
        return pl.pallas_call(
            functools.partial(_fox_prompt_bounded_kernel, tq=tq, tk=tk),
            grid=(slabs, n // tq),
            in_specs=[smem, smem] + in_specs, out_specs=out_spec,
            out_shape=jax.ShapeDtypeStruct((n, width), BF16),
            scratch_shapes=[pltpu.VMEM((tq, LANES), F32), pltpu.VMEM((tq, 2 * LANES), F32),
                            pltpu.VMEM((tq, tk), F32)],
            compiler_params=_cparams("parallel", "parallel"),
            name="fox_prompt",
        )(shift.reshape(1, 1), first, q, dq, k, v, dk)

    return lax.cond(logit_bound < MAX_FIXED_SHIFT, bounded, exact, logit_bound)


def _fox_sample_kernel(q_ref, dq_ref, kn_ref, vn_ref, dkn_ref, ck_ref, cv_ref, dkc_ref, o_ref,
                       qrows_ref, m_ref, l_ref, acc_ref, *, nh):
    jj = pl.program_id(1)
    t = q_ref.shape[1]
    dq = dq_ref[0]

    @pl.when(jj == 0)
    def _():
        qrows = _stack_heads(q_ref[0], nh)
        qrows_ref[...] = qrows
        tq_idx, c_idx = _row_query_index(nh * t, t, t)
        logits = (_dot_nt(qrows, kn_ref[0]) + dq) - _key_decay_rows(dkn_ref[0], t)
        logits = jnp.where(c_idx <= tq_idx, logits, MASKED)
        m, l, acc = _softmax_tile(logits, vn_ref[0], jnp.full(m_ref.shape, MASKED, F32),
                                  jnp.zeros(l_ref.shape, F32), jnp.zeros(acc_ref.shape, F32))
        m_ref[...] = m
        l_ref[...] = l
        acc_ref[...] = acc

    logits = (_dot(qrows_ref[...], ck_ref[...].astype(BF16)) + dq) - _key_decay_rows(dkc_ref[0], t)
    m, l, acc = _softmax_tile(logits, cv_ref[...].astype(BF16), m_ref[...], l_ref[...], acc_ref[...],
                              feature_major=True)
    m_ref[...] = m
    l_ref[...] = l
    acc_ref[...] = acc

    @pl.when(jj == pl.num_programs(1) - 1)
    def _():
        o_ref[0] = _unstack_heads(acc_ref[...] / l_ref[...], nh).astype(o_ref.dtype)


def _fox_sample(q, kn, vn, cache_kt, cache_vt, layer, cum_cache, cum_new, tk=512):
    b, t, width = q.shape
    past = cache_kt.shape[3]
    tk = min(tk, past)
    nblk = past // tk
    nh = width // HEAD_DIM
    dq = cum_new.reshape(b, nh * t, 1)
    new_spec = pl.BlockSpec((1, t, width), lambda bi, jj: (bi, 0, 0))
    cache_spec = pl.BlockSpec((None, None, width, tk), lambda bi, jj: (layer, bi, 0, jj))
    return pl.pallas_call(
        functools.partial(_fox_sample_kernel, nh=nh),
        grid=(b, nblk),
        in_specs=[new_spec, pl.BlockSpec((1, nh * t, 1), lambda bi, jj: (bi, 0, 0)),
                  new_spec, new_spec, pl.BlockSpec((1, nh, t), lambda bi, jj: (bi, 0, 0)),
                  cache_spec, cache_spec, pl.BlockSpec((1, nh, tk), lambda bi, jj: (bi, 0, jj))],
        out_specs=pl.BlockSpec((1, t, width), lambda bi, jj: (bi, 0, 0)),
        out_shape=jax.ShapeDtypeStruct((b, t, width), BF16),
        scratch_shapes=[pltpu.VMEM((nh * t, width), BF16), pltpu.VMEM((nh * t, 1), F32),
                        pltpu.VMEM((nh * t, 1), F32), pltpu.VMEM((nh * t, width), F32)],
        compiler_params=_cparams("parallel", "arbitrary"),
        name="fox_sample",
    )(q, dq, kn, vn, cum_new, cache_kt, cache_vt, cum_cache)


def _pad_to(x, multiple, axis):
    size = x.shape[axis]
    target = -(-size // multiple) * multiple
    if target == size:
        return x
    pad = [(0, 0)] * x.ndim
    pad[axis] = (0, target - size)
    return jnp.pad(x, pad)


def kernel(x_prompt, x_sample, cache_a_k, cache_a_v, cache_b_k, cache_b_v, cache_c_k, cache_c_v, cache_c_logf, p_prompt, p_sample, g_mix, w_in_even, g_qb, g_kb, rel_bias, w_in_odd, b_forget, g_qc, g_kc, w_out, g_mlp, w_ff1, w_ff2, g_ple, w_ple_gate, w_ple_proj):
    bp, n, d = x_prompt.shape
    bs, ts, _ = x_sample.shape
    assert bp == 1, "prompt kernels assume a single prompt stream"
    depth = g_mix.shape[0]
    past = cache_a_k.shape[2]
    buf_len = cache_b_k.shape[2]
    ha, hb, hc = cache_a_k.shape[3], cache_b_k.shape[3], cache_c_k.shape[3]
    wa, wb, wc = ha * HEAD_DIM, hb * HEAD_DIM, hc * HEAD_DIM
    rows_s = bs * ts

    blk = jnp.arange(NORM_SLAB) // HEAD_DIM
    gmat = jnp.where(blk[:, None] == blk[None, :], 1.0 / HEAD_DIM, 0.0).astype(BF16)

    cache_a_kt, cache_a_vt = _feature_major(cache_a_k), _feature_major(cache_a_v)
    cache_b_kt, cache_b_vt = _feature_major(cache_b_k), _feature_major(cache_b_v)
    cache_c_kt, cache_c_vt = _feature_major(cache_c_k), _feature_major(cache_c_v)

    rows = n + rows_s
    tail = -rows % ROW_TILE
    h = jnp.concatenate([x_prompt.reshape(n, d), x_sample.reshape(rows_s, d), jnp.zeros((tail, d), F32)], axis=0)
    d_ple = p_prompt.shape[-1]
    p_all = jnp.concatenate([p_prompt.reshape(depth, n, d_ple), p_sample.reshape(depth, rows_s, d_ple),
                             jnp.zeros((depth, tail, d_ple), F32)], axis=1)

    def prompt_state(xt, heads, lo=0):
        return jnp.transpose(xt[:, lo:].reshape(heads, HEAD_DIM, n - lo), (2, 0, 1))[None]

    def sample_state(xt, heads):
        return jnp.transpose(xt[:, :rows_s].reshape(heads, HEAD_DIM, bs, ts), (2, 3, 0, 1))

    st = {name: [] for name in ("pa_k", "pa_v", "pb_k", "pb_v", "pc_k", "pc_v", "pc_f",
                                "sa_k", "sa_v", "sb_k", "sb_v", "sc_k", "sc_v", "sc_f")}
    for i in range(depth):
        j = i // 2
        g_row = g_mix[i].reshape(1, d)
        if i % 2 == 0:
            gq_row = jnp.tile(g_qb[j], hb).reshape(1, wb)
            gk_row = jnp.tile(g_kb[j], hb).reshape(1, wb)
            qa, ka16, va16, qb, kb16, vb16, ka_p, va_p, kb_p, vb_p, ka_s, va_s, kb_s, vb_s = _proj_even(
                h, g_row, w_in_even[j].astype(BF16), gq_row, gk_row, gmat, wa, wb, n)
            mix_a_p = _stick_prompt(qa, ka16, va16, n)
            band_bound = 1.01 * HEAD_DIM * SCALE * jnp.max(jnp.abs(g_qb[j])) * jnp.max(jnp.abs(g_kb[j]))
            mix_b_p = _band_prompt(qb, kb16, vb16, rel_bias[j], n, band_bound.astype(F32))
            s3 = lambda a: a[n:rows].reshape(bs, ts, a.shape[1])
            mix_a_s = _stick_sample(s3(qa), s3(ka16), s3(va16), cache_a_kt, cache_a_vt, j)
            mix_b_s = _band_sample(s3(qb), s3(kb16), s3(vb16), cache_b_kt, cache_b_vt, j, rel_bias[j], past)
            mix = jnp.concatenate([jnp.concatenate([mix_a_p, mix_b_p], axis=1),
                                   jnp.concatenate([mix_a_s.reshape(rows_s, wa), mix_b_s.reshape(rows_s, wb)],
                                                   axis=1),
                                   jnp.zeros((tail, wa + wb), BF16)], axis=0)
            st["pa_k"].append(prompt_state(ka_p, ha))
            st["pa_v"].append(prompt_state(va_p, ha))
            keep = min(BAND, n)
            st["pb_k"].append(prompt_state(kb_p, hb, n - keep))
            st["pb_v"].append(prompt_state(vb_p, hb, n - keep))
            st["sa_k"].append(sample_state(ka_s, ha))
            st["sa_v"].append(sample_state(va_s, ha))
            st["sb_k"].append(jnp.concatenate([cache_b_k[j], sample_state(kb_s, hb)], axis=1)[:, ts:])
            st["sb_v"].append(jnp.concatenate([cache_b_v[j], sample_state(vb_s, hb)], axis=1)[:, ts:])
        else:
            gq_row = jnp.tile(g_qc[j], hc).reshape(1, wc)
            gk_row = jnp.tile(g_kc[j], hc).reshape(1, wc)
            w_pad = _pad_to(w_in_odd[j], LANES, 1).astype(BF16)
            bf_row = _pad_to(b_forget[j].reshape(1, hc), LANES, 1)
            q, k16, v16, lf, k_p, v_p, k_s, v_s = _proj_odd(h, g_row, w_pad, gq_row, gk_row, gmat, bf_row, wc, n)
            log_f = lf[:, :hc]
            cum_p = _cumsum_lanes(log_f[:n].T)
            logit_bound = 1.01 * HEAD_DIM * SCALE * jnp.max(jnp.abs(g_qc[j])) * jnp.max(jnp.abs(g_kc[j]))
            mix_p = _fox_prompt(q, k16, v16, cum_p.T, n, logit_bound.astype(F32))
            lf_new = log_f[n:rows].reshape(bs, ts, hc).transpose(0, 2, 1)
            lf_all = jnp.concatenate([cache_c_logf[j].transpose(0, 2, 1), lf_new], axis=2)
            cum_s = _cumsum_lanes(lf_all.reshape(bs * hc, past + ts)).reshape(bs, hc, past + ts)
            s3 = lambda a: a[n:rows].reshape(bs, ts, a.shape[1])
            mix_s = _fox_sample(s3(q), s3(k16), s3(v16), cache_c_kt, cache_c_vt, j,
                                cum_s[:, :, :past], cum_s[:, :, past:past + ts])
            mix = jnp.concatenate([mix_p, mix_s.reshape(rows_s, wc), jnp.zeros((tail, wc), BF16)], axis=0)
            st["pc_k"].append(prompt_state(k_p, hc))
            st["pc_v"].append(prompt_state(v_p, hc))
            st["pc_f"].append(log_f[:n].reshape(1, n, hc))
            st["sc_k"].append(sample_state(k_s, hc))
            st["sc_v"].append(sample_state(v_s, hc))
            st["sc_f"].append(log_f[n:rows].reshape(bs, ts, hc))
        h = _post(h, mix, w_out[i].astype(BF16), g_mlp[i].reshape(1, d), w_ff1[i].astype(BF16),
                  w_ff2[i].astype(BF16), g_ple[i].reshape(1, d), w_ple_gate[i].astype(BF16),
                  p_all[i], w_ple_proj[i].astype(BF16))

    y_prompt = h[:n].reshape(1, n, d)
    y_sample = h[n:rows].reshape(bs, ts, d)
    stk = {name: jnp.stack(vals) for name, vals in st.items()}
    return (y_prompt, y_sample, stk["pa_k"], stk["pa_v"], stk["pb_k"], stk["pb_v"],
            stk["pc_k"], stk["pc_v"], stk["pc_f"], stk["sa_k"], stk["sa_v"], stk["sb_k"], stk["sb_v"],
            stk["sc_k"], stk["sc_v"], stk["sc_f"])
```

```python
import functools

import numpy as np
import jax
import jax.numpy as jnp
from jax import lax
from jax.experimental import pallas as pl
from jax.experimental.pallas import tpu as pltpu

F32 = jnp.float32
BF16 = jnp.bfloat16

HEAD_DIM = 64
CHUNK = 64
BAND_CHUNKS = 8
BAND = BAND_CHUNKS * CHUNK
REL_CLIP = 128
EPS = 1e-6
SCALE = HEAD_DIM ** -0.5
LANES = 128
HEADS_PER_SLAB = LANES // HEAD_DIM
NORM_SLAB = 256
EXP_ZERO_BELOW = -104.0
MASKED = -1e30
VMEM_LIMIT = 56 * 1024 * 1024


def _cparams(*sem):
    return pltpu.CompilerParams(dimension_semantics=sem, vmem_limit_bytes=VMEM_LIMIT)


def _split2(x):
    hi = x.astype(BF16)
    lo = (x - hi.astype(F32)).astype(BF16)
    return hi, lo


def _split3(x):
    hi = x.astype(BF16)
    r = x - hi.astype(F32)
    mid = r.astype(BF16)
    lo = (r - mid.astype(F32)).astype(BF16)
    return hi, mid, lo


def _dot(a, b):
    return jnp.dot(a, b, preferred_element_type=F32)


def _dot_nt(a, b):
    return lax.dot_general(a, b, (((1,), (1,)), ((), ())), preferred_element_type=F32)


def _dot_split2(x, m):
    hi, lo = _split2(x)
    return _dot(hi, m) + _dot(lo, m)


def _dot_split3(x, m):
    hi, mid, lo = _split3(x)
    return (_dot(hi, m) + _dot(mid, m)) + _dot(lo, m)


def _rms_rows(x, g):
    ms = jnp.mean(x * x, axis=-1, keepdims=True)
    return x * lax.rsqrt(ms + EPS) * g


def _head_rms(x, gmat, gain_row):
    width = x.shape[1]
    outs = []
    for s in range(width // NORM_SLAB):
        xs = x[:, NORM_SLAB * s:NORM_SLAB * (s + 1)]
        ms = _dot_split2(xs * xs, gmat)
        outs.append(xs * lax.rsqrt(ms + EPS))
    y = outs[0] if len(outs) == 1 else jnp.concatenate(outs, axis=1)
    return y * gain_row


def _softplus(z):
    return jnp.maximum(z, 0.0) + jnp.log(1.0 + jnp.exp(-jnp.abs(z)))


def _log_sigmoid(z):
    return jnp.minimum(z, 0.0) - jnp.log(1.0 + jnp.exp(-jnp.abs(z)))


def _store_state(values, prompt_refs, sample_refs, prompt_tiles):
    step = pl.program_id(0)

    @pl.when(step < prompt_tiles)
    def _():
        for value, ref in zip(values, prompt_refs):
            ref[...] = value.T

    @pl.when(step >= prompt_tiles)
    def _():
        for value, ref in zip(values, sample_refs):
            ref[...] = value.T


def _proj_even_kernel(h_ref, g_ref, w_ref, gq_ref, gk_ref, gmat_ref,
                      qa_ref, ka16_ref, va16_ref, qb_ref, kb16_ref, vb16_ref,
                      ka_p_ref, va_p_ref, kb_p_ref, vb_p_ref, ka_s_ref, va_s_ref, kb_s_ref, vb_s_ref,
                      *, wa, wb, prompt_tiles):
    xn = _rms_rows(h_ref[...], g_ref[...]).astype(BF16)
    proj = _dot(xn, w_ref[...])
    gmat = gmat_ref[...]
    qa = proj[:, 0:wa]
    ka = proj[:, wa:2 * wa]
    va = proj[:, 2 * wa:3 * wa]
    o = 3 * wa
    qb = _head_rms(proj[:, o:o + wb], gmat, gq_ref[...])
    kb = _head_rms(proj[:, o + wb:o + 2 * wb], gmat, gk_ref[...])
    vb = proj[:, o + 2 * wb:o + 3 * wb]
    qa_ref[...] = (qa * SCALE).astype(BF16)
    ka16_ref[...] = ka.astype(BF16)
    va16_ref[...] = va.astype(BF16)
    qb_ref[...] = (qb * SCALE).astype(BF16)
    kb16_ref[...] = kb.astype(BF16)
    vb16_ref[...] = vb.astype(BF16)
    _store_state([ka, va, kb, vb], [ka_p_ref, va_p_ref, kb_p_ref, vb_p_ref],
                 [ka_s_ref, va_s_ref, kb_s_ref, vb_s_ref], prompt_tiles)


def _proj_odd_kernel(h_ref, g_ref, w_ref, gq_ref, gk_ref, gmat_ref, bf_ref,
                     q_ref, k16_ref, v16_ref, lf_ref, k_p_ref, v_p_ref, k_s_ref, v_s_ref, *, wc, prompt_tiles):
    xn = _rms_rows(h_ref[...], g_ref[...]).astype(BF16)
    proj = _dot(xn, w_ref[...])
    gmat = gmat_ref[...]
    q = _head_rms(proj[:, 0:wc], gmat, gq_ref[...])
    k = _head_rms(proj[:, wc:2 * wc], gmat, gk_ref[...])
    v = proj[:, 2 * wc:3 * wc]
    f = proj[:, 3 * wc:3 * wc + LANES]
    q_ref[...] = (q * SCALE).astype(BF16)
    k16_ref[...] = k.astype(BF16)
    v16_ref[...] = v.astype(BF16)
    _store_state([k, v], [k_p_ref, v_p_ref], [k_s_ref, v_s_ref], prompt_tiles)
    lf_ref[...] = _log_sigmoid(f + bf_ref[...])


def _post1_kernel(*refs, n_mix, ff_chunk, prompt_tiles):
    h_ref, mix_p_refs, mix_s_ref = refs[0], refs[1:1 + n_mix], refs[1 + n_mix]
    wo_ref, g_ref, w1_ref, h1_ref, hid_ref = refs[2 + n_mix:]
    pieces = [r[...] for r in mix_p_refs]
    mix_p = pieces[0] if n_mix == 1 else jnp.concatenate(pieces, axis=1)
    mix = jnp.where(pl.program_id(0) < prompt_tiles, mix_p, mix_s_ref[...])
    h1 = h_ref[...] + _dot(mix, wo_ref[...])
    h1_ref[...] = h1
    xn = _rms_rows(h1, g_ref[...]).astype(BF16)
    d_ff = w1_ref.shape[1]
    for c in range(d_ff // ff_chunk):
        sl = slice(c * ff_chunk, (c + 1) * ff_chunk)
        a = jnp.maximum(_dot(xn, w1_ref[:, sl]), 0.0)
        hid_ref[:, sl] = (a * a).astype(BF16)


def _post2_kernel(h1_ref, hid_ref, w2_ref, g_ref, wg_ref, p_p_ref, p_s_ref, wp_ref, *out_refs, prompt_tiles):
    step = pl.program_id(0)
    h2 = h1_ref[...] + _dot(hid_ref[...], w2_ref[...])
    xg = _rms_rows(h2, g_ref[...]).astype(BF16)
    gate = jax.nn.sigmoid(_dot(xg, wg_ref[...]))
    p = jnp.where(step < prompt_tiles, p_p_ref[...], p_s_ref[...]).astype(BF16)
    out = h2 + gate * _dot(p, wp_ref[...])
    if len(out_refs) == 1:
        out_refs[0][...] = out
    else:
        @pl.when(step < prompt_tiles)
        def _():
            out_refs[0][...] = out

        @pl.when(step >= prompt_tiles)
        def _():
            out_refs[1][...] = out


def _row_spec(tm, width):
    return pl.BlockSpec((tm, width), lambda i: (i, 0))


def _full_spec(shape):
    return pl.BlockSpec(shape, lambda i: (0,) * len(shape))


ROW_TILE = 512


def _row_tile(rows):
    assert rows % ROW_TILE == 0, rows
    return ROW_TILE


def _proj_call(body, name, inputs, layer, n, row_widths, row_dtypes, state_widths):
    h = inputs[0]
    rows, d = h.shape
    tm = _row_tile(rows)
    assert n % tm == 0, (n, tm)
    prompt_tiles = n // tm
    prompt_spec = lambda wd: pl.BlockSpec((wd, tm), lambda i: (0, jnp.minimum(i, prompt_tiles - 1)))
    sample_spec = lambda wd: pl.BlockSpec((wd, tm), lambda i: (0, jnp.maximum(i - prompt_tiles, 0)))
    return pl.pallas_call(
        functools.partial(body, prompt_tiles=prompt_tiles),
        grid=(rows // tm,),
        in_specs=[_row_spec(tm, d)] + [_layer_spec(x, layer) if x.ndim == 3 else _full_spec(x.shape)
                                       for x in inputs[1:]],
        out_specs=([_row_spec(tm, wd) for wd in row_widths] + [prompt_spec(wd) for wd in state_widths]
                   + [sample_spec(wd) for wd in state_widths]),
        out_shape=([jax.ShapeDtypeStruct((rows, wd), dt) for wd, dt in zip(row_widths, row_dtypes)]
                   + [jax.ShapeDtypeStruct((wd, n), F32) for wd in state_widths]
                   + [jax.ShapeDtypeStruct((wd, rows - n), F32) for wd in state_widths]),
        compiler_params=_cparams("arbitrary"),
        name=name,
    )(*inputs)


def _proj_even(h, g, w_stack, layer, gq_row, gk_row, gmat, wa, wb, n):
    return _proj_call(functools.partial(_proj_even_kernel, wa=wa, wb=wb), "proj_even",
                      (h, g, w_stack, gq_row, gk_row, gmat), layer, n,
                      [wa, wa, wa, wb, wb, wb], [BF16] * 6, [wa, wa, wb, wb])


def _proj_odd(h, g, w_stack, layer, gq_row, gk_row, gmat, bf_row, wc, n):
    return _proj_call(functools.partial(_proj_odd_kernel, wc=wc), "proj_odd",
                      (h, g, w_stack, gq_row, gk_row, gmat, bf_row), layer, n,
                      [wc, wc, wc, LANES], [BF16] * 3 + [F32], [wc, wc])


def _layer_spec(stacked, layer):
    return pl.BlockSpec((None,) + stacked.shape[1:], lambda i: (layer, 0, 0))


def _post(h, mix_prompt, mix_sample, n, layer, wo, g_mlp, w1, w2, g_ple, wg, p_prompt, p_sample, wp, split_output):
    rows, d = h.shape
    tm = _row_tile(rows)
    assert n % tm == 0, (n, tm)
    prompt_tiles = n // tm
    d_ff = w1.shape[2]
    d_ple = p_prompt.shape[2]
    ff_chunk = 1024 if d_ff % 1024 == 0 else d_ff
    prompt_rows = lambda wd: pl.BlockSpec((tm, wd), lambda i: (jnp.minimum(i, prompt_tiles - 1), 0))
    sample_rows = lambda wd: pl.BlockSpec((tm, wd), lambda i: (jnp.maximum(i - prompt_tiles, 0), 0))
    h1, hid = pl.pallas_call(
        functools.partial(_post1_kernel, n_mix=len(mix_prompt), ff_chunk=ff_chunk, prompt_tiles=prompt_tiles),
        grid=(rows // tm,),
        in_specs=([_row_spec(tm, d)] + [prompt_rows(m.shape[1]) for m in mix_prompt]
                  + [sample_rows(mix_sample.shape[1]), _layer_spec(wo, layer), _full_spec(g_mlp.shape),
                     _layer_spec(w1, layer)]),
        out_specs=[_row_spec(tm, d), _row_spec(tm, d_ff)],
        out_shape=[jax.ShapeDtypeStruct((rows, d), F32), jax.ShapeDtypeStruct((rows, d_ff), BF16)],
        compiler_params=_cparams("arbitrary"),
        name="post_attn_mlp_up",
    )(h, *mix_prompt, mix_sample, wo, g_mlp, w1)
    if split_output:
        out_specs = [prompt_rows(d), sample_rows(d)]
        out_shape = [jax.ShapeDtypeStruct((n, d), F32), jax.ShapeDtypeStruct((rows - n, d), F32)]
    else:
        out_specs = _row_spec(tm, d)
        out_shape = jax.ShapeDtypeStruct((rows, d), F32)
    return pl.pallas_call(
        functools.partial(_post2_kernel, prompt_tiles=prompt_tiles),
        grid=(rows // tm,),
        in_specs=[_row_spec(tm, d), _row_spec(tm, d_ff), _layer_spec(w2, layer), _full_spec(g_ple.shape),
                  _layer_spec(wg, layer),
                  pl.BlockSpec((None, tm, d_ple), lambda i: (layer, jnp.minimum(i, prompt_tiles - 1), 0)),
                  pl.BlockSpec((None, tm, d_ple), lambda i: (layer, jnp.maximum(i - prompt_tiles, 0), 0)),
                  _layer_spec(wp, layer)],
        out_specs=out_specs,
        out_shape=out_shape,
        compiler_params=_cparams("arbitrary"),
        name="mlp_down_ple",
    )(h1, hid, w2, g_ple, wg, p_prompt, p_sample, wp)


def _cumsum_kernel(x_ref, o_ref):
    gs, nb, _ = x_ref.shape
    x2 = x_ref[...].reshape(gs * nb, LANES)
    r = lax.broadcasted_iota(jnp.int32, (LANES, LANES), 0)
    c = lax.broadcasted_iota(jnp.int32, (LANES, LANES), 1)
    upper = jnp.where(r <= c, 1.0, 0.0).astype(BF16)
    ones = jnp.ones((LANES, LANES), BF16)
    within = _dot_split3(x2, upper).reshape(gs, nb, LANES)
    total = _dot_split3(x2, ones).reshape(gs, nb, LANES)
    rb = lax.broadcasted_iota(jnp.int32, (nb, nb), 0)
    cb = lax.broadcasted_iota(jnp.int32, (nb, nb), 1)
    strict_lower = jnp.where(cb < rb, 1.0, 0.0).astype(BF16)
    for g in range(gs):
        hi, mid, lo = _split3(total[g])
        offs = (_dot(strict_lower, hi) + _dot(strict_lower, mid)) + _dot(strict_lower, lo)
        o_ref[g] = within[g] + offs


def _cumsum_lanes(x):
    n_in = x.shape[1]
    x = _pad_to(x, LANES * LANES, 1)
    groups, n = x.shape
    nb = n // LANES
    gs = min(groups, 16)
    out = pl.pallas_call(
        _cumsum_kernel,
        grid=(groups // gs,),
        in_specs=[pl.BlockSpec((gs, nb, LANES), lambda i: (i, 0, 0))],
        out_specs=pl.BlockSpec((gs, nb, LANES), lambda i: (i, 0, 0)),
        out_shape=jax.ShapeDtypeStruct((groups, nb, LANES), F32),
        compiler_params=_cparams("parallel"),
        name="logf_cumsum",
    )(x.reshape(groups, nb, LANES))
    return out.reshape(groups, n)[:, :n_in]


def _neg_suffix_matrix(tk):
    r = lax.broadcasted_iota(jnp.int32, (tk, tk), 0)
    c = lax.broadcasted_iota(jnp.int32, (tk, tk), 1)
    return jnp.where(r >= c, -1.0, 0.0).astype(BF16)


def _stick_tile(qrows, kb, vb, nsuf, carry, acc, mask, feature_major=False):
    slabs = qrows if isinstance(qrows, (list, tuple)) else None
    if slabs is not None:
        z = jnp.concatenate([_dot_nt(qs, kb[:, s * LANES:(s + 1) * LANES]) for s, qs in enumerate(slabs)], axis=0)
    else:
        z = _dot(qrows, kb) if feature_major else _dot_nt(qrows, kb)
    sp = _softplus(z)
    if mask is not None:
        sp = jnp.where(mask, sp, 0.0)
    incl = _dot_split2(sp, nsuf)
    w = jnp.exp(z + incl + carry)
    if mask is not None:
        w = jnp.where(mask, w, 0.0)
    wb = w.astype(BF16)
    if slabs is not None:
        rows = slabs[0].shape[0]
        pv = jnp.concatenate([_dot(wb[s * rows:(s + 1) * rows], vb[:, s * LANES:(s + 1) * LANES])
                              for s in range(len(slabs))], axis=0)
    else:
        pv = _dot_nt(wb, vb) if feature_major else _dot(wb, vb)
    acc = acc + pv
    carry = carry + incl[:, 0:1]
    return carry, acc


def _softmax_tile(logits, vb, m, l, acc, feature_major=False):
    m_new = jnp.maximum(m, jnp.max(logits, axis=1, keepdims=True))
    alpha = jnp.exp(m - m_new)
    p = jnp.exp(logits - m_new)
    l = alpha * l + jnp.sum(p, axis=1, keepdims=True)
    pb = p.astype(BF16)
    acc = alpha * acc + (_dot_nt(pb, vb) if feature_major else _dot(pb, vb))
    return m_new, l, acc


def _stack_heads(q, n_heads):
    t, width = q.shape
    rep = jnp.concatenate([q] * n_heads, axis=0) if n_heads > 1 else q
    row = lax.broadcasted_iota(jnp.int32, (n_heads * t, width), 0)
    lane = lax.broadcasted_iota(jnp.int32, (n_heads * t, width), 1)
    lo = (row // t) * HEAD_DIM
    keep = (lane >= lo) & (lane < lo + HEAD_DIM)
    return jnp.where(keep, rep, jnp.zeros_like(rep))


def _unstack_heads(acc, n_heads):
    rows, width = acc.shape
    t = rows // n_heads
    lane = lax.broadcasted_iota(jnp.int32, (t, width), 1)
    out = jnp.zeros((t, width), acc.dtype)
    for h in range(n_heads):
        sel = (lane >= h * HEAD_DIM) & (lane < (h + 1) * HEAD_DIM)
        out = jnp.where(sel, acc[h * t:(h + 1) * t, :], out)
    return out


def _row_query_index(rows, cols, t):
    r = lax.broadcasted_iota(jnp.int32, (rows, cols), 0)
    c = lax.broadcasted_iota(jnp.int32, (rows, cols), 1)
    return r % t, c


def _stick_prompt_kernel(q_ref, k_ref, v_ref, o_ref, carry_ref, acc_ref, *, t):
    i = pl.program_id(1)
    nh = HEADS_PER_SLAB
    n_slabs = q_ref.shape[1] // LANES
    rows = n_slabs * nh * t
    q = q_ref[...]
    qrows = [_stack_heads(q[:, s * LANES:(s + 1) * LANES], nh) for s in range(n_slabs)]
    nsuf = _neg_suffix_matrix(t)
    tq_idx, c_idx = _row_query_index(rows, t, t)

    def block(j):
        s = pl.multiple_of(j * t, t)
        return k_ref[pl.ds(s, t), :], v_ref[pl.ds(s, t), :]

    def diagonal():
        return _stick_tile(qrows, *block(i), nsuf, jnp.zeros((rows, 1), F32),
                           jnp.zeros((rows, LANES), F32), c_idx < tq_idx)

    @pl.when(i == 0)
    def _():
        carry, acc = diagonal()
        carry_ref[...] = carry
        acc_ref[...] = acc

    @pl.when(i > 0)
    def _():
        carry, acc = diagonal()
        carry, acc = _stick_tile(qrows, *block(i - 1), nsuf, carry, acc, None)
        carry_ref[...] = carry
        acc_ref[...] = acc

    def cond(state):
        j, carry_max = state
        return jnp.logical_and(j >= 0, carry_max > EXP_ZERO_BELOW)

    def body(state):
        j, _ = state
        carry, acc = _stick_tile(qrows, *block(j), nsuf, carry_ref[...], acc_ref[...], None)
        carry_ref[...] = carry
        acc_ref[...] = acc
        return j - 1, jnp.max(carry)

    lax.while_loop(cond, body, (i - 2, jnp.max(carry_ref[...])))
    acc = acc_ref[...]
    outs = [_unstack_heads(acc[s * nh * t:(s + 1) * nh * t], nh) for s in range(n_slabs)]
    o_ref[...] = (outs[0] if n_slabs == 1 else jnp.concatenate(outs, axis=1)).astype(o_ref.dtype)


def _stick_prompt(q, k, v, n, t=256, slabs_per_step=2):
    width = q.shape[1]
    t = min(t, n)
    group = slabs_per_step * LANES
    assert width % group == 0
    rows = slabs_per_step * HEADS_PER_SLAB * t
    return pl.pallas_call(
        functools.partial(_stick_prompt_kernel, t=t),
        grid=(width // group, n // t),
        in_specs=[pl.BlockSpec((t, group), lambda p, i: (i, p)),
                  pl.BlockSpec((n, group), lambda p, i: (0, p)),
                  pl.BlockSpec((n, group), lambda p, i: (0, p))],
        out_specs=pl.BlockSpec((t, group), lambda p, i: (i, p)),
        out_shape=jax.ShapeDtypeStruct((n, width), BF16),
        scratch_shapes=[pltpu.VMEM((rows, 1), F32), pltpu.VMEM((rows, LANES), F32)],
        compiler_params=_cparams("parallel", "parallel"),
        name="stick_prompt",
    )(q, k, v)


def _stick_sample_kernel(q_ref, kn_ref, vn_ref, ck_ref, cv_ref, nsuf_ref, o_ref, qrows_ref, carry_ref, acc_ref,
                         *, nh):
    jj = pl.program_id(1)
    t = q_ref.shape[1]

    @pl.when(jj == 0)
    def _():
        qrows = _stack_heads(q_ref[0], nh)
        qrows_ref[...] = qrows
        tq_idx, c_idx = _row_query_index(nh * t, t, t)
        carry, acc = _stick_tile(qrows, kn_ref[0], vn_ref[0], _neg_suffix_matrix(t),
                                 jnp.zeros((nh * t, 1), F32), jnp.zeros(acc_ref.shape, F32),
                                 c_idx < tq_idx)
        carry_ref[...] = carry
        acc_ref[...] = acc

    carry, acc = _stick_tile(qrows_ref[...], ck_ref[...].astype(BF16), cv_ref[...].astype(BF16),
                             nsuf_ref[...], carry_ref[...], acc_ref[...], None, feature_major=True)
    carry_ref[...] = carry
    acc_ref[...] = acc

    @pl.when(jj == pl.num_programs(1) - 1)
    def _():
        o_ref[0] = _unstack_heads(acc_ref[...], nh).astype(o_ref.dtype)


def _feature_major(cache):
    layers, b, past, heads, hd = cache.shape
    return jnp.transpose(cache, (0, 1, 3, 4, 2)).reshape(layers, b, heads * hd, past)


def _stick_sample(q, kn, vn, cache_kt, cache_vt, layer, tk=512):
    b, t, width = q.shape
    past = cache_kt.shape[3]
    tk = min(tk, past)
    nblk = past // tk
    nh = width // HEAD_DIM
    r = jnp.arange(tk)
    nsuf = jnp.where(r[:, None] >= r[None, :], -1.0, 0.0).astype(BF16)
    new_spec = pl.BlockSpec((1, t, width), lambda bi, jj: (bi, 0, 0))
    cache_spec = pl.BlockSpec((None, None, width, tk), lambda bi, jj: (layer, bi, 0, nblk - 1 - jj))
    return pl.pallas_call(
        functools.partial(_stick_sample_kernel, nh=nh),
        grid=(b, nblk),
        in_specs=[new_spec, new_spec, new_spec, cache_spec, cache_spec,
                  pl.BlockSpec((tk, tk), lambda bi, jj: (0, 0))],
        out_specs=pl.BlockSpec((1, t, width), lambda bi, jj: (bi, 0, 0)),
        out_shape=jax.ShapeDtypeStruct((b, t, width), BF16),
        scratch_shapes=[pltpu.VMEM((nh * t, width), BF16),
                        pltpu.VMEM((nh * t, 1), F32),
                        pltpu.VMEM((nh * t, width), F32)],
        compiler_params=_cparams("parallel", "arbitrary"),
        name="stick_sample",
    )(q, kn, vn, cache_kt, cache_vt, nsuf)


def _band_prompt_kernel(q_ref, k0_ref, k1_ref, k2_ref, v0_ref, v1_ref, v2_ref, bias_ref, o_ref,
                        *, t, nprev, fixed_shift):
    i = pl.program_id(1)
    nh = HEADS_PER_SLAB
    qrows = _stack_heads(q_ref[...], nh)
    krefs = (k0_ref, k1_ref, k2_ref)[3 - (nprev + 1):]
    vrefs = (v0_ref, v1_ref, v2_ref)[3 - (nprev + 1):]
    kcat = jnp.concatenate([r[...] for r in krefs], axis=0)
    vcat = jnp.concatenate([r[...] for r in vrefs], axis=0)
    logits = _dot_nt(qrows, kcat) + bias_ref[0]
    c_idx = lax.broadcasted_iota(jnp.int32, logits.shape, 1)
    logits = jnp.where(c_idx >= (nprev - i) * t, logits, MASKED)
    if fixed_shift:
        v_ones = jnp.concatenate([vcat, jnp.ones(vcat.shape, BF16)], axis=1)
        acc = _dot(jnp.exp(logits).astype(BF16), v_ones)
        acc = acc[:, :LANES] / acc[:, LANES:]
    else:
        m = jnp.max(logits, axis=1, keepdims=True)
        p = jnp.exp(logits - m)
        l = jnp.sum(p, axis=1, keepdims=True)
        acc = _dot(p.astype(BF16), vcat) / l
    o_ref[...] = _unstack_heads(acc, nh).astype(o_ref.dtype)


def _rel_bias(rel_table, nq, nk, q0, k0):
    m_len = nq + nk - 1
    u = np.arange(m_len)
    d = np.where(u < nk, (q0 - k0) - u, (q0 - k0) + (m_len - u))
    g = jnp.take(rel_table.astype(F32), np.clip(d, -REL_CLIP, REL_CLIP) + REL_CLIP, axis=1)
    n_heads = g.shape[0]
    bias = jnp.tile(g, (1, nq))[:, :nq * (m_len - 1)].reshape(n_heads, nq, m_len - 1)[:, :, :nk]
    qp = q0 + np.arange(nq)[:, None]
    kp = k0 + np.arange(nk)[None, :]
    qc, kc = qp // CHUNK, kp // CHUNK
    visible = (kp >= 0) & (kc <= qc) & (kc >= qc - BAND_CHUNKS)
    return bias, jnp.asarray(visible)[None]


def _band_prompt(q, k, v, rel_table, n, logit_bound, t=256):
    width = q.shape[1]
    t = min(t, n)
    assert t % CHUNK == 0
    nprev = min(-(-BAND // t), 2)
    assert nprev * t >= BAND
    slabs = width // LANES
    bias, visible = _rel_bias(rel_table, t, (nprev + 1) * t, nprev * t, 0)
    bias_shape = (slabs, HEADS_PER_SLAB * t, (nprev + 1) * t)

    def kv_spec(back):
        return pl.BlockSpec((t, LANES), lambda p, i: (jnp.maximum(i - back, 0), p))

    def call(masked_bias, fixed_shift):
        return pl.pallas_call(
            functools.partial(_band_prompt_kernel, t=t, nprev=nprev, fixed_shift=fixed_shift),
            grid=(slabs, n // t),
            in_specs=[pl.BlockSpec((t, LANES), lambda p, i: (i, p)),
                      kv_spec(2), kv_spec(1), kv_spec(0), kv_spec(2), kv_spec(1), kv_spec(0),
                      pl.BlockSpec((1,) + bias_shape[1:], lambda p, i: (p, 0, 0))],
            out_specs=pl.BlockSpec((t, LANES), lambda p, i: (i, p)),
            out_shape=jax.ShapeDtypeStruct((n, width), BF16),
            compiler_params=_cparams("parallel", "parallel"),
            name="band_prompt" if fixed_shift else "band_prompt_row_max",
        )(q, k, k, k, v, v, v, masked_bias.reshape(bias_shape))

    tab_max, tab_min = jnp.max(rel_table), jnp.min(rel_table)
    shift = logit_bound + tab_max
    spread = 2.0 * logit_bound + (tab_max - tab_min)
    return lax.cond(spread < 2.0 * MAX_FIXED_SHIFT,
                    lambda: call(jnp.where(visible, bias - shift, MASKED), True),
                    lambda: call(jnp.where(visible, bias, MASKED), False))


def _band_sample_kernel(q_ref, kn_ref, vn_ref, ck_ref, cv_ref, bc_ref, bn_ref, o_ref, *, nh):
    qrows = _stack_heads(q_ref[0], nh)
    lc = _dot(qrows, ck_ref[...].astype(BF16)) + bc_ref[...]
    ln = _dot_nt(qrows, kn_ref[0]) + bn_ref[...]
    m = jnp.maximum(jnp.max(lc, axis=1, keepdims=True), jnp.max(ln, axis=1, keepdims=True))
    pc = jnp.exp(lc - m)
    pn = jnp.exp(ln - m)
    l = jnp.sum(pc, axis=1, keepdims=True) + jnp.sum(pn, axis=1, keepdims=True)
    acc = (_dot_nt(pc.astype(BF16), cv_ref[...].astype(BF16)) + _dot(pn.astype(BF16), vn_ref[0])) / l
    o_ref[0] = _unstack_heads(acc, nh).astype(o_ref.dtype)


def _band_sample(q, kn, vn, cache_kt, cache_vt, layer, rel_table, past_len):
    b, t, width = q.shape
    buf_len = cache_kt.shape[3]
    nh = width // HEAD_DIM
    def masked_bias(nk, k0):
        bias, visible = _rel_bias(rel_table, t, nk, past_len, k0)
        return jnp.where(visible, bias, MASKED).reshape(nh * t, nk)

    bias_c = masked_bias(buf_len, past_len - buf_len)
    bias_n = masked_bias(t, past_len)
    new_spec = pl.BlockSpec((1, t, width), lambda bi: (bi, 0, 0))
    cache_spec = pl.BlockSpec((None, None, width, buf_len), lambda bi: (layer, bi, 0, 0))
    return pl.pallas_call(
        functools.partial(_band_sample_kernel, nh=nh),
        grid=(b,),
        in_specs=[new_spec, new_spec, new_spec, cache_spec, cache_spec,
                  _full_spec(bias_c.shape), _full_spec(bias_n.shape)],
        out_specs=pl.BlockSpec((1, t, width), lambda bi: (bi, 0, 0)),
        out_shape=jax.ShapeDtypeStruct((b, t, width), BF16),
        compiler_params=_cparams("parallel"),
        name="band_sample",
    )(q, kn, vn, cache_kt, cache_vt, bias_c, bias_n)


def _key_decay_rows(dk, t):
    n_heads, tk = dk.shape
    return jnp.concatenate([jnp.broadcast_to(dk[h:h + 1, :], (t, tk)) for h in range(n_heads)], axis=0)


def _fox_prompt_kernel(q_ref, dq_ref, k_ref, v_ref, dk_ref, o_ref, m_ref, l_ref, acc_ref, *, t):
    i = pl.program_id(1)
    nh = HEADS_PER_SLAB
    qrows = _stack_heads(q_ref[...], nh)
    dq = jnp.concatenate([dq_ref[:, h:h + 1] for h in range(nh)], axis=0)
    m_ref[...] = jnp.full(m_ref.shape, MASKED, F32)
    l_ref[...] = jnp.zeros(l_ref.shape, F32)
    acc_ref[...] = jnp.zeros(acc_ref.shape, F32)

    def tile(j, mask):
        s = pl.multiple_of(j * t, t)
        z = _dot_nt(qrows, k_ref[pl.ds(s, t), :])
        logits = (z + dq) - _key_decay_rows(dk_ref[:, pl.ds(s, t)], t)
        if mask is not None:
            logits = jnp.where(mask, logits, MASKED)
        m, l, acc = _softmax_tile(logits, v_ref[pl.ds(s, t), :], m_ref[...], l_ref[...], acc_ref[...])
        m_ref[...] = m
        l_ref[...] = l
        acc_ref[...] = acc

    def body(j, carry):
        tile(j, None)
        return carry

    lax.fori_loop(0, i, body, 0)
    tq_idx, c_idx = _row_query_index(nh * t, t, t)
    tile(i, c_idx <= tq_idx)
    o_ref[...] = _unstack_heads(acc_ref[...] / l_ref[...], nh).astype(o_ref.dtype)


def _fox_prompt_bounded_kernel(shift_ref, first_ref, q_ref, dq_ref, k_ref, v_ref, dk_ref, o_ref,
                               dqrep_ref, acc_ref, z_ref, *, tq, tk):
    slab = pl.program_id(0)
    i = pl.program_id(1)
    ratio = tq // tk
    q = q_ref[...]
    ones = jnp.ones((tk, LANES), BF16)
    row = lax.broadcasted_iota(jnp.int32, (tq, LANES), 0)
    lane = lax.broadcasted_iota(jnp.int32, (tq, LANES), 1)
    outs = []
    for hh in range(HEADS_PER_SLAB):
        qm = jnp.where((lane >= hh * HEAD_DIM) & (lane < (hh + 1) * HEAD_DIM), q, jnp.zeros_like(q))
        dqrep_ref[...] = jnp.broadcast_to(dq_ref[:, hh:hh + 1] - shift_ref[0, 0], (tq, LANES))
        acc_ref[...] = jnp.zeros(acc_ref.shape, F32)

        def scores(j, r0, qm=qm):
            s = pl.multiple_of(j * tk, tk)
            return _dot_nt(qm[r0:], k_ref[pl.ds(s, tk), :])

        def tile(j, r0, causal, next_r0, hh=hh, scores=scores):
            s = pl.multiple_of(j * tk, tk)
            z = z_ref[r0:, :]
            dqh = dqrep_ref[r0:, :]
            cols = []
            for c in range(tk // LANES):
                lg = (z[:, c * LANES:(c + 1) * LANES] + dqh) - dk_ref[hh:hh + 1, pl.ds(s + c * LANES, LANES)]
                if causal:
                    key_pos = s + c * LANES + lax.broadcasted_iota(jnp.int32, lg.shape, 1)
                    query_pos = i * tq + r0 + lax.broadcasted_iota(jnp.int32, lg.shape, 0)
                    lg = jnp.where(key_pos <= query_pos, lg, MASKED)
                cols.append(jnp.exp(lg).astype(BF16))
            if next_r0 is not None:
                z_next = scores(j + 1, next_r0)
            v_ones = jnp.concatenate([v_ref[pl.ds(s, tk), :], ones], axis=1)
            acc_ref[r0:, :] += _dot(jnp.concatenate(cols, axis=1), v_ones)
            if next_r0 is not None:
                z_ref[next_r0:, :] = z_next

        start = first_ref[slab * HEADS_PER_SLAB + hh, i]
        z_ref[...] = scores(start, 0)

        def body(j, carry, tile=tile):
            tile(j, 0, False, 0)
            return carry

        lax.fori_loop(start, i * ratio, body, 0)
        for d in range(ratio):
            tile(i * ratio + d, d * tk, True, (d + 1) * tk if d + 1 < ratio else None)
        acc = acc_ref[...]
        outs.append(acc[:, :LANES] / acc[:, LANES:])
    o_ref[...] = jnp.where(lane < HEAD_DIM, outs[0], outs[1]).astype(o_ref.dtype)


MAX_FIXED_SHIFT = 30.0


def _fox_prompt(q, k, v, cum, n, logit_bound, t_exact=256, tq_bounded=1024, tk_bounded=512):
    width = q.shape[1]
    slabs = width // LANES
    nh = HEADS_PER_SLAB
    dq = cum.reshape(n, slabs, nh).transpose(1, 0, 2)
    dk = cum.reshape(n, slabs, nh).transpose(1, 2, 0)

    def specs(t):
        return ([pl.BlockSpec((t, LANES), lambda p, i: (i, p)),
                 pl.BlockSpec((None, t, nh), lambda p, i: (p, i, 0)),
                 pl.BlockSpec((n, LANES), lambda p, i: (0, p)),
                 pl.BlockSpec((n, LANES), lambda p, i: (0, p)),
                 pl.BlockSpec((None, nh, n), lambda p, i: (p, 0, 0))],
                pl.BlockSpec((t, LANES), lambda p, i: (i, p)))

    def exact(_):
        t = min(t_exact, n)
        in_specs, out_spec = specs(t)
        return pl.pallas_call(
            functools.partial(_fox_prompt_kernel, t=t),
            grid=(slabs, n // t),
            in_specs=in_specs, out_specs=out_spec,
            out_shape=jax.ShapeDtypeStruct((n, width), BF16),
            scratch_shapes=[pltpu.VMEM((nh * t, 1), F32), pltpu.VMEM((nh * t, 1), F32),
                            pltpu.VMEM((nh * t, LANES), F32)],
            compiler_params=_cparams("parallel", "parallel"),
            name="fox_prompt_running_max",
        )(q, dq, k, v, dk)

    def bounded(shift):
        tq = min(tq_bounded, n)
        tk = min(tk_bounded, n)
        in_specs, out_spec = specs(tq)
        cum_t = cum.T
        decay = cum_t[:, ::tq][:, :, None] - cum_t[:, tk - 1::tk][:, None, :]
        before = jnp.arange(n // tk)[None, None, :] < (jnp.arange(n // tq) * (tq // tk))[None, :, None]
        dead = jnp.logical_and(decay < EXP_ZERO_BELOW, before)
        first = jnp.sum(jnp.cumprod(dead.astype(jnp.int32), axis=2), axis=2).astype(jnp.int32)
        smem = pl.BlockSpec(memory_space=pltpu.SMEM)
        return pl.pallas_call(
            functools.partial(_fox_prompt_bounded_kernel, tq=tq, tk=tk),
            grid=(slabs, n // tq),
            in_specs=[smem, smem] + in_specs, out_specs=out_spec,
            out_shape=jax.ShapeDtypeStruct((n, width), BF16),
            scratch_shapes=[pltpu.VMEM((tq, LANES), F32), pltpu.VMEM((tq, 2 * LANES), F32),
                            pltpu.VMEM((tq, tk), F32)],
            compiler_params=_cparams("parallel", "parallel"),
            name="fox_prompt",
        )(shift.reshape(1, 1), first, q, dq, k, v, dk)

    return lax.cond(logit_bound < MAX_FIXED_SHIFT, bounded, exact, logit_bound)


def _fox_sample_kernel(q_ref, dq_ref, kn_ref, vn_ref, dkn_ref, ck_ref, cv_ref, dkc_ref, o_ref,
                       qrows_ref, m_ref, l_ref, acc_ref, *, nh):
    jj = pl.program_id(1)
    t = q_ref.shape[1]
    dq = dq_ref[0]

    @pl.when(jj == 0)
    def _():
        qrows = _stack_heads(q_ref[0], nh)
        qrows_ref[...] = qrows
        tq_idx, c_idx = _row_query_index(nh * t, t, t)
        logits = (_dot_nt(qrows, kn_ref[0]) + dq) - _key_decay_rows(dkn_ref[0], t)
        logits = jnp.where(c_idx <= tq_idx, logits, MASKED)
        m, l, acc = _softmax_tile(logits, vn_ref[0], jnp.full(m_ref.shape, MASKED, F32),
                                  jnp.zeros(l_ref.shape, F32), jnp.zeros(acc_ref.shape, F32))
        m_ref[...] = m
        l_ref[...] = l
        acc_ref[...] = acc

    logits = (_dot(qrows_ref[...], ck_ref[...].astype(BF16)) + dq) - _key_decay_rows(dkc_ref[0], t)
    m, l, acc = _softmax_tile(logits, cv_ref[...].astype(BF16), m_ref[...], l_ref[...], acc_ref[...],
                              feature_major=True)
    m_ref[...] = m
    l_ref[...] = l
    acc_ref[...] = acc

    @pl.when(jj == pl.num_programs(1) - 1)
    def _():
        o_ref[0] = _unstack_heads(acc_ref[...] / l_ref[...], nh).astype(o_ref.dtype)


def _fox_sample(q, kn, vn, cache_kt, cache_vt, layer, cum_cache, cum_new, tk=512):
    b, t, width = q.shape
    past = cache_kt.shape[3]
    tk = min(tk, past)
    nblk = past // tk
    nh = width // HEAD_DIM
    dq = cum_new.reshape(b, nh * t, 1)
    new_spec = pl.BlockSpec((1, t, width), lambda bi, jj: (bi, 0, 0))
    cache_spec = pl.BlockSpec((None, None, width, tk), lambda bi, jj: (layer, bi, 0, jj))
    return pl.pallas_call(
        functools.partial(_fox_sample_kernel, nh=nh),
        grid=(b, nblk),
        in_specs=[new_spec, pl.BlockSpec((1, nh * t, 1), lambda bi, jj: (bi, 0, 0)),
                  new_spec, new_spec, pl.BlockSpec((1, nh, t), lambda bi, jj: (bi, 0, 0)),
                  cache_spec, cache_spec, pl.BlockSpec((1, nh, tk), lambda bi, jj: (bi, 0, jj))],
        out_specs=pl.BlockSpec((1, t, width), lambda bi, jj: (bi, 0, 0)),
        out_shape=jax.ShapeDtypeStruct((b, t, width), BF16),
        scratch_shapes=[pltpu.VMEM((nh * t, width), BF16), pltpu.VMEM((nh * t, 1), F32),
                        pltpu.VMEM((nh * t, 1), F32), pltpu.VMEM((nh * t, width), F32)],
        compiler_params=_cparams("parallel", "arbitrary"),
        name="fox_sample",
    )(q, dq, kn, vn, cum_new, cache_kt, cache_vt, cum_cache)


def _pad_to(x, multiple, axis):
    size = x.shape[axis]
    target = -(-size // multiple) * multiple
    if target == size:
        return x
    pad = [(0, 0)] * x.ndim
    pad[axis] = (0, target - size)
    return jnp.pad(x, pad)


def kernel(x_prompt, x_sample, cache_a_k, cache_a_v, cache_b_k, cache_b_v, cache_c_k, cache_c_v, cache_c_logf, p_prompt, p_sample, g_mix, w_in_even, g_qb, g_kb, rel_bias, w_in_odd, b_forget, g_qc, g_kc, w_out, g_mlp, w_ff1, w_ff2, g_ple, w_ple_gate, w_ple_proj):
    bp, n, d = x_prompt.shape
    bs, ts, _ = x_sample.shape
    assert bp == 1, "prompt kernels assume a single prompt stream"
    depth = g_mix.shape[0]
    past = cache_a_k.shape[2]
    buf_len = cache_b_k.shape[2]
    ha, hb, hc = cache_a_k.shape[3], cache_b_k.shape[3], cache_c_k.shape[3]
    wa, wb, wc = ha * HEAD_DIM, hb * HEAD_DIM, hc * HEAD_DIM
    rows_s = bs * ts

    blk = jnp.arange(NORM_SLAB) // HEAD_DIM
    gmat = jnp.where(blk[:, None] == blk[None, :], 1.0 / HEAD_DIM, 0.0).astype(BF16)

    cache_a_kt, cache_a_vt = _feature_major(cache_a_k), _feature_major(cache_a_v)
    cache_b_kt, cache_b_vt = _feature_major(cache_b_k), _feature_major(cache_b_v)
    cache_c_kt, cache_c_vt = _feature_major(cache_c_k), _feature_major(cache_c_v)

    rows = n + rows_s
    tail = -rows % ROW_TILE
    h = jnp.concatenate([x_prompt.reshape(n, d), x_sample.reshape(rows_s, d), jnp.zeros((tail, d), F32)], axis=0)
    sample_pad = rows + tail - n
    d_ple = p_prompt.shape[-1]
    p_prompt2 = p_prompt.reshape(depth, n, d_ple)
    p_sample2 = _pad_to(p_sample.reshape(depth, rows_s, d_ple), sample_pad, 1)
    w_even16 = w_in_even.astype(BF16)
    w_odd16 = _pad_to(w_in_odd, LANES, 2).astype(BF16)
    w_out16, w_ff1_16, w_ff2_16 = w_out.astype(BF16), w_ff1.astype(BF16), w_ff2.astype(BF16)
    w_gate16, w_proj16 = w_ple_gate.astype(BF16), w_ple_proj.astype(BF16)

    def prompt_state(xt, heads, lo=0):
        return jnp.transpose(xt[:, lo:].reshape(heads, HEAD_DIM, n - lo), (2, 0, 1))[None]

    def sample_state(xt, heads):
        return jnp.transpose(xt[:, :rows_s].reshape(heads, HEAD_DIM, bs, ts), (2, 3, 0, 1))

    st = {name: [] for name in ("pa_k", "pa_v", "pb_k", "pb_v", "pc_k", "pc_v", "pc_f",
                                "sa_k", "sa_v", "sb_k", "sb_v", "sc_k", "sc_v", "sc_f")}
    for i in range(depth):
        j = i // 2
        g_row = g_mix[i].reshape(1, d)
        if i % 2 == 0:
            gq_row = jnp.tile(g_qb[j], hb).reshape(1, wb)
            gk_row = jnp.tile(g_kb[j], hb).reshape(1, wb)
            qa, ka16, va16, qb, kb16, vb16, ka_p, va_p, kb_p, vb_p, ka_s, va_s, kb_s, vb_s = _proj_even(
                h, g_row, w_even16, j, gq_row, gk_row, gmat, wa, wb, n)
            mix_a_p = _stick_prompt(qa, ka16, va16, n)
            band_bound = 1.01 * HEAD_DIM * SCALE * jnp.max(jnp.abs(g_qb[j])) * jnp.max(jnp.abs(g_kb[j]))
            mix_b_p = _band_prompt(qb, kb16, vb16, rel_bias[j], n, band_bound.astype(F32))
            s3 = lambda a: a[n:rows].reshape(bs, ts, a.shape[1])
            mix_a_s = _stick_sample(s3(qa), s3(ka16), s3(va16), cache_a_kt, cache_a_vt, j)
            mix_b_s = _band_sample(s3(qb), s3(kb16), s3(vb16), cache_b_kt, cache_b_vt, j, rel_bias[j], past)
            mix_prompt = [mix_a_p, mix_b_p]
            mix_sample = _pad_to(jnp.concatenate([mix_a_s.reshape(rows_s, wa), mix_b_s.reshape(rows_s, wb)],
                                                 axis=1), sample_pad, 0)
            st["pa_k"].append(prompt_state(ka_p, ha))
            st["pa_v"].append(prompt_state(va_p, ha))
            keep = min(BAND, n)
            st["pb_k"].append(prompt_state(kb_p, hb, n - keep))
            st["pb_v"].append(prompt_state(vb_p, hb, n - keep))
            st["sa_k"].append(sample_state(ka_s, ha))
            st["sa_v"].append(sample_state(va_s, ha))
            st["sb_k"].append(jnp.concatenate([cache_b_k[j], sample_state(kb_s, hb)], axis=1)[:, ts:])
            st["sb_v"].append(jnp.concatenate([cache_b_v[j], sample_state(vb_s, hb)], axis=1)[:, ts:])
        else:
            gq_row = jnp.tile(g_qc[j], hc).reshape(1, wc)
            gk_row = jnp.tile(g_kc[j], hc).reshape(1, wc)
            bf_row = _pad_to(b_forget[j].reshape(1, hc), LANES, 1)
            q, k16, v16, lf, k_p, v_p, k_s, v_s = _proj_odd(h, g_row, w_odd16, j, gq_row, gk_row, gmat, bf_row,
                                                            wc, n)
            log_f = lf[:, :hc]
            cum_p = _cumsum_lanes(log_f[:n].T)
            logit_bound = 1.01 * HEAD_DIM * SCALE * jnp.max(jnp.abs(g_qc[j])) * jnp.max(jnp.abs(g_kc[j]))
            mix_p = _fox_prompt(q, k16, v16, cum_p.T, n, logit_bound.astype(F32))
            lf_new = log_f[n:rows].reshape(bs, ts, hc).transpose(0, 2, 1)
            lf_all = jnp.concatenate([cache_c_logf[j].transpose(0, 2, 1), lf_new], axis=2)
            cum_s = _cumsum_lanes(lf_all.reshape(bs * hc, past + ts)).reshape(bs, hc, past + ts)
            s3 = lambda a: a[n:rows].reshape(bs, ts, a.shape[1])
            mix_s = _fox_sample(s3(q), s3(k16), s3(v16), cache_c_kt, cache_c_vt, j,
                                cum_s[:, :, :past], cum_s[:, :, past:past + ts])
            mix_prompt = [mix_p]
            mix_sample = _pad_to(mix_s.reshape(rows_s, wc), sample_pad, 0)
            st["pc_k"].append(prompt_state(k_p, hc))
            st["pc_v"].append(prompt_state(v_p, hc))
            st["pc_f"].append(log_f[:n].reshape(1, n, hc))
            st["sc_k"].append(sample_state(k_s, hc))
            st["sc_v"].append(sample_state(v_s, hc))
            st["sc_f"].append(log_f[n:rows].reshape(bs, ts, hc))
        h = _post(h, mix_prompt, mix_sample, n, i, w_out16, g_mlp[i].reshape(1, d), w_ff1_16, w_ff2_16,
                  g_ple[i].reshape(1, d), w_gate16, p_prompt2, p_sample2, w_proj16, split_output=i == depth - 1)

    y_prompt = h[0].reshape(1, n, d)
    y_sample = h[1][:rows_s].reshape(bs, ts, d)
    stk = {name: jnp.stack(vals) for name, vals in st.items()}
    return (y_prompt, y_sample, stk["pa_k"], stk["pa_v"], stk["pb_k"], stk["pb_v"],
            stk["pc_k"], stk["pc_v"], stk["pc_f"], stk["sa_k"], stk["sa_v"], stk["sb_k"], stk["sb_v"],
            stk["sc_k"], stk["sc_v"], stk["sc_f"])
```

```python
import functools

import numpy as np
import jax
import jax.numpy as jnp
from jax import lax
from jax.experimental import pallas as pl
from jax.experimental.pallas import tpu as pltpu

F32 = jnp.float32
BF16 = jnp.bfloat16

HEAD_DIM = 64
CHUNK = 64
BAND_CHUNKS = 8
BAND = BAND_CHUNKS * CHUNK
REL_CLIP = 128
EPS = 1e-6
SCALE = HEAD_DIM ** -0.5
LANES = 128
HEADS_PER_SLAB = LANES // HEAD_DIM
NORM_SLAB = 256
EXP_ZERO_BELOW = -104.0
MASKED = -1e30
VMEM_LIMIT = 56 * 1024 * 1024


def _cparams(*sem):
    return pltpu.CompilerParams(dimension_semantics=sem, vmem_limit_bytes=VMEM_LIMIT)


def _split2(x):
    hi = x.astype(BF16)
    lo = (x - hi.astype(F32)).astype(BF16)
    return hi, lo


def _split3(x):
    hi = x.astype(BF16)
    r = x - hi.astype(F32)
    mid = r.astype(BF16)
    lo = (r - mid.astype(F32)).astype(BF16)
    return hi, mid, lo


def _dot(a, b):
    return jnp.dot(a, b, preferred_element_type=F32)


def _dot_nt(a, b):
    return lax.dot_general(a, b, (((1,), (1,)), ((), ())), preferred_element_type=F32)


def _dot_split2(x, m):
    hi, lo = _split2(x)
    return _dot(hi, m) + _dot(lo, m)


def _dot_split3(x, m):
    hi, mid, lo = _split3(x)
    return (_dot(hi, m) + _dot(mid, m)) + _dot(lo, m)


def _rms_rows(x, g):
    ms = jnp.mean(x * x, axis=-1, keepdims=True)
    return x * lax.rsqrt(ms + EPS) * g


def _head_rms(x, gmat, gain_row):
    width = x.shape[1]
    outs = []
    for s in range(width // NORM_SLAB):
        xs = x[:, NORM_SLAB * s:NORM_SLAB * (s + 1)]
        ms = _dot_split2(xs * xs, gmat)
        outs.append(xs * lax.rsqrt(ms + EPS))
    y = outs[0] if len(outs) == 1 else jnp.concatenate(outs, axis=1)
    return y * gain_row


def _softplus(z):
    return jnp.maximum(z, 0.0) + jnp.log(1.0 + jnp.exp(-jnp.abs(z)))


def _log_sigmoid(z):
    return jnp.minimum(z, 0.0) - jnp.log(1.0 + jnp.exp(-jnp.abs(z)))


def _store_state(values, prompt_refs, sample_refs, prompt_tiles):
    step = pl.program_id(0)

    @pl.when(step < prompt_tiles)
    def _():
        for value, ref in zip(values, prompt_refs):
            ref[...] = value.T

    @pl.when(step >= prompt_tiles)
    def _():
        for value, ref in zip(values, sample_refs):
            ref[...] = value.T


def _proj_even_kernel(h_ref, g_ref, w_ref, gq_ref, gk_ref, gmat_ref,
                      qa_ref, ka16_ref, va16_ref, qb_ref, kb16_ref, vb16_ref,
                      ka_p_ref, va_p_ref, kb_p_ref, vb_p_ref, ka_s_ref, va_s_ref, kb_s_ref, vb_s_ref,
                      *, wa, wb, prompt_tiles):
    xn = _rms_rows(h_ref[...], g_ref[...]).astype(BF16)
    proj = _dot(xn, w_ref[...])
    gmat = gmat_ref[...]
    qa = proj[:, 0:wa]
    ka = proj[:, wa:2 * wa]
    va = proj[:, 2 * wa:3 * wa]
    o = 3 * wa
    qb = _head_rms(proj[:, o:o + wb], gmat, gq_ref[...])
    kb = _head_rms(proj[:, o + wb:o + 2 * wb], gmat, gk_ref[...])
    vb = proj[:, o + 2 * wb:o + 3 * wb]
    qa_ref[...] = (qa * SCALE).astype(BF16)
    ka16_ref[...] = ka.astype(BF16)
    va16_ref[...] = va.astype(BF16)
    qb_ref[...] = (qb * SCALE).astype(BF16)
    kb16_ref[...] = kb.astype(BF16)
    vb16_ref[...] = vb.astype(BF16)
    _store_state([ka, va, kb, vb], [ka_p_ref, va_p_ref, kb_p_ref, vb_p_ref],
                 [ka_s_ref, va_s_ref, kb_s_ref, vb_s_ref], prompt_tiles)


def _proj_odd_kernel(h_ref, g_ref, w_ref, gq_ref, gk_ref, gmat_ref, bf_ref,
                     q_ref, k16_ref, v16_ref, lf_ref, k_p_ref, v_p_ref, k_s_ref, v_s_ref, *, wc, prompt_tiles):
    xn = _rms_rows(h_ref[...], g_ref[...]).astype(BF16)
    proj = _dot(xn, w_ref[...])
    gmat = gmat_ref[...]
    q = _head_rms(proj[:, 0:wc], gmat, gq_ref[...])
    k = _head_rms(proj[:, wc:2 * wc], gmat, gk_ref[...])
    v = proj[:, 2 * wc:3 * wc]
    f = proj[:, 3 * wc:3 * wc + LANES]
    q_ref[...] = (q * SCALE).astype(BF16)
    k16_ref[...] = k.astype(BF16)
    v16_ref[...] = v.astype(BF16)
    _store_state([k, v], [k_p_ref, v_p_ref], [k_s_ref, v_s_ref], prompt_tiles)
    lf_ref[...] = _log_sigmoid(f + bf_ref[...])


def _post1_kernel(*refs, n_mix, ff_chunk, prompt_tiles):
    h_ref, mix_p_refs, mix_s_ref = refs[0], refs[1:1 + n_mix], refs[1 + n_mix]
    wo_ref, g_ref, w1_ref, h1_ref, hid_ref = refs[2 + n_mix:]
    pieces = [r[...] for r in mix_p_refs]
    mix_p = pieces[0] if n_mix == 1 else jnp.concatenate(pieces, axis=1)
    mix = jnp.where(pl.program_id(0) < prompt_tiles, mix_p, mix_s_ref[...])
    h1 = h_ref[...] + _dot(mix, wo_ref[...])
    h1_ref[...] = h1
    xn = _rms_rows(h1, g_ref[...]).astype(BF16)
    d_ff = w1_ref.shape[1]
    for c in range(d_ff // ff_chunk):
        sl = slice(c * ff_chunk, (c + 1) * ff_chunk)
        a = jnp.maximum(_dot(xn, w1_ref[:, sl]), 0.0)
        hid_ref[:, sl] = (a * a).astype(BF16)


def _post2_kernel(h1_ref, hid_ref, w2_ref, g_ref, wg_ref, p_p_ref, p_s_ref, wp_ref, *out_refs, prompt_tiles):
    step = pl.program_id(0)
    h2 = h1_ref[...] + _dot(hid_ref[...], w2_ref[...])
    xg = _rms_rows(h2, g_ref[...]).astype(BF16)
    gate = jax.nn.sigmoid(_dot(xg, wg_ref[...]))
    p = jnp.where(step < prompt_tiles, p_p_ref[...], p_s_ref[...]).astype(BF16)
    out = h2 + gate * _dot(p, wp_ref[...])
    if len(out_refs) == 1:
        out_refs[0][...] = out
    else:
        @pl.when(step < prompt_tiles)
        def _():
            out_refs[0][...] = out

        @pl.when(step >= prompt_tiles)
        def _():
            out_refs[1][...] = out


def _row_spec(tm, width):
    return pl.BlockSpec((tm, width), lambda i: (i, 0))


def _full_spec(shape):
    return pl.BlockSpec(shape, lambda i: (0,) * len(shape))


ROW_TILE = 512


def _row_tile(rows):
    assert rows % ROW_TILE == 0, rows
    return ROW_TILE


def _proj_call(body, name, inputs, layer, n, row_widths, row_dtypes, state_widths):
    h = inputs[0]
    rows, d = h.shape
    tm = _row_tile(rows)
    assert n % tm == 0, (n, tm)
    prompt_tiles = n // tm
    prompt_spec = lambda wd: pl.BlockSpec((None, wd, tm), lambda i: (0, 0, jnp.minimum(i, prompt_tiles - 1)))
    sample_spec = lambda wd: pl.BlockSpec((wd, tm), lambda i: (0, jnp.maximum(i - prompt_tiles, 0)))
    return pl.pallas_call(
        functools.partial(body, prompt_tiles=prompt_tiles),
        grid=(rows // tm,),
        in_specs=[_row_spec(tm, d)] + [_layer_spec(x, layer) if x.ndim == 3 else _full_spec(x.shape)
                                       for x in inputs[1:]],
        out_specs=([_row_spec(tm, wd) for wd in row_widths] + [prompt_spec(wd) for wd in state_widths]
                   + [sample_spec(wd) for wd in state_widths]),
        out_shape=([jax.ShapeDtypeStruct((rows, wd), dt) for wd, dt in zip(row_widths, row_dtypes)]
                   + [jax.ShapeDtypeStruct((1, wd, n), F32) for wd in state_widths]
                   + [jax.ShapeDtypeStruct((wd, rows - n), F32) for wd in state_widths]),
        compiler_params=_cparams("arbitrary"),
        name=name,
    )(*inputs)


def _proj_even(h, g, w_stack, layer, gq_row, gk_row, gmat, wa, wb, n):
    return _proj_call(functools.partial(_proj_even_kernel, wa=wa, wb=wb), "proj_even",
                      (h, g, w_stack, gq_row, gk_row, gmat), layer, n,
                      [wa, wa, wa, wb, wb, wb], [BF16] * 6, [wa, wa, wb, wb])


def _proj_odd(h, g, w_stack, layer, gq_row, gk_row, gmat, bf_row, wc, n):
    return _proj_call(functools.partial(_proj_odd_kernel, wc=wc), "proj_odd",
                      (h, g, w_stack, gq_row, gk_row, gmat, bf_row), layer, n,
                      [wc, wc, wc, LANES], [BF16] * 3 + [F32], [wc, wc])


def _layer_spec(stacked, layer):
    return pl.BlockSpec((None,) + stacked.shape[1:], lambda i: (layer, 0, 0))


def _post(h, mix_prompt, mix_sample, n, layer, wo, g_mlp, w1, w2, g_ple, wg, p_prompt, p_sample, wp, split_output):
    rows, d = h.shape
    tm = _row_tile(rows)
    assert n % tm == 0, (n, tm)
    prompt_tiles = n // tm
    d_ff = w1.shape[2]
    d_ple = p_prompt.shape[2]
    ff_chunk = 1024 if d_ff % 1024 == 0 else d_ff
    prompt_rows = lambda wd: pl.BlockSpec((tm, wd), lambda i: (jnp.minimum(i, prompt_tiles - 1), 0))
    sample_rows = lambda wd: pl.BlockSpec((tm, wd), lambda i: (jnp.maximum(i - prompt_tiles, 0), 0))
    h1, hid = pl.pallas_call(
        functools.partial(_post1_kernel, n_mix=len(mix_prompt), ff_chunk=ff_chunk, prompt_tiles=prompt_tiles),
        grid=(rows // tm,),
        in_specs=([_row_spec(tm, d)] + [prompt_rows(m.shape[1]) for m in mix_prompt]
                  + [sample_rows(mix_sample.shape[1]), _layer_spec(wo, layer), _full_spec(g_mlp.shape),
                     _layer_spec(w1, layer)]),
        out_specs=[_row_spec(tm, d), _row_spec(tm, d_ff)],
        out_shape=[jax.ShapeDtypeStruct((rows, d), F32), jax.ShapeDtypeStruct((rows, d_ff), BF16)],
        compiler_params=_cparams("arbitrary"),
        name="post_attn_mlp_up",
    )(h, *mix_prompt, mix_sample, wo, g_mlp, w1)
    if split_output:
        out_specs = [prompt_rows(d), sample_rows(d)]
        out_shape = [jax.ShapeDtypeStruct((n, d), F32), jax.ShapeDtypeStruct((rows - n, d), F32)]
    else:
        out_specs = _row_spec(tm, d)
        out_shape = jax.ShapeDtypeStruct((rows, d), F32)
    return pl.pallas_call(
        functools.partial(_post2_kernel, prompt_tiles=prompt_tiles),
        grid=(rows // tm,),
        in_specs=[_row_spec(tm, d), _row_spec(tm, d_ff), _layer_spec(w2, layer), _full_spec(g_ple.shape),
                  _layer_spec(wg, layer),
                  pl.BlockSpec((None, tm, d_ple), lambda i: (layer, jnp.minimum(i, prompt_tiles - 1), 0)),
                  pl.BlockSpec((None, tm, d_ple), lambda i: (layer, jnp.maximum(i - prompt_tiles, 0), 0)),
                  _layer_spec(wp, layer)],
        out_specs=out_specs,
        out_shape=out_shape,
        compiler_params=_cparams("arbitrary"),
        name="mlp_down_ple",
    )(h1, hid, w2, g_ple, wg, p_prompt, p_sample, wp)


def _cumsum_kernel(x_ref, o_ref):
    gs, nb, _ = x_ref.shape
    x2 = x_ref[...].reshape(gs * nb, LANES)
    r = lax.broadcasted_iota(jnp.int32, (LANES, LANES), 0)
    c = lax.broadcasted_iota(jnp.int32, (LANES, LANES), 1)
    upper = jnp.where(r <= c, 1.0, 0.0).astype(BF16)
    ones = jnp.ones((LANES, LANES), BF16)
    within = _dot_split3(x2, upper).reshape(gs, nb, LANES)
    total = _dot_split3(x2, ones).reshape(gs, nb, LANES)
    rb = lax.broadcasted_iota(jnp.int32, (nb, nb), 0)
    cb = lax.broadcasted_iota(jnp.int32, (nb, nb), 1)
    strict_lower = jnp.where(cb < rb, 1.0, 0.0).astype(BF16)
    for g in range(gs):
        hi, mid, lo = _split3(total[g])
        offs = (_dot(strict_lower, hi) + _dot(strict_lower, mid)) + _dot(strict_lower, lo)
        o_ref[g] = within[g] + offs


def _cumsum_lanes(x):
    n_in = x.shape[1]
    x = _pad_to(x, LANES * LANES, 1)
    groups, n = x.shape
    nb = n // LANES
    gs = min(groups, 16)
    out = pl.pallas_call(
        _cumsum_kernel,
        grid=(groups // gs,),
        in_specs=[pl.BlockSpec((gs, nb, LANES), lambda i: (i, 0, 0))],
        out_specs=pl.BlockSpec((gs, nb, LANES), lambda i: (i, 0, 0)),
        out_shape=jax.ShapeDtypeStruct((groups, nb, LANES), F32),
        compiler_params=_cparams("parallel"),
        name="logf_cumsum",
    )(x.reshape(groups, nb, LANES))
    return out.reshape(groups, n)[:, :n_in]


def _neg_suffix_matrix(tk):
    r = lax.broadcasted_iota(jnp.int32, (tk, tk), 0)
    c = lax.broadcasted_iota(jnp.int32, (tk, tk), 1)
    return jnp.where(r >= c, -1.0, 0.0).astype(BF16)


def _stick_tile(qrows, kb, vb, nsuf, carry, acc, mask, feature_major=False):
    slabs = qrows if isinstance(qrows, (list, tuple)) else None
    if slabs is not None:
        z = jnp.concatenate([_dot_nt(qs, kb[:, s * LANES:(s + 1) * LANES]) for s, qs in enumerate(slabs)], axis=0)
    else:
        z = _dot(qrows, kb) if feature_major else _dot_nt(qrows, kb)
    sp = _softplus(z)
    if mask is not None:
        sp = jnp.where(mask, sp, 0.0)
    incl = _dot_split2(sp, nsuf)
    w = jnp.exp(z + incl + carry)
    if mask is not None:
        w = jnp.where(mask, w, 0.0)
    wb = w.astype(BF16)
    if slabs is not None:
        rows = slabs[0].shape[0]
        pv = jnp.concatenate([_dot(wb[s * rows:(s + 1) * rows], vb[:, s * LANES:(s + 1) * LANES])
                              for s in range(len(slabs))], axis=0)
    else:
        pv = _dot_nt(wb, vb) if feature_major else _dot(wb, vb)
    acc = acc + pv
    carry = carry + incl[:, 0:1]
    return carry, acc


def _softmax_tile(logits, vb, m, l, acc, feature_major=False):
    m_new = jnp.maximum(m, jnp.max(logits, axis=1, keepdims=True))
    alpha = jnp.exp(m - m_new)
    p = jnp.exp(logits - m_new)
    l = alpha * l + jnp.sum(p, axis=1, keepdims=True)
    pb = p.astype(BF16)
    acc = alpha * acc + (_dot_nt(pb, vb) if feature_major else _dot(pb, vb))
    return m_new, l, acc


def _stack_heads(q, n_heads):
    t, width = q.shape
    rep = jnp.concatenate([q] * n_heads, axis=0) if n_heads > 1 else q
    row = lax.broadcasted_iota(jnp.int32, (n_heads * t, width), 0)
    lane = lax.broadcasted_iota(jnp.int32, (n_heads * t, width), 1)
    lo = (row // t) * HEAD_DIM
    keep = (lane >= lo) & (lane < lo + HEAD_DIM)
    return jnp.where(keep, rep, jnp.zeros_like(rep))


def _unstack_heads(acc, n_heads):
    rows, width = acc.shape
    t = rows // n_heads
    lane = lax.broadcasted_iota(jnp.int32, (t, width), 1)
    out = jnp.zeros((t, width), acc.dtype)
    for h in range(n_heads):
        sel = (lane >= h * HEAD_DIM) & (lane < (h + 1) * HEAD_DIM)
        out = jnp.where(sel, acc[h * t:(h + 1) * t, :], out)
    return out


def _row_query_index(rows, cols, t):
    r = lax.broadcasted_iota(jnp.int32, (rows, cols), 0)
    c = lax.broadcasted_iota(jnp.int32, (rows, cols), 1)
    return r % t, c


def _stick_prompt_kernel(q_ref, k_ref, v_ref, o_ref, carry_ref, acc_ref, *, t):
    i = pl.program_id(1)
    nh = HEADS_PER_SLAB
    n_slabs = q_ref.shape[1] // LANES
    rows = n_slabs * nh * t
    q = q_ref[...]
    qrows = [_stack_heads(q[:, s * LANES:(s + 1) * LANES], nh) for s in range(n_slabs)]
    nsuf = _neg_suffix_matrix(t)
    tq_idx, c_idx = _row_query_index(rows, t, t)

    def block(j):
        s = pl.multiple_of(j * t, t)
        return k_ref[pl.ds(s, t), :], v_ref[pl.ds(s, t), :]

    def diagonal():
        return _stick_tile(qrows, *block(i), nsuf, jnp.zeros((rows, 1), F32),
                           jnp.zeros((rows, LANES), F32), c_idx < tq_idx)

    @pl.when(i == 0)
    def _():
        carry, acc = diagonal()
        carry_ref[...] = carry
        acc_ref[...] = acc

    @pl.when(i > 0)
    def _():
        carry, acc = diagonal()
        carry, acc = _stick_tile(qrows, *block(i - 1), nsuf, carry, acc, None)
        carry_ref[...] = carry
        acc_ref[...] = acc

    def cond(state):
        j, carry_max = state
        return jnp.logical_and(j >= 0, carry_max > EXP_ZERO_BELOW)

    def body(state):
        j, _ = state
        carry, acc = _stick_tile(qrows, *block(j), nsuf, carry_ref[...], acc_ref[...], None)
        carry_ref[...] = carry
        acc_ref[...] = acc
        return j - 1, jnp.max(carry)

    lax.while_loop(cond, body, (i - 2, jnp.max(carry_ref[...])))
    acc = acc_ref[...]
    outs = [_unstack_heads(acc[s * nh * t:(s + 1) * nh * t], nh) for s in range(n_slabs)]
    o_ref[...] = (outs[0] if n_slabs == 1 else jnp.concatenate(outs, axis=1)).astype(o_ref.dtype)


def _stick_prompt(q, k, v, n, t=256, slabs_per_step=2):
    width = q.shape[1]
    t = min(t, n)
    group = slabs_per_step * LANES
    assert width % group == 0
    rows = slabs_per_step * HEADS_PER_SLAB * t
    return pl.pallas_call(
        functools.partial(_stick_prompt_kernel, t=t),
        grid=(width // group, n // t),
        in_specs=[pl.BlockSpec((t, group), lambda p, i: (i, p)),
                  pl.BlockSpec((n, group), lambda p, i: (0, p)),
                  pl.BlockSpec((n, group), lambda p, i: (0, p))],
        out_specs=pl.BlockSpec((t, group), lambda p, i: (i, p)),
        out_shape=jax.ShapeDtypeStruct((n, width), BF16),
        scratch_shapes=[pltpu.VMEM((rows, 1), F32), pltpu.VMEM((rows, LANES), F32)],
        compiler_params=_cparams("parallel", "parallel"),
        name="stick_prompt",
    )(q, k, v)


def _stick_sample_kernel(q_ref, kn_ref, vn_ref, ck_ref, cv_ref, nsuf_ref, o_ref, qrows_ref, carry_ref, acc_ref,
                         *, nh):
    jj = pl.program_id(1)
    t = q_ref.shape[1]

    @pl.when(jj == 0)
    def _():
        qrows = _stack_heads(q_ref[0], nh)
        qrows_ref[...] = qrows
        tq_idx, c_idx = _row_query_index(nh * t, t, t)
        carry, acc = _stick_tile(qrows, kn_ref[0], vn_ref[0], _neg_suffix_matrix(t),
                                 jnp.zeros((nh * t, 1), F32), jnp.zeros(acc_ref.shape, F32),
                                 c_idx < tq_idx)
        carry_ref[...] = carry
        acc_ref[...] = acc

    carry, acc = _stick_tile(qrows_ref[...], ck_ref[...].astype(BF16), cv_ref[...].astype(BF16),
                             nsuf_ref[...], carry_ref[...], acc_ref[...], None, feature_major=True)
    carry_ref[...] = carry
    acc_ref[...] = acc

    @pl.when(jj == pl.num_programs(1) - 1)
    def _():
        o_ref[0] = _unstack_heads(acc_ref[...], nh).astype(o_ref.dtype)


def _feature_major(cache):
    layers, b, past, heads, hd = cache.shape
    return jnp.transpose(cache, (0, 1, 3, 4, 2)).reshape(layers, b, heads * hd, past)


def _stick_sample(q, kn, vn, cache_kt, cache_vt, layer, tk=1024):
    b, t, width = q.shape
    past = cache_kt.shape[3]
    tk = min(tk, past)
    nblk = past // tk
    nh = width // HEAD_DIM
    r = jnp.arange(tk)
    nsuf = jnp.where(r[:, None] >= r[None, :], -1.0, 0.0).astype(BF16)
    new_spec = pl.BlockSpec((1, t, width), lambda bi, jj: (bi, 0, 0))
    cache_spec = pl.BlockSpec((None, None, width, tk), lambda bi, jj: (layer, bi, 0, nblk - 1 - jj))
    return pl.pallas_call(
        functools.partial(_stick_sample_kernel, nh=nh),
        grid=(b, nblk),
        in_specs=[new_spec, new_spec, new_spec, cache_spec, cache_spec,
                  pl.BlockSpec((tk, tk), lambda bi, jj: (0, 0))],
        out_specs=pl.BlockSpec((1, t, width), lambda bi, jj: (bi, 0, 0)),
        out_shape=jax.ShapeDtypeStruct((b, t, width), BF16),
        scratch_shapes=[pltpu.VMEM((nh * t, width), BF16),
                        pltpu.VMEM((nh * t, 1), F32),
                        pltpu.VMEM((nh * t, width), F32)],
        compiler_params=_cparams("parallel", "arbitrary"),
        name="stick_sample",
    )(q, kn, vn, cache_kt, cache_vt, nsuf)


def _band_prompt_kernel(q_ref, k0_ref, k1_ref, k2_ref, v0_ref, v1_ref, v2_ref, bias_ref, o_ref,
                        *, t, nprev, fixed_shift):
    i = pl.program_id(1)
    nh = HEADS_PER_SLAB
    n_slabs = q_ref.shape[1] // LANES
    q = q_ref[...]
    krefs = (k0_ref, k1_ref, k2_ref)[3 - (nprev + 1):]
    vrefs = (v0_ref, v1_ref, v2_ref)[3 - (nprev + 1):]
    kcat = jnp.concatenate([r[...] for r in krefs], axis=0)
    vcat = jnp.concatenate([r[...] for r in vrefs], axis=0)
    outs = []
    for s in range(n_slabs):
        lanes = slice(s * LANES, (s + 1) * LANES)
        qrows = _stack_heads(q[:, lanes], nh)
        logits = _dot_nt(qrows, kcat[:, lanes]) + bias_ref[s]
        c_idx = lax.broadcasted_iota(jnp.int32, logits.shape, 1)
        logits = jnp.where(c_idx >= (nprev - i) * t, logits, MASKED)
        vs = vcat[:, lanes]
        if fixed_shift:
            v_ones = jnp.concatenate([vs, jnp.ones(vs.shape, BF16)], axis=1)
            acc = _dot(jnp.exp(logits).astype(BF16), v_ones)
            acc = acc[:, :LANES] / acc[:, LANES:]
        else:
            m = jnp.max(logits, axis=1, keepdims=True)
            p = jnp.exp(logits - m)
            l = jnp.sum(p, axis=1, keepdims=True)
            acc = _dot(p.astype(BF16), vs) / l
        outs.append(_unstack_heads(acc, nh))
    o_ref[...] = (outs[0] if n_slabs == 1 else jnp.concatenate(outs, axis=1)).astype(o_ref.dtype)


def _rel_bias(rel_table, nq, nk, q0, k0):
    m_len = nq + nk - 1
    u = np.arange(m_len)
    d = np.where(u < nk, (q0 - k0) - u, (q0 - k0) + (m_len - u))
    g = jnp.take(rel_table.astype(F32), np.clip(d, -REL_CLIP, REL_CLIP) + REL_CLIP, axis=1)
    n_heads = g.shape[0]
    bias = jnp.tile(g, (1, nq))[:, :nq * (m_len - 1)].reshape(n_heads, nq, m_len - 1)[:, :, :nk]
    qp = q0 + np.arange(nq)[:, None]
    kp = k0 + np.arange(nk)[None, :]
    qc, kc = qp // CHUNK, kp // CHUNK
    visible = (kp >= 0) & (kc <= qc) & (kc >= qc - BAND_CHUNKS)
    return bias, jnp.asarray(visible)[None]


def _band_prompt(q, k, v, rel_table, n, logit_bound, t=256):
    width = q.shape[1]
    t = min(t, n)
    assert t % CHUNK == 0
    nprev = min(-(-BAND // t), 2)
    assert nprev * t >= BAND
    slabs = width // LANES
    bias, visible = _rel_bias(rel_table, t, (nprev + 1) * t, nprev * t, 0)
    bias_shape = (slabs, HEADS_PER_SLAB * t, (nprev + 1) * t)
    per_step = 2 if slabs % 2 == 0 else 1
    group = per_step * LANES

    def kv_spec(back):
        return pl.BlockSpec((t, group), lambda p, i: (jnp.maximum(i - back, 0), p))

    def call(masked_bias, fixed_shift):
        return pl.pallas_call(
            functools.partial(_band_prompt_kernel, t=t, nprev=nprev, fixed_shift=fixed_shift),
            grid=(slabs // per_step, n // t),
            in_specs=[pl.BlockSpec((t, group), lambda p, i: (i, p)),
                      kv_spec(2), kv_spec(1), kv_spec(0), kv_spec(2), kv_spec(1), kv_spec(0),
                      pl.BlockSpec((per_step,) + bias_shape[1:], lambda p, i: (p, 0, 0))],
            out_specs=pl.BlockSpec((t, group), lambda p, i: (i, p)),
            out_shape=jax.ShapeDtypeStruct((n, width), BF16),
            compiler_params=_cparams("parallel", "parallel"),
            name="band_prompt" if fixed_shift else "band_prompt_row_max",
        )(q, k, k, k, v, v, v, masked_bias.reshape(bias_shape))

    tab_max, tab_min = jnp.max(rel_table), jnp.min(rel_table)
    shift = logit_bound + tab_max
    spread = 2.0 * logit_bound + (tab_max - tab_min)
    return lax.cond(spread < 2.0 * MAX_FIXED_SHIFT,
                    lambda: call(jnp.where(visible, bias - shift, MASKED), True),
                    lambda: call(jnp.where(visible, bias, MASKED), False))


def _band_sample_kernel(q_ref, kn_ref, vn_ref, ck_ref, cv_ref, bc_ref, bn_ref, o_ref, *, nh):
    qrows = _stack_heads(q_ref[0], nh)
    lc = _dot(qrows, ck_ref[...].astype(BF16)) + bc_ref[...]
    ln = _dot_nt(qrows, kn_ref[0]) + bn_ref[...]
    m = jnp.maximum(jnp.max(lc, axis=1, keepdims=True), jnp.max(ln, axis=1, keepdims=True))
    pc = jnp.exp(lc - m)
    pn = jnp.exp(ln - m)
    l = jnp.sum(pc, axis=1, keepdims=True) + jnp.sum(pn, axis=1, keepdims=True)
    acc = (_dot_nt(pc.astype(BF16), cv_ref[...].astype(BF16)) + _dot(pn.astype(BF16), vn_ref[0])) / l
    o_ref[0] = _unstack_heads(acc, nh).astype(o_ref.dtype)


def _band_sample(q, kn, vn, cache_kt, cache_vt, layer, rel_table, past_len):
    b, t, width = q.shape
    buf_len = cache_kt.shape[3]
    nh = width // HEAD_DIM
    def masked_bias(nk, k0):
        bias, visible = _rel_bias(rel_table, t, nk, past_len, k0)
        return jnp.where(visible, bias, MASKED).reshape(nh * t, nk)

    bias_c = masked_bias(buf_len, past_len - buf_len)
    bias_n = masked_bias(t, past_len)
    new_spec = pl.BlockSpec((1, t, width), lambda bi: (bi, 0, 0))
    cache_spec = pl.BlockSpec((None, None, width, buf_len), lambda bi: (layer, bi, 0, 0))
    return pl.pallas_call(
        functools.partial(_band_sample_kernel, nh=nh),
        grid=(b,),
        in_specs=[new_spec, new_spec, new_spec, cache_spec, cache_spec,
                  _full_spec(bias_c.shape), _full_spec(bias_n.shape)],
        out_specs=pl.BlockSpec((1, t, width), lambda bi: (bi, 0, 0)),
        out_shape=jax.ShapeDtypeStruct((b, t, width), BF16),
        compiler_params=_cparams("parallel"),
        name="band_sample",
    )(q, kn, vn, cache_kt, cache_vt, bias_c, bias_n)


def _key_decay_rows(dk, t):
    n_heads, tk = dk.shape
    return jnp.concatenate([jnp.broadcast_to(dk[h:h + 1, :], (t, tk)) for h in range(n_heads)], axis=0)


def _fox_prompt_kernel(q_ref, dq_ref, k_ref, v_ref, dk_ref, o_ref, m_ref, l_ref, acc_ref, *, t):
    i = pl.program_id(1)
    nh = HEADS_PER_SLAB
    qrows = _stack_heads(q_ref[...], nh)
    dq = jnp.concatenate([dq_ref[:, h:h + 1] for h in range(nh)], axis=0)
    m_ref[...] = jnp.full(m_ref.shape, MASKED, F32)
    l_ref[...] = jnp.zeros(l_ref.shape, F32)
    acc_ref[...] = jnp.zeros(acc_ref.shape, F32)

    def tile(j, mask):
        s = pl.multiple_of(j * t, t)
        z = _dot_nt(qrows, k_ref[pl.ds(s, t), :])
        logits = (z + dq) - _key_decay_rows(dk_ref[:, pl.ds(s, t)], t)
        if mask is not None:
            logits = jnp.where(mask, logits, MASKED)
        m, l, acc = _softmax_tile(logits, v_ref[pl.ds(s, t), :], m_ref[...], l_ref[...], acc_ref[...])
        m_ref[...] = m
        l_ref[...] = l
        acc_ref[...] = acc

    def body(j, carry):
        tile(j, None)
        return carry

    lax.fori_loop(0, i, body, 0)
    tq_idx, c_idx = _row_query_index(nh * t, t, t)
    tile(i, c_idx <= tq_idx)
    o_ref[...] = _unstack_heads(acc_ref[...] / l_ref[...], nh).astype(o_ref.dtype)


def _fox_prompt_bounded_kernel(shift_ref, first_ref, q_ref, dq_ref, k_ref, v_ref, dk_ref, o_ref,
                               dqrep_ref, acc_ref, z_ref, *, tq, tk):
    slab = pl.program_id(0)
    i = pl.program_id(1)
    ratio = tq // tk
    q = q_ref[...]
    ones = jnp.ones((tk, LANES), BF16)
    row = lax.broadcasted_iota(jnp.int32, (tq, LANES), 0)
    lane = lax.broadcasted_iota(jnp.int32, (tq, LANES), 1)
    outs = []
    for hh in range(HEADS_PER_SLAB):
        qm = jnp.where((lane >= hh * HEAD_DIM) & (lane < (hh + 1) * HEAD_DIM), q, jnp.zeros_like(q))
        dqrep_ref[...] = jnp.broadcast_to(dq_ref[:, hh:hh + 1] - shift_ref[0, 0], (tq, LANES))
        acc_ref[...] = jnp.zeros(acc_ref.shape, F32)

        def scores(j, r0, qm=qm):
            s = pl.multiple_of(j * tk, tk)
            return _dot_nt(qm[r0:], k_ref[pl.ds(s, tk), :])

        def tile(j, r0, causal, next_r0, hh=hh, scores=scores):
            s = pl.multiple_of(j * tk, tk)
            z = z_ref[r0:, :]
            dqh = dqrep_ref[r0:, :]
            cols = []
            for c in range(tk // LANES):
                lg = (z[:, c * LANES:(c + 1) * LANES] + dqh) - dk_ref[hh:hh + 1, pl.ds(s + c * LANES, LANES)]
                if causal:
                    key_pos = s + c * LANES + lax.broadcasted_iota(jnp.int32, lg.shape, 1)
                    query_pos = i * tq + r0 + lax.broadcasted_iota(jnp.int32, lg.shape, 0)
                    lg = jnp.where(key_pos <= query_pos, lg, MASKED)
                cols.append(jnp.exp(lg).astype(BF16))
            if next_r0 is not None:
                z_next = scores(j + 1, next_r0)
            v_ones = jnp.concatenate([v_ref[pl.ds(s, tk), :], ones], axis=1)
            acc_ref[r0:, :] += _dot(jnp.concatenate(cols, axis=1), v_ones)
            if next_r0 is not None:
                z_ref[next_r0:, :] = z_next

        start = first_ref[slab * HEADS_PER_SLAB + hh, i]
        z_ref[...] = scores(start, 0)

        def body(j, carry, tile=tile):
            tile(j, 0, False, 0)
            return carry

        lax.fori_loop(start, i * ratio, body, 0)
        for d in range(ratio):
            tile(i * ratio + d, d * tk, True, (d + 1) * tk if d + 1 < ratio else None)
        acc = acc_ref[...]
        outs.append(acc[:, :LANES] / acc[:, LANES:])
    o_ref[...] = jnp.where(lane < HEAD_DIM, outs[0], outs[1]).astype(o_ref.dtype)


MAX_FIXED_SHIFT = 30.0


def _fox_prompt(q, k, v, cum, n, logit_bound, t_exact=256, tq_bounded=1024, tk_bounded=512):
    width = q.shape[1]
    slabs = width // LANES
    nh = HEADS_PER_SLAB
    dq = cum.reshape(n, slabs, nh).transpose(1, 0, 2)
    dk = cum.reshape(n, slabs, nh).transpose(1, 2, 0)

    def specs(t):
        return ([pl.BlockSpec((t, LANES), lambda p, i: (i, p)),
                 pl.BlockSpec((None, t, nh), lambda p, i: (p, i, 0)),
                 pl.BlockSpec((n, LANES), lambda p, i: (0, p)),
                 pl.BlockSpec((n, LANES), lambda p, i: (0, p)),
                 pl.BlockSpec((None, nh, n), lambda p, i: (p, 0, 0))],
                pl.BlockSpec((t, LANES), lambda p, i: (i, p)))

    def exact(_):
        t = min(t_exact, n)
        in_specs, out_spec = specs(t)
        return pl.pallas_call(
            functools.partial(_fox_prompt_kernel, t=t),
            grid=(slabs, n // t),
            in_specs=in_specs, out_specs=out_spec,
            out_shape=jax.ShapeDtypeStruct((n, width), BF16),
            scratch_shapes=[pltpu.VMEM((nh * t, 1), F32), pltpu.VMEM((nh * t, 1), F32),
                            pltpu.VMEM((nh * t, LANES), F32)],
            compiler_params=_cparams("parallel", "parallel"),
            name="fox_prompt_running_max",
        )(q, dq, k, v, dk)

    def bounded(shift):
        tq = min(tq_bounded, n)
        tk = min(tk_bounded, n)
        in_specs, out_spec = specs(tq)
        cum_t = cum.T
        decay = cum_t[:, ::tq][:, :, None] - cum_t[:, tk - 1::tk][:, None, :]
        before = jnp.arange(n // tk)[None, None, :] < (jnp.arange(n // tq) * (tq // tk))[None, :, None]
        dead = jnp.logical_and(decay < EXP_ZERO_BELOW, before)
        first = jnp.sum(jnp.cumprod(dead.astype(jnp.int32), axis=2), axis=2).astype(jnp.int32)
        smem = pl.BlockSpec(memory_space=pltpu.SMEM)
        return pl.pallas_call(
            functools.partial(_fox_prompt_bounded_kernel, tq=tq, tk=tk),
            grid=(slabs, n // tq),
            in_specs=[smem, smem] + in_specs, out_specs=out_spec,
            out_shape=jax.ShapeDtypeStruct((n, width), BF16),
            scratch_shapes=[pltpu.VMEM((tq, LANES), F32), pltpu.VMEM((tq, 2 * LANES), F32),
                            pltpu.VMEM((tq, tk), F32)],
            compiler_params=_cparams("parallel", "parallel"),
            name="fox_prompt",
        )(shift.reshape(1, 1), first, q, dq, k, v, dk)

    return lax.cond(logit_bound < MAX_FIXED_SHIFT, bounded, exact, logit_bound)


def _fox_sample_kernel(q_ref, dq_ref, kn_ref, vn_ref, dkn_ref, ck_ref, cv_ref, dkc_ref, o_ref,
                       qrows_ref, m_ref, l_ref, acc_ref, *, nh):
    jj = pl.program_id(1)
    t = q_ref.shape[1]
    dq = dq_ref[0]

    @pl.when(jj == 0)
    def _():
        qrows = _stack_heads(q_ref[0], nh)
        qrows_ref[...] = qrows
        tq_idx, c_idx = _row_query_index(nh * t, t, t)
        logits = (_dot_nt(qrows, kn_ref[0]) + dq) - _key_decay_rows(dkn_ref[0], t)
        logits = jnp.where(c_idx <= tq_idx, logits, MASKED)
        m, l, acc = _softmax_tile(logits, vn_ref[0], jnp.full(m_ref.shape, MASKED, F32),
                                  jnp.zeros(l_ref.shape, F32), jnp.zeros(acc_ref.shape, F32))
        m_ref[...] = m
        l_ref[...] = l
        acc_ref[...] = acc

    logits = (_dot(qrows_ref[...], ck_ref[...].astype(BF16)) + dq) - _key_decay_rows(dkc_ref[0], t)
    m, l, acc = _softmax_tile(logits, cv_ref[...].astype(BF16), m_ref[...], l_ref[...], acc_ref[...],
                              feature_major=True)
    m_ref[...] = m
    l_ref[...] = l
    acc_ref[...] = acc

    @pl.when(jj == pl.num_programs(1) - 1)
    def _():
        o_ref[0] = _unstack_heads(acc_ref[...] / l_ref[...], nh).astype(o_ref.dtype)


def _fox_sample(q, kn, vn, cache_kt, cache_vt, layer, cum_cache, cum_new, tk=1024):
    b, t, width = q.shape
    past = cache_kt.shape[3]
    tk = min(tk, past)
    nblk = past // tk
    nh = width // HEAD_DIM
    dq = cum_new.reshape(b, nh * t, 1)
    new_spec = pl.BlockSpec((1, t, width), lambda bi, jj: (bi, 0, 0))
    cache_spec = pl.BlockSpec((None, None, width, tk), lambda bi, jj: (layer, bi, 0, jj))
    return pl.pallas_call(
        functools.partial(_fox_sample_kernel, nh=nh),
        grid=(b, nblk),
        in_specs=[new_spec, pl.BlockSpec((1, nh * t, 1), lambda bi, jj: (bi, 0, 0)),
                  new_spec, new_spec, pl.BlockSpec((1, nh, t), lambda bi, jj: (bi, 0, 0)),
                  cache_spec, cache_spec, pl.BlockSpec((1, nh, tk), lambda bi, jj: (bi, 0, jj))],
        out_specs=pl.BlockSpec((1, t, width), lambda bi, jj: (bi, 0, 0)),
        out_shape=jax.ShapeDtypeStruct((b, t, width), BF16),
        scratch_shapes=[pltpu.VMEM((nh * t, width), BF16), pltpu.VMEM((nh * t, 1), F32),
                        pltpu.VMEM((nh * t, 1), F32), pltpu.VMEM((nh * t, width), F32)],
        compiler_params=_cparams("parallel", "arbitrary"),
        name="fox_sample",
    )(q, dq, kn, vn, cum_new, cache_kt, cache_vt, cum_cache)


def _pad_to(x, multiple, axis):
    size = x.shape[axis]
    target = -(-size // multiple) * multiple
    if target == size:
        return x
    pad = [(0, 0)] * x.ndim
    pad[axis] = (0, target - size)
    return jnp.pad(x, pad)


def kernel(x_prompt, x_sample, cache_a_k, cache_a_v, cache_b_k, cache_b_v, cache_c_k, cache_c_v, cache_c_logf, p_prompt, p_sample, g_mix, w_in_even, g_qb, g_kb, rel_bias, w_in_odd, b_forget, g_qc, g_kc, w_out, g_mlp, w_ff1, w_ff2, g_ple, w_ple_gate, w_ple_proj):
    bp, n, d = x_prompt.shape
    bs, ts, _ = x_sample.shape
    assert bp == 1, "prompt kernels assume a single prompt stream"
    depth = g_mix.shape[0]
    past = cache_a_k.shape[2]
    buf_len = cache_b_k.shape[2]
    ha, hb, hc = cache_a_k.shape[3], cache_b_k.shape[3], cache_c_k.shape[3]
    wa, wb, wc = ha * HEAD_DIM, hb * HEAD_DIM, hc * HEAD_DIM
    rows_s = bs * ts

    blk = jnp.arange(NORM_SLAB) // HEAD_DIM
    gmat = jnp.where(blk[:, None] == blk[None, :], 1.0 / HEAD_DIM, 0.0).astype(BF16)

    cache_a_kt, cache_a_vt = _feature_major(cache_a_k), _feature_major(cache_a_v)
    cache_b_kt, cache_b_vt = _feature_major(cache_b_k), _feature_major(cache_b_v)
    cache_c_kt, cache_c_vt = _feature_major(cache_c_k), _feature_major(cache_c_v)

    rows = n + rows_s
    tail = -rows % ROW_TILE
    h = jnp.concatenate([x_prompt.reshape(n, d), x_sample.reshape(rows_s, d), jnp.zeros((tail, d), F32)], axis=0)
    sample_pad = rows + tail - n
    d_ple = p_prompt.shape[-1]
    p_prompt2 = p_prompt.reshape(depth, n, d_ple)
    p_sample2 = _pad_to(p_sample.reshape(depth, rows_s, d_ple), sample_pad, 1)
    w_even16 = w_in_even.astype(BF16)
    w_odd16 = _pad_to(w_in_odd, LANES, 2).astype(BF16)
    w_out16, w_ff1_16, w_ff2_16 = w_out.astype(BF16), w_ff1.astype(BF16), w_ff2.astype(BF16)
    w_gate16, w_proj16 = w_ple_gate.astype(BF16), w_ple_proj.astype(BF16)

    def prompt_state(per_layer, heads, lo=0):
        xt = jnp.concatenate([x[:, :, lo:] for x in per_layer], axis=0)
        return jnp.transpose(xt.reshape(len(per_layer), heads, HEAD_DIM, n - lo), (0, 3, 1, 2))[:, None]

    def sample_state(xt, heads):
        return jnp.transpose(xt[:, :rows_s].reshape(heads, HEAD_DIM, bs, ts), (2, 3, 0, 1))

    st = {name: [] for name in ("pc_f", "sa_k", "sa_v", "sb_k", "sb_v", "sc_k", "sc_v", "sc_f")}
    even_states, odd_states = [], []
    for i in range(depth):
        j = i // 2
        g_row = g_mix[i].reshape(1, d)
        if i % 2 == 0:
            gq_row = jnp.tile(g_qb[j], hb).reshape(1, wb)
            gk_row = jnp.tile(g_kb[j], hb).reshape(1, wb)
            outs = _proj_even(h, g_row, w_even16, j, gq_row, gk_row, gmat, wa, wb, n)
            qa, ka16, va16, qb, kb16, vb16 = outs[:6]
            even_states.append(outs[6:10])
            ka_s, va_s, kb_s, vb_s = outs[10:]
            mix_a_p = _stick_prompt(qa, ka16, va16, n)
            band_bound = 1.01 * HEAD_DIM * SCALE * jnp.max(jnp.abs(g_qb[j])) * jnp.max(jnp.abs(g_kb[j]))
            mix_b_p = _band_prompt(qb, kb16, vb16, rel_bias[j], n, band_bound.astype(F32))
            s3 = lambda a: a[n:rows].reshape(bs, ts, a.shape[1])
            mix_a_s = _stick_sample(s3(qa), s3(ka16), s3(va16), cache_a_kt, cache_a_vt, j)
            mix_b_s = _band_sample(s3(qb), s3(kb16), s3(vb16), cache_b_kt, cache_b_vt, j, rel_bias[j], past)
            mix_prompt = [mix_a_p, mix_b_p]
            mix_sample = _pad_to(jnp.concatenate([mix_a_s.reshape(rows_s, wa), mix_b_s.reshape(rows_s, wb)],
                                                 axis=1), sample_pad, 0)
            st["sa_k"].append(sample_state(ka_s, ha))
            st["sa_v"].append(sample_state(va_s, ha))
            st["sb_k"].append(jnp.concatenate([cache_b_k[j], sample_state(kb_s, hb)], axis=1)[:, ts:])
            st["sb_v"].append(jnp.concatenate([cache_b_v[j], sample_state(vb_s, hb)], axis=1)[:, ts:])
        else:
            gq_row = jnp.tile(g_qc[j], hc).reshape(1, wc)
            gk_row = jnp.tile(g_kc[j], hc).reshape(1, wc)
            bf_row = _pad_to(b_forget[j].reshape(1, hc), LANES, 1)
            outs = _proj_odd(h, g_row, w_odd16, j, gq_row, gk_row, gmat, bf_row, wc, n)
            q, k16, v16, lf = outs[:4]
            odd_states.append(outs[4:6])
            k_s, v_s = outs[6:]
            log_f = lf[:, :hc]
            cum_p = _cumsum_lanes(log_f[:n].T)
            logit_bound = 1.01 * HEAD_DIM * SCALE * jnp.max(jnp.abs(g_qc[j])) * jnp.max(jnp.abs(g_kc[j]))
            mix_p = _fox_prompt(q, k16, v16, cum_p.T, n, logit_bound.astype(F32))
            lf_new = log_f[n:rows].reshape(bs, ts, hc).transpose(0, 2, 1)
            lf_all = jnp.concatenate([cache_c_logf[j].transpose(0, 2, 1), lf_new], axis=2)
            cum_s = _cumsum_lanes(lf_all.reshape(bs * hc, past + ts)).reshape(bs, hc, past + ts)
            s3 = lambda a: a[n:rows].reshape(bs, ts, a.shape[1])
            mix_s = _fox_sample(s3(q), s3(k16), s3(v16), cache_c_kt, cache_c_vt, j,
                                cum_s[:, :, :past], cum_s[:, :, past:past + ts])
            mix_prompt = [mix_p]
            mix_sample = _pad_to(mix_s.reshape(rows_s, wc), sample_pad, 0)
            st["pc_f"].append(log_f[:n].reshape(1, n, hc))
            st["sc_k"].append(sample_state(k_s, hc))
            st["sc_v"].append(sample_state(v_s, hc))
            st["sc_f"].append(log_f[n:rows].reshape(bs, ts, hc))
        h = _post(h, mix_prompt, mix_sample, n, i, w_out16, g_mlp[i].reshape(1, d), w_ff1_16, w_ff2_16,
                  g_ple[i].reshape(1, d), w_gate16, p_prompt2, p_sample2, w_proj16, split_output=i == depth - 1)

    y_prompt = h[0].reshape(1, n, d)
    y_sample = h[1][:rows_s].reshape(bs, ts, d)
    stk = {name: jnp.stack(vals) for name, vals in st.items()}
    keep = min(BAND, n)
    ka_l, va_l, kb_l, vb_l = zip(*even_states)
    kc_l, vc_l = zip(*odd_states)
    stk["pa_k"], stk["pa_v"] = prompt_state(ka_l, ha), prompt_state(va_l, ha)
    stk["pb_k"], stk["pb_v"] = prompt_state(kb_l, hb, n - keep), prompt_state(vb_l, hb, n - keep)
    stk["pc_k"], stk["pc_v"] = prompt_state(kc_l, hc), prompt_state(vc_l, hc)
    return (y_prompt, y_sample, stk["pa_k"], stk["pa_v"], stk["pb_k"], stk["pb_v"],
            stk["pc_k"], stk["pc_v"], stk["pc_f"], stk["sa_k"], stk["sa_v"], stk["sb_k"], stk["sb_v"],
            stk["sc_k"], stk["sc_v"], stk["sc_f"])
```

```python
import functools

import numpy as np
import jax
import jax.numpy as jnp
from jax import lax
from jax.experimental import pallas as pl
from jax.experimental.pallas import tpu as pltpu

F32 = jnp.float32
BF16 = jnp.bfloat16

HEAD_DIM = 64
CHUNK = 64
BAND_CHUNKS = 8
BAND = BAND_CHUNKS * CHUNK
REL_CLIP = 128
EPS = 1e-6
SCALE = HEAD_DIM ** -0.5
LANES = 128
HEADS_PER_SLAB = LANES // HEAD_DIM
NORM_SLAB = 256
EXP_ZERO_BELOW = -104.0
MASKED = -1e30
VMEM_LIMIT = 56 * 1024 * 1024


def _cparams(*sem):
    return pltpu.CompilerParams(dimension_semantics=sem, vmem_limit_bytes=VMEM_LIMIT)


def _split2(x):
    hi = x.astype(BF16)
    lo = (x - hi.astype(F32)).astype(BF16)
    return hi, lo


def _split3(x):
    hi = x.astype(BF16)
    r = x - hi.astype(F32)
    mid = r.astype(BF16)
    lo = (r - mid.astype(F32)).astype(BF16)
    return hi, mid, lo


def _dot(a, b):
    return jnp.dot(a, b, preferred_element_type=F32)


def _dot_nt(a, b):
    return lax.dot_general(a, b, (((1,), (1,)), ((), ())), preferred_element_type=F32)


def _dot_split2(x, m):
    hi, lo = _split2(x)
    return _dot(hi, m) + _dot(lo, m)


def _dot_split3(x, m):
    hi, mid, lo = _split3(x)
    return (_dot(hi, m) + _dot(mid, m)) + _dot(lo, m)


def _rms_rows(x, g):
    ms = jnp.mean(x * x, axis=-1, keepdims=True)
    return x * lax.rsqrt(ms + EPS) * g


def _head_rms(x, gmat, gain_row):
    width = x.shape[1]
    outs = []
    for s in range(width // NORM_SLAB):
        xs = x[:, NORM_SLAB * s:NORM_SLAB * (s + 1)]
        ms = _dot_split2(xs * xs, gmat)
        outs.append(xs * lax.rsqrt(ms + EPS))
    y = outs[0] if len(outs) == 1 else jnp.concatenate(outs, axis=1)
    return y * gain_row


def _softplus(z):
    return jnp.maximum(z, 0.0) + jnp.log(1.0 + jnp.exp(-jnp.abs(z)))


def _log_sigmoid(z):
    return jnp.minimum(z, 0.0) - jnp.log(1.0 + jnp.exp(-jnp.abs(z)))


def _store_state(values, prompt_refs, sample_refs, prompt_tiles):
    step = pl.program_id(0)

    @pl.when(step < prompt_tiles)
    def _():
        for value, ref in zip(values, prompt_refs):
            ref[...] = value.T

    @pl.when(step >= prompt_tiles)
    def _():
        for value, ref in zip(values, sample_refs):
            ref[...] = value.T


def _proj_even_kernel(h_ref, g_ref, w_ref, gq_ref, gk_ref, gmat_ref,
                      qa_ref, ka16_ref, va16_ref, qb_ref, kb16_ref, vb16_ref,
                      ka_p_ref, va_p_ref, kb_p_ref, vb_p_ref, ka_s_ref, va_s_ref, kb_s_ref, vb_s_ref,
                      *, wa, wb, prompt_tiles):
    xn = _rms_rows(h_ref[...], g_ref[...]).astype(BF16)
    proj = _dot(xn, w_ref[...])
    gmat = gmat_ref[...]
    qa = proj[:, 0:wa]
    ka = proj[:, wa:2 * wa]
    va = proj[:, 2 * wa:3 * wa]
    o = 3 * wa
    qb = _head_rms(proj[:, o:o + wb], gmat, gq_ref[...])
    kb = _head_rms(proj[:, o + wb:o + 2 * wb], gmat, gk_ref[...])
    vb = proj[:, o + 2 * wb:o + 3 * wb]
    qa_ref[...] = (qa * SCALE).astype(BF16)
    ka16_ref[...] = ka.astype(BF16)
    va16_ref[...] = va.astype(BF16)
    qb_ref[...] = (qb * SCALE).astype(BF16)
    kb16_ref[...] = kb.astype(BF16)
    vb16_ref[...] = vb.astype(BF16)
    _store_state([ka, va, kb, vb], [ka_p_ref, va_p_ref, kb_p_ref, vb_p_ref],
                 [ka_s_ref, va_s_ref, kb_s_ref, vb_s_ref], prompt_tiles)


def _proj_odd_kernel(h_ref, g_ref, w_ref, gq_ref, gk_ref, gmat_ref, bf_ref,
                     q_ref, k16_ref, v16_ref, lf_ref, k_p_ref, v_p_ref, k_s_ref, v_s_ref, *, wc, prompt_tiles):
    xn = _rms_rows(h_ref[...], g_ref[...]).astype(BF16)
    proj = _dot(xn, w_ref[...])
    gmat = gmat_ref[...]
    q = _head_rms(proj[:, 0:wc], gmat, gq_ref[...])
    k = _head_rms(proj[:, wc:2 * wc], gmat, gk_ref[...])
    v = proj[:, 2 * wc:3 * wc]
    f = proj[:, 3 * wc:3 * wc + LANES]
    q_ref[...] = (q * SCALE).astype(BF16)
    k16_ref[...] = k.astype(BF16)
    v16_ref[...] = v.astype(BF16)
    _store_state([k, v], [k_p_ref, v_p_ref], [k_s_ref, v_s_ref], prompt_tiles)
    lf_ref[...] = _log_sigmoid(f + bf_ref[...])


def _post1_kernel(*refs, n_mix, ff_chunk, prompt_tiles):
    h_ref, mix_p_refs, mix_s_ref = refs[0], refs[1:1 + n_mix], refs[1 + n_mix]
    wo_ref, g_ref, w1_ref, h1_ref, hid_ref = refs[2 + n_mix:]
    pieces = [r[...] for r in mix_p_refs]
    mix_p = pieces[0] if n_mix == 1 else jnp.concatenate(pieces, axis=1)
    mix = jnp.where(pl.program_id(0) < prompt_tiles, mix_p, mix_s_ref[...])
    h1 = h_ref[...] + _dot(mix, wo_ref[...])
    h1_ref[...] = h1
    xn = _rms_rows(h1, g_ref[...]).astype(BF16)
    d_ff = w1_ref.shape[1]
    for c in range(d_ff // ff_chunk):
        sl = slice(c * ff_chunk, (c + 1) * ff_chunk)
        a = jnp.maximum(_dot(xn, w1_ref[:, sl]), 0.0)
        hid_ref[:, sl] = (a * a).astype(BF16)


def _post2_kernel(h1_ref, hid_ref, w2_ref, g_ref, wg_ref, p_p_ref, p_s_ref, wp_ref, *out_refs, prompt_tiles):
    step = pl.program_id(0)
    h2 = h1_ref[...] + _dot(hid_ref[...], w2_ref[...])
    xg = _rms_rows(h2, g_ref[...]).astype(BF16)
    gate = jax.nn.sigmoid(_dot(xg, wg_ref[...]))
    p = jnp.where(step < prompt_tiles, p_p_ref[...], p_s_ref[...]).astype(BF16)
    out = h2 + gate * _dot(p, wp_ref[...])
    if len(out_refs) == 1:
        out_refs[0][...] = out
    else:
        @pl.when(step < prompt_tiles)
        def _():
            out_refs[0][...] = out

        @pl.when(step >= prompt_tiles)
        def _():
            out_refs[1][...] = out


def _row_spec(tm, width):
    return pl.BlockSpec((tm, width), lambda i: (i, 0))


def _full_spec(shape):
    return pl.BlockSpec(shape, lambda i: (0,) * len(shape))


ROW_TILE = 512


def _row_tile(rows):
    assert rows % ROW_TILE == 0, rows
    return ROW_TILE


def _proj_call(body, name, inputs, layer, n, row_widths, row_dtypes, state_widths):
    h = inputs[0]
    rows, d = h.shape
    tm = _row_tile(rows)
    assert n % tm == 0, (n, tm)
    prompt_tiles = n // tm
    prompt_spec = lambda wd: pl.BlockSpec((None, wd, tm), lambda i: (0, 0, jnp.minimum(i, prompt_tiles - 1)))
    sample_spec = lambda wd: pl.BlockSpec((wd, tm), lambda i: (0, jnp.maximum(i - prompt_tiles, 0)))
    return pl.pallas_call(
        functools.partial(body, prompt_tiles=prompt_tiles),
        grid=(rows // tm,),
        in_specs=[_row_spec(tm, d)] + [_layer_spec(x, layer) if x.ndim == 3 else _full_spec(x.shape)
                                       for x in inputs[1:]],
        out_specs=([_row_spec(tm, wd) for wd in row_widths] + [prompt_spec(wd) for wd in state_widths]
                   + [sample_spec(wd) for wd in state_widths]),
        out_shape=([jax.ShapeDtypeStruct((rows, wd), dt) for wd, dt in zip(row_widths, row_dtypes)]
                   + [jax.ShapeDtypeStruct((1, wd, n), F32) for wd in state_widths]
                   + [jax.ShapeDtypeStruct((wd, rows - n), F32) for wd in state_widths]),
        compiler_params=_cparams("arbitrary"),
        name=name,
    )(*inputs)


def _proj_even(h, g, w_stack, layer, gq_row, gk_row, gmat, wa, wb, n):
    return _proj_call(functools.partial(_proj_even_kernel, wa=wa, wb=wb), "proj_even",
                      (h, g, w_stack, gq_row, gk_row, gmat), layer, n,
                      [wa, wa, wa, wb, wb, wb], [BF16] * 6, [wa, wa, wb, wb])


def _proj_odd(h, g, w_stack, layer, gq_row, gk_row, gmat, bf_row, wc, n):
    return _proj_call(functools.partial(_proj_odd_kernel, wc=wc), "proj_odd",
                      (h, g, w_stack, gq_row, gk_row, gmat, bf_row), layer, n,
                      [wc, wc, wc, LANES], [BF16] * 3 + [F32], [wc, wc])


def _layer_spec(stacked, layer):
    return pl.BlockSpec((None,) + stacked.shape[1:], lambda i: (layer, 0, 0))


def _post(h, mix_prompt, mix_sample, n, layer, wo, g_mlp, w1, w2, g_ple, wg, p_prompt, p_sample, wp, split_output):
    rows, d = h.shape
    tm = _row_tile(rows)
    assert n % tm == 0, (n, tm)
    prompt_tiles = n // tm
    d_ff = w1.shape[2]
    d_ple = p_prompt.shape[2]
    ff_chunk = 1024 if d_ff % 1024 == 0 else d_ff
    prompt_rows = lambda wd: pl.BlockSpec((tm, wd), lambda i: (jnp.minimum(i, prompt_tiles - 1), 0))
    sample_rows = lambda wd: pl.BlockSpec((tm, wd), lambda i: (jnp.maximum(i - prompt_tiles, 0), 0))
    h1, hid = pl.pallas_call(
        functools.partial(_post1_kernel, n_mix=len(mix_prompt), ff_chunk=ff_chunk, prompt_tiles=prompt_tiles),
        grid=(rows // tm,),
        in_specs=([_row_spec(tm, d)] + [prompt_rows(m.shape[1]) for m in mix_prompt]
                  + [sample_rows(mix_sample.shape[1]), _layer_spec(wo, layer), _full_spec(g_mlp.shape),
                     _layer_spec(w1, layer)]),
        out_specs=[_row_spec(tm, d), _row_spec(tm, d_ff)],
        out_shape=[jax.ShapeDtypeStruct((rows, d), F32), jax.ShapeDtypeStruct((rows, d_ff), BF16)],
        compiler_params=_cparams("arbitrary"),
        name="post_attn_mlp_up",
    )(h, *mix_prompt, mix_sample, wo, g_mlp, w1)
    if split_output:
        out_specs = [prompt_rows(d), sample_rows(d)]
        out_shape = [jax.ShapeDtypeStruct((n, d), F32), jax.ShapeDtypeStruct((rows - n, d), F32)]
    else:
        out_specs = _row_spec(tm, d)
        out_shape = jax.ShapeDtypeStruct((rows, d), F32)
    return pl.pallas_call(
        functools.partial(_post2_kernel, prompt_tiles=prompt_tiles),
        grid=(rows // tm,),
        in_specs=[_row_spec(tm, d), _row_spec(tm, d_ff), _layer_spec(w2, layer), _full_spec(g_ple.shape),
                  _layer_spec(wg, layer),
                  pl.BlockSpec((None, tm, d_ple), lambda i: (layer, jnp.minimum(i, prompt_tiles - 1), 0)),
                  pl.BlockSpec((None, tm, d_ple), lambda i: (layer, jnp.maximum(i - prompt_tiles, 0), 0)),
                  _layer_spec(wp, layer)],
        out_specs=out_specs,
        out_shape=out_shape,
        compiler_params=_cparams("arbitrary"),
        name="mlp_down_ple",
    )(h1, hid, w2, g_ple, wg, p_prompt, p_sample, wp)


def _cumsum_kernel(x_ref, o_ref):
    gs, nb, _ = x_ref.shape
    x2 = x_ref[...].reshape(gs * nb, LANES)
    r = lax.broadcasted_iota(jnp.int32, (LANES, LANES), 0)
    c = lax.broadcasted_iota(jnp.int32, (LANES, LANES), 1)
    upper = jnp.where(r <= c, 1.0, 0.0).astype(BF16)
    ones = jnp.ones((LANES, LANES), BF16)
    within = _dot_split3(x2, upper).reshape(gs, nb, LANES)
    total = _dot_split3(x2, ones).reshape(gs, nb, LANES)
    rb = lax.broadcasted_iota(jnp.int32, (nb, nb), 0)
    cb = lax.broadcasted_iota(jnp.int32, (nb, nb), 1)
    strict_lower = jnp.where(cb < rb, 1.0, 0.0).astype(BF16)
    for g in range(gs):
        hi, mid, lo = _split3(total[g])
        offs = (_dot(strict_lower, hi) + _dot(strict_lower, mid)) + _dot(strict_lower, lo)
        o_ref[g] = within[g] + offs


def _cumsum_lanes(x):
    n_in = x.shape[1]
    x = _pad_to(x, LANES * LANES, 1)
    groups, n = x.shape
    nb = n // LANES
    gs = min(groups, 16)
    out = pl.pallas_call(
        _cumsum_kernel,
        grid=(groups // gs,),
        in_specs=[pl.BlockSpec((gs, nb, LANES), lambda i: (i, 0, 0))],
        out_specs=pl.BlockSpec((gs, nb, LANES), lambda i: (i, 0, 0)),
        out_shape=jax.ShapeDtypeStruct((groups, nb, LANES), F32),
        compiler_params=_cparams("parallel"),
        name="logf_cumsum",
    )(x.reshape(groups, nb, LANES))
    return out.reshape(groups, n)[:, :n_in]


def _neg_suffix_matrix(tk):
    r = lax.broadcasted_iota(jnp.int32, (tk, tk), 0)
    c = lax.broadcasted_iota(jnp.int32, (tk, tk), 1)
    return jnp.where(r >= c, -1.0, 0.0).astype(BF16)


def _stick_tile(qrows, kb, vb, nsuf, carry, acc, mask, feature_major=False):
    slabs = qrows if isinstance(qrows, (list, tuple)) else None
    if slabs is not None:
        z = jnp.concatenate([_dot_nt(qs, kb[:, s * LANES:(s + 1) * LANES]) for s, qs in enumerate(slabs)], axis=0)
    else:
        z = _dot(qrows, kb) if feature_major else _dot_nt(qrows, kb)
    sp = _softplus(z)
    if mask is not None:
        sp = jnp.where(mask, sp, 0.0)
    incl = _dot_split2(sp, nsuf)
    w = jnp.exp(z + incl + carry)
    if mask is not None:
        w = jnp.where(mask, w, 0.0)
    wb = w.astype(BF16)
    if slabs is not None:
        rows = slabs[0].shape[0]
        pv = jnp.concatenate([_dot(wb[s * rows:(s + 1) * rows], vb[:, s * LANES:(s + 1) * LANES])
                              for s in range(len(slabs))], axis=0)
    else:
        pv = _dot_nt(wb, vb) if feature_major else _dot(wb, vb)
    acc = acc + pv
    carry = carry + incl[:, 0:1]
    return carry, acc


def _softmax_tile(logits, vb, m, l, acc, feature_major=False):
    m_new = jnp.maximum(m, jnp.max(logits, axis=1, keepdims=True))
    alpha = jnp.exp(m - m_new)
    p = jnp.exp(logits - m_new)
    l = alpha * l + jnp.sum(p, axis=1, keepdims=True)
    pb = p.astype(BF16)
    acc = alpha * acc + (_dot_nt(pb, vb) if feature_major else _dot(pb, vb))
    return m_new, l, acc


def _stack_heads(q, n_heads):
    t, width = q.shape
    rep = jnp.concatenate([q] * n_heads, axis=0) if n_heads > 1 else q
    row = lax.broadcasted_iota(jnp.int32, (n_heads * t, width), 0)
    lane = lax.broadcasted_iota(jnp.int32, (n_heads * t, width), 1)
    lo = (row // t) * HEAD_DIM
    keep = (lane >= lo) & (lane < lo + HEAD_DIM)
    return jnp.where(keep, rep, jnp.zeros_like(rep))


def _unstack_heads(acc, n_heads):
    rows, width = acc.shape
    t = rows // n_heads
    lane = lax.broadcasted_iota(jnp.int32, (t, width), 1)
    out = jnp.zeros((t, width), acc.dtype)
    for h in range(n_heads):
        sel = (lane >= h * HEAD_DIM) & (lane < (h + 1) * HEAD_DIM)
        out = jnp.where(sel, acc[h * t:(h + 1) * t, :], out)
    return out


def _row_query_index(rows, cols, t):
    r = lax.broadcasted_iota(jnp.int32, (rows, cols), 0)
    c = lax.broadcasted_iota(jnp.int32, (rows, cols), 1)
    return r % t, c


def _stick_prompt_kernel(q_ref, k_ref, v_ref, o_ref, carry_ref, acc_ref, *, t):
    i = pl.program_id(1)
    nh = HEADS_PER_SLAB
    n_slabs = q_ref.shape[1] // LANES
    rows = n_slabs * nh * t
    q = q_ref[...]
    qrows = [_stack_heads(q[:, s * LANES:(s + 1) * LANES], nh) for s in range(n_slabs)]
    nsuf = _neg_suffix_matrix(t)
    tq_idx, c_idx = _row_query_index(rows, t, t)

    def block(j):
        s = pl.multiple_of(j * t, t)
        return k_ref[pl.ds(s, t), :], v_ref[pl.ds(s, t), :]

    def diagonal():
        return _stick_tile(qrows, *block(i), nsuf, jnp.zeros((rows, 1), F32),
                           jnp.zeros((rows, LANES), F32), c_idx < tq_idx)

    @pl.when(i == 0)
    def _():
        carry, acc = diagonal()
        carry_ref[...] = carry
        acc_ref[...] = acc

    @pl.when(i > 0)
    def _():
        carry, acc = diagonal()
        carry, acc = _stick_tile(qrows, *block(i - 1), nsuf, carry, acc, None)
        carry_ref[...] = carry
        acc_ref[...] = acc

    def cond(state):
        j, carry_max = state
        return jnp.logical_and(j >= 0, carry_max > EXP_ZERO_BELOW)

    def body(state):
        j, _ = state
        carry, acc = _stick_tile(qrows, *block(j), nsuf, carry_ref[...], acc_ref[...], None)
        carry_ref[...] = carry
        acc_ref[...] = acc
        return j - 1, jnp.max(carry)

    lax.while_loop(cond, body, (i - 2, jnp.max(carry_ref[...])))
    acc = acc_ref[...]
    outs = [_unstack_heads(acc[s * nh * t:(s + 1) * nh * t], nh) for s in range(n_slabs)]
    o_ref[...] = (outs[0] if n_slabs == 1 else jnp.concatenate(outs, axis=1)).astype(o_ref.dtype)


def _stick_prompt(q, k, v, n, t=256, slabs_per_step=2):
    width = q.shape[1]
    t = min(t, n)
    group = slabs_per_step * LANES
    assert width % group == 0
    rows = slabs_per_step * HEADS_PER_SLAB * t
    return pl.pallas_call(
        functools.partial(_stick_prompt_kernel, t=t),
        grid=(width // group, n // t),
        in_specs=[pl.BlockSpec((t, group), lambda p, i: (i, p)),
                  pl.BlockSpec((n, group), lambda p, i: (0, p)),
                  pl.BlockSpec((n, group), lambda p, i: (0, p))],
        out_specs=pl.BlockSpec((t, group), lambda p, i: (i, p)),
        out_shape=jax.ShapeDtypeStruct((n, width), BF16),
        scratch_shapes=[pltpu.VMEM((rows, 1), F32), pltpu.VMEM((rows, LANES), F32)],
        compiler_params=_cparams("parallel", "parallel"),
        name="stick_prompt",
    )(q, k, v)


def _stick_sample_kernel(q_ref, kn_ref, vn_ref, ck_ref, cv_ref, nsuf_ref, o_ref, qrows_ref, carry_ref, acc_ref,
                         *, nh):
    jj = pl.program_id(1)
    t = q_ref.shape[1]

    @pl.when(jj == 0)
    def _():
        qrows = _stack_heads(q_ref[0], nh)
        qrows_ref[...] = qrows
        tq_idx, c_idx = _row_query_index(nh * t, t, t)
        carry, acc = _stick_tile(qrows, kn_ref[0], vn_ref[0], _neg_suffix_matrix(t),
                                 jnp.zeros((nh * t, 1), F32), jnp.zeros(acc_ref.shape, F32),
                                 c_idx < tq_idx)
        carry_ref[...] = carry
        acc_ref[...] = acc

    carry, acc = _stick_tile(qrows_ref[...], ck_ref[...].astype(BF16), cv_ref[...].astype(BF16),
                             nsuf_ref[...], carry_ref[...], acc_ref[...], None, feature_major=True)
    carry_ref[...] = carry
    acc_ref[...] = acc

    @pl.when(jj == pl.num_programs(1) - 1)
    def _():
        o_ref[0] = _unstack_heads(acc_ref[...], nh).astype(o_ref.dtype)


def _feature_major(cache):
    layers, b, past, heads, hd = cache.shape
    return jnp.transpose(cache, (0, 1, 3, 4, 2)).reshape(layers, b, heads * hd, past)


def _stick_sample(q, kn, vn, cache_kt, cache_vt, layer, tk=1024):
    b, t, width = q.shape
    past = cache_kt.shape[3]
    tk = min(tk, past)
    nblk = past // tk
    nh = width // HEAD_DIM
    r = jnp.arange(tk)
    nsuf = jnp.where(r[:, None] >= r[None, :], -1.0, 0.0).astype(BF16)
    new_spec = pl.BlockSpec((1, t, width), lambda bi, jj: (bi, 0, 0))
    cache_spec = pl.BlockSpec((None, None, width, tk), lambda bi, jj: (layer, bi, 0, nblk - 1 - jj))
    return pl.pallas_call(
        functools.partial(_stick_sample_kernel, nh=nh),
        grid=(b, nblk),
        in_specs=[new_spec, new_spec, new_spec, cache_spec, cache_spec,
                  pl.BlockSpec((tk, tk), lambda bi, jj: (0, 0))],
        out_specs=pl.BlockSpec((1, t, width), lambda bi, jj: (bi, 0, 0)),
        out_shape=jax.ShapeDtypeStruct((b, t, width), BF16),
        scratch_shapes=[pltpu.VMEM((nh * t, width), BF16),
                        pltpu.VMEM((nh * t, 1), F32),
                        pltpu.VMEM((nh * t, width), F32)],
        compiler_params=_cparams("parallel", "arbitrary"),
        name="stick_sample",
    )(q, kn, vn, cache_kt, cache_vt, nsuf)


def _band_prompt_kernel(q_ref, k0_ref, k1_ref, k2_ref, v0_ref, v1_ref, v2_ref, bias_ref, o_ref,
                        *, t, nprev, fixed_shift):
    i = pl.program_id(1)
    nh = HEADS_PER_SLAB
    n_slabs = q_ref.shape[1] // LANES
    q = q_ref[...]
    krefs = (k0_ref, k1_ref, k2_ref)[3 - (nprev + 1):]
    vrefs = (v0_ref, v1_ref, v2_ref)[3 - (nprev + 1):]
    kcat = jnp.concatenate([r[...] for r in krefs], axis=0)
    vcat = jnp.concatenate([r[...] for r in vrefs], axis=0)
    outs = []
    for s in range(n_slabs):
        lanes = slice(s * LANES, (s + 1) * LANES)
        qrows = _stack_heads(q[:, lanes], nh)
        logits = _dot_nt(qrows, kcat[:, lanes]) + bias_ref[s]
        c_idx = lax.broadcasted_iota(jnp.int32, logits.shape, 1)
        logits = jnp.where(c_idx >= (nprev - i) * t, logits, MASKED)
        vs = vcat[:, lanes]
        if fixed_shift:
            v_ones = jnp.concatenate([vs, jnp.ones(vs.shape, BF16)], axis=1)
            acc = _dot(jnp.exp(logits).astype(BF16), v_ones)
            acc = acc[:, :LANES] / acc[:, LANES:]
        else:
            m = jnp.max(logits, axis=1, keepdims=True)
            p = jnp.exp(logits - m)
            l = jnp.sum(p, axis=1, keepdims=True)
            acc = _dot(p.astype(BF16), vs) / l
        outs.append(_unstack_heads(acc, nh))
    o_ref[...] = (outs[0] if n_slabs == 1 else jnp.concatenate(outs, axis=1)).astype(o_ref.dtype)


def _rel_bias(rel_table, nq, nk, q0, k0):
    m_len = nq + nk - 1
    u = np.arange(m_len)
    d = np.where(u < nk, (q0 - k0) - u, (q0 - k0) + (m_len - u))
    g = jnp.take(rel_table.astype(F32), np.clip(d, -REL_CLIP, REL_CLIP) + REL_CLIP, axis=1)
    n_heads = g.shape[0]
    bias = jnp.tile(g, (1, nq))[:, :nq * (m_len - 1)].reshape(n_heads, nq, m_len - 1)[:, :, :nk]
    qp = q0 + np.arange(nq)[:, None]
    kp = k0 + np.arange(nk)[None, :]
    qc, kc = qp // CHUNK, kp // CHUNK
    visible = (kp >= 0) & (kc <= qc) & (kc >= qc - BAND_CHUNKS)
    return bias, jnp.asarray(visible)[None]


def _band_prompt(q, k, v, rel_table, n, logit_bound, t=256):
    width = q.shape[1]
    t = min(t, n)
    assert t % CHUNK == 0
    nprev = min(-(-BAND // t), 2)
    assert nprev * t >= BAND
    slabs = width // LANES
    bias, visible = _rel_bias(rel_table, t, (nprev + 1) * t, nprev * t, 0)
    bias_shape = (slabs, HEADS_PER_SLAB * t, (nprev + 1) * t)
    per_step = 2 if slabs % 2 == 0 else 1
    group = per_step * LANES

    def kv_spec(back):
        return pl.BlockSpec((t, group), lambda p, i: (jnp.maximum(i - back, 0), p))

    def call(masked_bias, fixed_shift):
        return pl.pallas_call(
            functools.partial(_band_prompt_kernel, t=t, nprev=nprev, fixed_shift=fixed_shift),
            grid=(slabs // per_step, n // t),
            in_specs=[pl.BlockSpec((t, group), lambda p, i: (i, p)),
                      kv_spec(2), kv_spec(1), kv_spec(0), kv_spec(2), kv_spec(1), kv_spec(0),
                      pl.BlockSpec((per_step,) + bias_shape[1:], lambda p, i: (p, 0, 0))],
            out_specs=pl.BlockSpec((t, group), lambda p, i: (i, p)),
            out_shape=jax.ShapeDtypeStruct((n, width), BF16),
            compiler_params=_cparams("parallel", "parallel"),
            name="band_prompt" if fixed_shift else "band_prompt_row_max",
        )(q, k, k, k, v, v, v, masked_bias.reshape(bias_shape))

    tab_max, tab_min = jnp.max(rel_table), jnp.min(rel_table)
    shift = logit_bound + tab_max
    spread = 2.0 * logit_bound + (tab_max - tab_min)
    return lax.cond(spread < 2.0 * MAX_FIXED_SHIFT,
                    lambda: call(jnp.where(visible, bias - shift, MASKED), True),
                    lambda: call(jnp.where(visible, bias, MASKED), False))


def _band_sample_kernel(q_ref, kn_ref, vn_ref, ck_ref, cv_ref, bc_ref, bn_ref, o_ref, *, nh):
    qrows = _stack_heads(q_ref[0], nh)
    lc = _dot(qrows, ck_ref[...].astype(BF16)) + bc_ref[...]
    ln = _dot_nt(qrows, kn_ref[0]) + bn_ref[...]
    m = jnp.maximum(jnp.max(lc, axis=1, keepdims=True), jnp.max(ln, axis=1, keepdims=True))
    pc = jnp.exp(lc - m)
    pn = jnp.exp(ln - m)
    l = jnp.sum(pc, axis=1, keepdims=True) + jnp.sum(pn, axis=1, keepdims=True)
    acc = (_dot_nt(pc.astype(BF16), cv_ref[...].astype(BF16)) + _dot(pn.astype(BF16), vn_ref[0])) / l
    o_ref[0] = _unstack_heads(acc, nh).astype(o_ref.dtype)


def _band_sample(q, kn, vn, cache_kt, cache_vt, layer, rel_table, past_len):
    b, t, width = q.shape
    buf_len = cache_kt.shape[3]
    nh = width // HEAD_DIM
    def masked_bias(nk, k0):
        bias, visible = _rel_bias(rel_table, t, nk, past_len, k0)
        return jnp.where(visible, bias, MASKED).reshape(nh * t, nk)

    bias_c = masked_bias(buf_len, past_len - buf_len)
    bias_n = masked_bias(t, past_len)
    new_spec = pl.BlockSpec((1, t, width), lambda bi: (bi, 0, 0))
    cache_spec = pl.BlockSpec((None, None, width, buf_len), lambda bi: (layer, bi, 0, 0))
    return pl.pallas_call(
        functools.partial(_band_sample_kernel, nh=nh),
        grid=(b,),
        in_specs=[new_spec, new_spec, new_spec, cache_spec, cache_spec,
                  _full_spec(bias_c.shape), _full_spec(bias_n.shape)],
        out_specs=pl.BlockSpec((1, t, width), lambda bi: (bi, 0, 0)),
        out_shape=jax.ShapeDtypeStruct((b, t, width), BF16),
        compiler_params=_cparams("parallel"),
        name="band_sample",
    )(q, kn, vn, cache_kt, cache_vt, bias_c, bias_n)


def _key_decay_rows(dk, t):
    n_heads, tk = dk.shape
    return jnp.concatenate([jnp.broadcast_to(dk[h:h + 1, :], (t, tk)) for h in range(n_heads)], axis=0)


def _fox_prompt_kernel(q_ref, dq_ref, k_ref, v_ref, dk_ref, o_ref, m_ref, l_ref, acc_ref, *, t):
    i = pl.program_id(1)
    nh = HEADS_PER_SLAB
    qrows = _stack_heads(q_ref[...], nh)
    dq = jnp.concatenate([dq_ref[:, h:h + 1] for h in range(nh)], axis=0)
    m_ref[...] = jnp.full(m_ref.shape, MASKED, F32)
    l_ref[...] = jnp.zeros(l_ref.shape, F32)
    acc_ref[...] = jnp.zeros(acc_ref.shape, F32)

    def tile(j, mask):
        s = pl.multiple_of(j * t, t)
        z = _dot_nt(qrows, k_ref[pl.ds(s, t), :])
        logits = (z + dq) - _key_decay_rows(dk_ref[:, pl.ds(s, t)], t)
        if mask is not None:
            logits = jnp.where(mask, logits, MASKED)
        m, l, acc = _softmax_tile(logits, v_ref[pl.ds(s, t), :], m_ref[...], l_ref[...], acc_ref[...])
        m_ref[...] = m
        l_ref[...] = l
        acc_ref[...] = acc

    def body(j, carry):
        tile(j, None)
        return carry

    lax.fori_loop(0, i, body, 0)
    tq_idx, c_idx = _row_query_index(nh * t, t, t)
    tile(i, c_idx <= tq_idx)
    o_ref[...] = _unstack_heads(acc_ref[...] / l_ref[...], nh).astype(o_ref.dtype)


def _fox_prompt_bounded_kernel(shift_ref, first_ref, q_ref, dq_ref, k_ref, v_ref, dk_ref, o_ref,
                               dqrep_ref, acc_ref, z_ref, *, tq, tk):
    slab = pl.program_id(0)
    i = pl.program_id(1)
    ratio = tq // tk
    q = q_ref[...]
    ones = jnp.ones((tk, LANES), BF16)
    row = lax.broadcasted_iota(jnp.int32, (tq, LANES), 0)
    lane = lax.broadcasted_iota(jnp.int32, (tq, LANES), 1)
    outs = []
    for hh in range(HEADS_PER_SLAB):
        qm = jnp.where((lane >= hh * HEAD_DIM) & (lane < (hh + 1) * HEAD_DIM), q, jnp.zeros_like(q))
        dqrep_ref[...] = jnp.broadcast_to(dq_ref[:, hh:hh + 1] - shift_ref[0, 0], (tq, LANES))
        acc_ref[...] = jnp.zeros(acc_ref.shape, F32)

        def scores(j, r0, qm=qm):
            s = pl.multiple_of(j * tk, tk)
            return _dot_nt(qm[r0:], k_ref[pl.ds(s, tk), :])

        def tile(j, r0, causal, next_r0, hh=hh, scores=scores):
            s = pl.multiple_of(j * tk, tk)
            z = z_ref[r0:, :]
            dqh = dqrep_ref[r0:, :]
            cols = []
            for c in range(tk // LANES):
                lg = (z[:, c * LANES:(c + 1) * LANES] + dqh) - dk_ref[hh:hh + 1, pl.ds(s + c * LANES, LANES)]
                if causal:
                    key_pos = s + c * LANES + lax.broadcasted_iota(jnp.int32, lg.shape, 1)
                    query_pos = i * tq + r0 + lax.broadcasted_iota(jnp.int32, lg.shape, 0)
                    lg = jnp.where(key_pos <= query_pos, lg, MASKED)
                cols.append(jnp.exp(lg).astype(BF16))
            if next_r0 is not None:
                z_next = scores(j + 1, next_r0)
            v_ones = jnp.concatenate([v_ref[pl.ds(s, tk), :], ones], axis=1)
            acc_ref[r0:, :] += _dot(jnp.concatenate(cols, axis=1), v_ones)
            if next_r0 is not None:
                z_ref[next_r0:, :] = z_next

        start = first_ref[slab * HEADS_PER_SLAB + hh, i]
        z_ref[...] = scores(start, 0)

        count = i * ratio - start

        def run(j, n_tiles, tile=tile):
            for u in range(n_tiles):
                tile(j + u, 0, False, 0)

        @pl.when((count & 1) == 1)
        def _(run=run, start=start):
            run(start, 1)

        after_one = start + (count & 1)

        @pl.when((count & 2) == 2)
        def _(run=run, after_one=after_one):
            run(after_one, 2)

        def quad_body(jj, carry, run=run, first_quad=after_one + (count & 2)):
            run(first_quad + 4 * jj, 4)
            return carry

        lax.fori_loop(0, lax.shift_right_logical(count, 2), quad_body, 0)
        for d in range(ratio):
            tile(i * ratio + d, d * tk, True, (d + 1) * tk if d + 1 < ratio else None)
        acc = acc_ref[...]
        outs.append(acc[:, :LANES] / acc[:, LANES:])
    o_ref[...] = jnp.where(lane < HEAD_DIM, outs[0], outs[1]).astype(o_ref.dtype)


MAX_FIXED_SHIFT = 30.0


def _fox_prompt(q, k, v, cum, n, logit_bound, t_exact=256, tq_bounded=1024, tk_bounded=512):
    width = q.shape[1]
    slabs = width // LANES
    nh = HEADS_PER_SLAB
    dq = cum.reshape(n, slabs, nh).transpose(1, 0, 2)
    dk = cum.reshape(n, slabs, nh).transpose(1, 2, 0)

    def specs(t):
        return ([pl.BlockSpec((t, LANES), lambda p, i: (i, p)),
                 pl.BlockSpec((None, t, nh), lambda p, i: (p, i, 0)),
                 pl.BlockSpec((n, LANES), lambda p, i: (0, p)),
                 pl.BlockSpec((n, LANES), lambda p, i: (0, p)),
                 pl.BlockSpec((None, nh, n), lambda p, i: (p, 0, 0))],
                pl.BlockSpec((t, LANES), lambda p, i: (i, p)))

    def exact(_):
        t = min(t_exact, n)
        in_specs, out_spec = specs(t)
        return pl.pallas_call(
            functools.partial(_fox_prompt_kernel, t=t),
            grid=(slabs, n // t),
            in_specs=in_specs, out_specs=out_spec,
            out_shape=jax.ShapeDtypeStruct((n, width), BF16),
            scratch_shapes=[pltpu.VMEM((nh * t, 1), F32), pltpu.VMEM((nh * t, 1), F32),
                            pltpu.VMEM((nh * t, LANES), F32)],
            compiler_params=_cparams("parallel", "parallel"),
            name="fox_prompt_running_max",
        )(q, dq, k, v, dk)

    def bounded(shift):
        tq = min(tq_bounded, n)
        tk = min(tk_bounded, n)
        in_specs, out_spec = specs(tq)
        cum_t = cum.T
        decay = cum_t[:, ::tq][:, :, None] - cum_t[:, tk - 1::tk][:, None, :]
        before = jnp.arange(n // tk)[None, None, :] < (jnp.arange(n // tq) * (tq // tk))[None, :, None]
        dead = jnp.logical_and(decay < EXP_ZERO_BELOW, before)
        first = jnp.sum(jnp.cumprod(dead.astype(jnp.int32), axis=2), axis=2).astype(jnp.int32)
        smem = pl.BlockSpec(memory_space=pltpu.SMEM)
        return pl.pallas_call(
            functools.partial(_fox_prompt_bounded_kernel, tq=tq, tk=tk),
            grid=(slabs, n // tq),
            in_specs=[smem, smem] + in_specs, out_specs=out_spec,
            out_shape=jax.ShapeDtypeStruct((n, width), BF16),
            scratch_shapes=[pltpu.VMEM((tq, LANES), F32), pltpu.VMEM((tq, 2 * LANES), F32),
                            pltpu.VMEM((tq, tk), F32)],
            compiler_params=_cparams("parallel", "parallel"),
            name="fox_prompt",
        )(shift.reshape(1, 1), first, q, dq, k, v, dk)

    return lax.cond(logit_bound < MAX_FIXED_SHIFT, bounded, exact, logit_bound)


def _fox_sample_kernel(q_ref, dq_ref, kn_ref, vn_ref, dkn_ref, ck_ref, cv_ref, dkc_ref, o_ref,
                       qrows_ref, m_ref, l_ref, acc_ref, *, nh):
    jj = pl.program_id(1)
    t = q_ref.shape[1]
    dq = dq_ref[0]

    @pl.when(jj == 0)
    def _():
        qrows = _stack_heads(q_ref[0], nh)
        qrows_ref[...] = qrows
        tq_idx, c_idx = _row_query_index(nh * t, t, t)
        logits = (_dot_nt(qrows, kn_ref[0]) + dq) - _key_decay_rows(dkn_ref[0], t)
        logits = jnp.where(c_idx <= tq_idx, logits, MASKED)
        m, l, acc = _softmax_tile(logits, vn_ref[0], jnp.full(m_ref.shape, MASKED, F32),
                                  jnp.zeros(l_ref.shape, F32), jnp.zeros(acc_ref.shape, F32))
        m_ref[...] = m
        l_ref[...] = l
        acc_ref[...] = acc

    logits = (_dot(qrows_ref[...], ck_ref[...].astype(BF16)) + dq) - _key_decay_rows(dkc_ref[0], t)
    m, l, acc = _softmax_tile(logits, cv_ref[...].astype(BF16), m_ref[...], l_ref[...], acc_ref[...],
                              feature_major=True)
    m_ref[...] = m
    l_ref[...] = l
    acc_ref[...] = acc

    @pl.when(jj == pl.num_programs(1) - 1)
    def _():
        o_ref[0] = _unstack_heads(acc_ref[...] / l_ref[...], nh).astype(o_ref.dtype)


def _fox_sample(q, kn, vn, cache_kt, cache_vt, layer, cum_cache, cum_new, tk=1024):
    b, t, width = q.shape
    past = cache_kt.shape[3]
    tk = min(tk, past)
    nblk = past // tk
    nh = width // HEAD_DIM
    dq = cum_new.reshape(b, nh * t, 1)
    new_spec = pl.BlockSpec((1, t, width), lambda bi, jj: (bi, 0, 0))
    cache_spec = pl.BlockSpec((None, None, width, tk), lambda bi, jj: (layer, bi, 0, jj))
    return pl.pallas_call(
        functools.partial(_fox_sample_kernel, nh=nh),
        grid=(b, nblk),
        in_specs=[new_spec, pl.BlockSpec((1, nh * t, 1), lambda bi, jj: (bi, 0, 0)),
                  new_spec, new_spec, pl.BlockSpec((1, nh, t), lambda bi, jj: (bi, 0, 0)),
                  cache_spec, cache_spec, pl.BlockSpec((1, nh, tk), lambda bi, jj: (bi, 0, jj))],
        out_specs=pl.BlockSpec((1, t, width), lambda bi, jj: (bi, 0, 0)),
        out_shape=jax.ShapeDtypeStruct((b, t, width), BF16),
        scratch_shapes=[pltpu.VMEM((nh * t, width), BF16), pltpu.VMEM((nh * t, 1), F32),
                        pltpu.VMEM((nh * t, 1), F32), pltpu.VMEM((nh * t, width), F32)],
        compiler_params=_cparams("parallel", "arbitrary"),
        name="fox_sample",
    )(q, dq, kn, vn, cum_new, cache_kt, cache_vt, cum_cache)


def _pad_to(x, multiple, axis):
    size = x.shape[axis]
    target = -(-size // multiple) * multiple
    if target == size:
        return x
    pad = [(0, 0)] * x.ndim
    pad[axis] = (0, target - size)
    return jnp.pad(x, pad)


def kernel(x_prompt, x_sample, cache_a_k, cache_a_v, cache_b_k, cache_b_v, cache_c_k, cache_c_v, cache_c_logf, p_prompt, p_sample, g_mix, w_in_even, g_qb, g_kb, rel_bias, w_in_odd, b_forget, g_qc, g_kc, w_out, g_mlp, w_ff1, w_ff2, g_ple, w_ple_gate, w_ple_proj):
    bp, n, d = x_prompt.shape
    bs, ts, _ = x_sample.shape
    assert bp == 1, "prompt kernels assume a single prompt stream"
    depth = g_mix.shape[0]
    past = cache_a_k.shape[2]
    buf_len = cache_b_k.shape[2]
    ha, hb, hc = cache_a_k.shape[3], cache_b_k.shape[3], cache_c_k.shape[3]
    wa, wb, wc = ha * HEAD_DIM, hb * HEAD_DIM, hc * HEAD_DIM
    rows_s = bs * ts

    blk = jnp.arange(NORM_SLAB) // HEAD_DIM
    gmat = jnp.where(blk[:, None] == blk[None, :], 1.0 / HEAD_DIM, 0.0).astype(BF16)

    cache_a_kt, cache_a_vt = _feature_major(cache_a_k), _feature_major(cache_a_v)
    cache_b_kt, cache_b_vt = _feature_major(cache_b_k), _feature_major(cache_b_v)
    cache_c_kt, cache_c_vt = _feature_major(cache_c_k), _feature_major(cache_c_v)

    rows = n + rows_s
    tail = -rows % ROW_TILE
    h = jnp.concatenate([x_prompt.reshape(n, d), x_sample.reshape(rows_s, d), jnp.zeros((tail, d), F32)], axis=0)
    sample_pad = rows + tail - n
    d_ple = p_prompt.shape[-1]
    p_prompt2 = p_prompt.reshape(depth, n, d_ple)
    p_sample2 = _pad_to(p_sample.reshape(depth, rows_s, d_ple), sample_pad, 1)
    w_even16 = w_in_even.astype(BF16)
    w_odd16 = _pad_to(w_in_odd, LANES, 2).astype(BF16)
    w_out16, w_ff1_16, w_ff2_16 = w_out.astype(BF16), w_ff1.astype(BF16), w_ff2.astype(BF16)
    w_gate16, w_proj16 = w_ple_gate.astype(BF16), w_ple_proj.astype(BF16)

    def prompt_state(per_layer, heads, lo=0):
        xt = jnp.concatenate([x[:, :, lo:] for x in per_layer], axis=0)
        return jnp.transpose(xt.reshape(len(per_layer), heads, HEAD_DIM, n - lo), (0, 3, 1, 2))[:, None]

    def sample_state(xt, heads):
        return jnp.transpose(xt[:, :rows_s].reshape(heads, HEAD_DIM, bs, ts), (2, 3, 0, 1))

    st = {name: [] for name in ("pc_f", "sa_k", "sa_v", "sb_k", "sb_v", "sc_k", "sc_v", "sc_f")}
    even_states, odd_states = [], []
    for i in range(depth):
        j = i // 2
        g_row = g_mix[i].reshape(1, d)
        if i % 2 == 0:
            gq_row = jnp.tile(g_qb[j], hb).reshape(1, wb)
            gk_row = jnp.tile(g_kb[j], hb).reshape(1, wb)
            outs = _proj_even(h, g_row, w_even16, j, gq_row, gk_row, gmat, wa, wb, n)
            qa, ka16, va16, qb, kb16, vb16 = outs[:6]
            even_states.append(outs[6:10])
            ka_s, va_s, kb_s, vb_s = outs[10:]
            mix_a_p = _stick_prompt(qa, ka16, va16, n)
            band_bound = 1.01 * HEAD_DIM * SCALE * jnp.max(jnp.abs(g_qb[j])) * jnp.max(jnp.abs(g_kb[j]))
            mix_b_p = _band_prompt(qb, kb16, vb16, rel_bias[j], n, band_bound.astype(F32))
            s3 = lambda a: a[n:rows].reshape(bs, ts, a.shape[1])
            mix_a_s = _stick_sample(s3(qa), s3(ka16), s3(va16), cache_a_kt, cache_a_vt, j)
            mix_b_s = _band_sample(s3(qb), s3(kb16), s3(vb16), cache_b_kt, cache_b_vt, j, rel_bias[j], past)
            mix_prompt = [mix_a_p, mix_b_p]
            mix_sample = _pad_to(jnp.concatenate([mix_a_s.reshape(rows_s, wa), mix_b_s.reshape(rows_s, wb)],
                                                 axis=1), sample_pad, 0)
            st["sa_k"].append(sample_state(ka_s, ha))
            st["sa_v"].append(sample_state(va_s, ha))
            st["sb_k"].append(sample_state(kb_s, hb))
            st["sb_v"].append(sample_state(vb_s, hb))
        else:
            gq_row = jnp.tile(g_qc[j], hc).reshape(1, wc)
            gk_row = jnp.tile(g_kc[j], hc).reshape(1, wc)
            bf_row = _pad_to(b_forget[j].reshape(1, hc), LANES, 1)
            outs = _proj_odd(h, g_row, w_odd16, j, gq_row, gk_row, gmat, bf_row, wc, n)
            q, k16, v16, lf = outs[:4]
            odd_states.append(outs[4:6])
            k_s, v_s = outs[6:]
            log_f = lf[:, :hc]
            cum_p = _cumsum_lanes(log_f[:n].T)
            logit_bound = 1.01 * HEAD_DIM * SCALE * jnp.max(jnp.abs(g_qc[j])) * jnp.max(jnp.abs(g_kc[j]))
            mix_p = _fox_prompt(q, k16, v16, cum_p.T, n, logit_bound.astype(F32))
            lf_new = log_f[n:rows].reshape(bs, ts, hc).transpose(0, 2, 1)
            lf_all = jnp.concatenate([cache_c_logf[j].transpose(0, 2, 1), lf_new], axis=2)
            cum_s = _cumsum_lanes(lf_all.reshape(bs * hc, past + ts)).reshape(bs, hc, past + ts)
            s3 = lambda a: a[n:rows].reshape(bs, ts, a.shape[1])
            mix_s = _fox_sample(s3(q), s3(k16), s3(v16), cache_c_kt, cache_c_vt, j,
                                cum_s[:, :, :past], cum_s[:, :, past:past + ts])
            mix_prompt = [mix_p]
            mix_sample = _pad_to(mix_s.reshape(rows_s, wc), sample_pad, 0)
            st["pc_f"].append(log_f[:n].reshape(1, n, hc))
            st["sc_k"].append(sample_state(k_s, hc))
            st["sc_v"].append(sample_state(v_s, hc))
            st["sc_f"].append(log_f[n:rows].reshape(bs, ts, hc))
        h = _post(h, mix_prompt, mix_sample, n, i, w_out16, g_mlp[i].reshape(1, d), w_ff1_16, w_ff2_16,
                  g_ple[i].reshape(1, d), w_gate16, p_prompt2, p_sample2, w_proj16, split_output=i == depth - 1)

    y_prompt = h[0].reshape(1, n, d)
    y_sample = h[1][:rows_s].reshape(bs, ts, d)
    stk = {name: jnp.stack(vals) for name, vals in st.items()}
    assert buf_len >= ts
    stk["sb_k"] = jnp.concatenate([cache_b_k[:, :, ts:], stk["sb_k"]], axis=2)
    stk["sb_v"] = jnp.concatenate([cache_b_v[:, :, ts:], stk["sb_v"]], axis=2)
    keep = min(BAND, n)
    ka_l, va_l, kb_l, vb_l = zip(*even_states)
    kc_l, vc_l = zip(*odd_states)
    stk["pa_k"], stk["pa_v"] = prompt_state(ka_l, ha), prompt_state(va_l, ha)
    stk["pb_k"], stk["pb_v"] = prompt_state(kb_l, hb, n - keep), prompt_state(vb_l, hb, n - keep)
    stk["pc_k"], stk["pc_v"] = prompt_state(kc_l, hc), prompt_state(vc_l, hc)
    return (y_prompt, y_sample, stk["pa_k"], stk["pa_v"], stk["pb_k"], stk["pb_v"],
            stk["pc_k"], stk["pc_v"], stk["pc_f"], stk["sa_k"], stk["sa_v"], stk["sb_k"], stk["sb_v"],
            stk["sc_k"], stk["sc_v"], stk["sc_f"])
```

```python
import functools

import numpy as np
import jax
import jax.numpy as jnp
from jax import lax
from jax.experimental import pallas as pl
from jax.experimental.pallas import tpu as pltpu

F32 = jnp.float32
BF16 = jnp.bfloat16

HEAD_DIM = 64
CHUNK = 64
BAND_CHUNKS = 8
BAND = BAND_CHUNKS * CHUNK
REL_CLIP = 128
EPS = 1e-6
SCALE = HEAD_DIM ** -0.5
LANES = 128
HEADS_PER_SLAB = LANES // HEAD_DIM
NORM_SLAB = 256
EXP_ZERO_BELOW = -104.0
MASKED = -1e30
VMEM_LIMIT = 56 * 1024 * 1024


def _cparams(*sem):
    return pltpu.CompilerParams(dimension_semantics=sem, vmem_limit_bytes=VMEM_LIMIT)


def _split2(x):
    hi = x.astype(BF16)
    lo = (x - hi.astype(F32)).astype(BF16)
    return hi, lo


def _split3(x):
    hi = x.astype(BF16)
    r = x - hi.astype(F32)
    mid = r.astype(BF16)
    lo = (r - mid.astype(F32)).astype(BF16)
    return hi, mid, lo


def _dot(a, b):
    return jnp.dot(a, b, preferred_element_type=F32)


def _dot_nt(a, b):
    return lax.dot_general(a, b, (((1,), (1,)), ((), ())), preferred_element_type=F32)


def _dot_split2(x, m):
    hi, lo = _split2(x)
    return _dot(hi, m) + _dot(lo, m)


def _dot_split3(x, m):
    hi, mid, lo = _split3(x)
    return (_dot(hi, m) + _dot(mid, m)) + _dot(lo, m)


def _rms_rows(x, g):
    ms = jnp.mean(x * x, axis=-1, keepdims=True)
    return x * lax.rsqrt(ms + EPS) * g


def _head_rms(x, gmat, gain_row):
    width = x.shape[1]
    outs = []
    for s in range(width // NORM_SLAB):
        xs = x[:, NORM_SLAB * s:NORM_SLAB * (s + 1)]
        ms = _dot_split2(xs * xs, gmat)
        outs.append(xs * lax.rsqrt(ms + EPS))
    y = outs[0] if len(outs) == 1 else jnp.concatenate(outs, axis=1)
    return y * gain_row


def _softplus(z):
    return jnp.maximum(z, 0.0) + jnp.log(1.0 + jnp.exp(-jnp.abs(z)))


def _log_sigmoid(z):
    return jnp.minimum(z, 0.0) - jnp.log(1.0 + jnp.exp(-jnp.abs(z)))


def _store_state(values, prompt_refs, sample_refs, prompt_tiles):
    step = pl.program_id(0)

    @pl.when(step < prompt_tiles)
    def _():
        for value, ref in zip(values, prompt_refs):
            ref[...] = value.T

    @pl.when(step >= prompt_tiles)
    def _():
        for value, ref in zip(values, sample_refs):
            ref[...] = value.T


def _proj_even_kernel(h_ref, g_ref, w_ref, gq_ref, gk_ref, gmat_ref,
                      qa_ref, ka16_ref, va16_ref, qb_ref, kb16_ref, vb16_ref,
                      ka_p_ref, va_p_ref, kb_p_ref, vb_p_ref, ka_s_ref, va_s_ref, kb_s_ref, vb_s_ref,
                      *, wa, wb, prompt_tiles):
    xn = _rms_rows(h_ref[...], g_ref[...]).astype(BF16)
    proj = _dot(xn, w_ref[...])
    gmat = gmat_ref[...]
    qa = proj[:, 0:wa]
    ka = proj[:, wa:2 * wa]
    va = proj[:, 2 * wa:3 * wa]
    o = 3 * wa
    qb = _head_rms(proj[:, o:o + wb], gmat, gq_ref[...])
    kb = _head_rms(proj[:, o + wb:o + 2 * wb], gmat, gk_ref[...])
    vb = proj[:, o + 2 * wb:o + 3 * wb]
    qa_ref[...] = (qa * SCALE).astype(BF16)
    ka16_ref[...] = ka.astype(BF16)
    va16_ref[...] = va.astype(BF16)
    qb_ref[...] = (qb * SCALE).astype(BF16)
    kb16_ref[...] = kb.astype(BF16)
    vb16_ref[...] = vb.astype(BF16)
    _store_state([ka, va, kb, vb], [ka_p_ref, va_p_ref, kb_p_ref, vb_p_ref],
                 [ka_s_ref, va_s_ref, kb_s_ref, vb_s_ref], prompt_tiles)


def _proj_odd_kernel(h_ref, g_ref, w_ref, gq_ref, gk_ref, gmat_ref, bf_ref,
                     q_ref, k16_ref, v16_ref, lf_ref, k_p_ref, v_p_ref, k_s_ref, v_s_ref, *, wc, prompt_tiles):
    xn = _rms_rows(h_ref[...], g_ref[...]).astype(BF16)
    proj = _dot(xn, w_ref[...])
    gmat = gmat_ref[...]
    q = _head_rms(proj[:, 0:wc], gmat, gq_ref[...])
    k = _head_rms(proj[:, wc:2 * wc], gmat, gk_ref[...])
    v = proj[:, 2 * wc:3 * wc]
    f = proj[:, 3 * wc:3 * wc + LANES]
    q_ref[...] = (q * SCALE).astype(BF16)
    k16_ref[...] = k.astype(BF16)
    v16_ref[...] = v.astype(BF16)
    _store_state([k, v], [k_p_ref, v_p_ref], [k_s_ref, v_s_ref], prompt_tiles)
    lf_ref[...] = _log_sigmoid(f + bf_ref[...])


def _post1_kernel(*refs, n_mix, ff_chunk, prompt_tiles):
    h_ref, mix_p_refs, mix_s_ref = refs[0], refs[1:1 + n_mix], refs[1 + n_mix]
    wo_ref, g_ref, w1_ref, h1_ref, hid_ref = refs[2 + n_mix:]
    pieces = [r[...] for r in mix_p_refs]
    mix_p = pieces[0] if n_mix == 1 else jnp.concatenate(pieces, axis=1)
    mix = jnp.where(pl.program_id(0) < prompt_tiles, mix_p, mix_s_ref[...])
    h1 = h_ref[...] + _dot(mix, wo_ref[...])
    h1_ref[...] = h1
    xn = _rms_rows(h1, g_ref[...]).astype(BF16)
    d_ff = w1_ref.shape[1]
    for c in range(d_ff // ff_chunk):
        sl = slice(c * ff_chunk, (c + 1) * ff_chunk)
        a = jnp.maximum(_dot(xn, w1_ref[:, sl]), 0.0)
        hid_ref[:, sl] = (a * a).astype(BF16)


def _post2_kernel(h1_ref, hid_ref, w2_ref, g_ref, wg_ref, p_p_ref, p_s_ref, wp_ref, *out_refs, prompt_tiles):
    step = pl.program_id(0)
    h2 = h1_ref[...] + _dot(hid_ref[...], w2_ref[...])
    xg = _rms_rows(h2, g_ref[...]).astype(BF16)
    gate = jax.nn.sigmoid(_dot(xg, wg_ref[...]))
    p = jnp.where(step < prompt_tiles, p_p_ref[...], p_s_ref[...]).astype(BF16)
    out = h2 + gate * _dot(p, wp_ref[...])
    if len(out_refs) == 1:
        out_refs[0][...] = out
    else:
        @pl.when(step < prompt_tiles)
        def _():
            out_refs[0][...] = out

        @pl.when(step >= prompt_tiles)
        def _():
            out_refs[1][...] = out


def _row_spec(tm, width):
    return pl.BlockSpec((tm, width), lambda i: (i, 0))


def _full_spec(shape):
    return pl.BlockSpec(shape, lambda i: (0,) * len(shape))


ROW_TILE = 512


def _row_tile(rows):
    assert rows % ROW_TILE == 0, rows
    return ROW_TILE


def _proj_call(body, name, inputs, layer, n, row_widths, row_dtypes, state_widths):
    h = inputs[0]
    rows, d = h.shape
    tm = _row_tile(rows)
    assert n % tm == 0, (n, tm)
    prompt_tiles = n // tm
    prompt_spec = lambda wd: pl.BlockSpec((None, wd, tm), lambda i: (0, 0, jnp.minimum(i, prompt_tiles - 1)))
    sample_spec = lambda wd: pl.BlockSpec((wd, tm), lambda i: (0, jnp.maximum(i - prompt_tiles, 0)))
    return pl.pallas_call(
        functools.partial(body, prompt_tiles=prompt_tiles),
        grid=(rows // tm,),
        in_specs=[_row_spec(tm, d)] + [_layer_spec(x, layer) if x.ndim == 3 else _full_spec(x.shape)
                                       for x in inputs[1:]],
        out_specs=([_row_spec(tm, wd) for wd in row_widths] + [prompt_spec(wd) for wd in state_widths]
                   + [sample_spec(wd) for wd in state_widths]),
        out_shape=([jax.ShapeDtypeStruct((rows, wd), dt) for wd, dt in zip(row_widths, row_dtypes)]
                   + [jax.ShapeDtypeStruct((1, wd, n), F32) for wd in state_widths]
                   + [jax.ShapeDtypeStruct((wd, rows - n), F32) for wd in state_widths]),
        compiler_params=_cparams("arbitrary"),
        name=name,
    )(*inputs)


def _proj_even(h, g, w_stack, layer, gq_row, gk_row, gmat, wa, wb, n):
    return _proj_call(functools.partial(_proj_even_kernel, wa=wa, wb=wb), "proj_even",
                      (h, g, w_stack, gq_row, gk_row, gmat), layer, n,
                      [wa, wa, wa, wb, wb, wb], [BF16] * 6, [wa, wa, wb, wb])


def _proj_odd(h, g, w_stack, layer, gq_row, gk_row, gmat, bf_row, wc, n):
    return _proj_call(functools.partial(_proj_odd_kernel, wc=wc), "proj_odd",
                      (h, g, w_stack, gq_row, gk_row, gmat, bf_row), layer, n,
                      [wc, wc, wc, LANES], [BF16] * 3 + [F32], [wc, wc])


def _layer_spec(stacked, layer):
    return pl.BlockSpec((None,) + stacked.shape[1:], lambda i: (layer, 0, 0))


def _post(h, mix_prompt, mix_sample, n, layer, wo, g_mlp, w1, w2, g_ple, wg, p_prompt, p_sample, wp, split_output):
    rows, d = h.shape
    tm = _row_tile(rows)
    assert n % tm == 0, (n, tm)
    prompt_tiles = n // tm
    d_ff = w1.shape[2]
    d_ple = p_prompt.shape[2]
    ff_chunk = 1024 if d_ff % 1024 == 0 else d_ff
    prompt_rows = lambda wd: pl.BlockSpec((tm, wd), lambda i: (jnp.minimum(i, prompt_tiles - 1), 0))
    sample_rows = lambda wd: pl.BlockSpec((tm, wd), lambda i: (jnp.maximum(i - prompt_tiles, 0), 0))
    h1, hid = pl.pallas_call(
        functools.partial(_post1_kernel, n_mix=len(mix_prompt), ff_chunk=ff_chunk, prompt_tiles=prompt_tiles),
        grid=(rows // tm,),
        in_specs=([_row_spec(tm, d)] + [prompt_rows(m.shape[1]) for m in mix_prompt]
                  + [sample_rows(mix_sample.shape[1]), _layer_spec(wo, layer), _full_spec(g_mlp.shape),
                     _layer_spec(w1, layer)]),
        out_specs=[_row_spec(tm, d), _row_spec(tm, d_ff)],
        out_shape=[jax.ShapeDtypeStruct((rows, d), F32), jax.ShapeDtypeStruct((rows, d_ff), BF16)],
        compiler_params=_cparams("arbitrary"),
        name="post_attn_mlp_up",
    )(h, *mix_prompt, mix_sample, wo, g_mlp, w1)
    if split_output:
        out_specs = [prompt_rows(d), sample_rows(d)]
        out_shape = [jax.ShapeDtypeStruct((n, d), F32), jax.ShapeDtypeStruct((rows - n, d), F32)]
    else:
        out_specs = _row_spec(tm, d)
        out_shape = jax.ShapeDtypeStruct((rows, d), F32)
    return pl.pallas_call(
        functools.partial(_post2_kernel, prompt_tiles=prompt_tiles),
        grid=(rows // tm,),
        in_specs=[_row_spec(tm, d), _row_spec(tm, d_ff), _layer_spec(w2, layer), _full_spec(g_ple.shape),
                  _layer_spec(wg, layer),
                  pl.BlockSpec((None, tm, d_ple), lambda i: (layer, jnp.minimum(i, prompt_tiles - 1), 0)),
                  pl.BlockSpec((None, tm, d_ple), lambda i: (layer, jnp.maximum(i - prompt_tiles, 0), 0)),
                  _layer_spec(wp, layer)],
        out_specs=out_specs,
        out_shape=out_shape,
        compiler_params=_cparams("arbitrary"),
        name="mlp_down_ple",
    )(h1, hid, w2, g_ple, wg, p_prompt, p_sample, wp)


def _cumsum_kernel(x_ref, o_ref):
    gs, nb, _ = x_ref.shape
    x2 = x_ref[...].reshape(gs * nb, LANES)
    r = lax.broadcasted_iota(jnp.int32, (LANES, LANES), 0)
    c = lax.broadcasted_iota(jnp.int32, (LANES, LANES), 1)
    upper = jnp.where(r <= c, 1.0, 0.0).astype(BF16)
    ones = jnp.ones((LANES, LANES), BF16)
    within = _dot_split3(x2, upper).reshape(gs, nb, LANES)
    total = _dot_split3(x2, ones).reshape(gs, nb, LANES)
    rb = lax.broadcasted_iota(jnp.int32, (nb, nb), 0)
    cb = lax.broadcasted_iota(jnp.int32, (nb, nb), 1)
    strict_lower = jnp.where(cb < rb, 1.0, 0.0).astype(BF16)
    for g in range(gs):
        hi, mid, lo = _split3(total[g])
        offs = (_dot(strict_lower, hi) + _dot(strict_lower, mid)) + _dot(strict_lower, lo)
        o_ref[g] = within[g] + offs


def _cumsum_lanes(x):
    n_in = x.shape[1]
    x = _pad_to(x, LANES * LANES, 1)
    groups, n = x.shape
    nb = n // LANES
    gs = min(groups, 16)
    out = pl.pallas_call(
        _cumsum_kernel,
        grid=(groups // gs,),
        in_specs=[pl.BlockSpec((gs, nb, LANES), lambda i: (i, 0, 0))],
        out_specs=pl.BlockSpec((gs, nb, LANES), lambda i: (i, 0, 0)),
        out_shape=jax.ShapeDtypeStruct((groups, nb, LANES), F32),
        compiler_params=_cparams("parallel"),
        name="logf_cumsum",
    )(x.reshape(groups, nb, LANES))
    return out.reshape(groups, n)[:, :n_in]


def _neg_suffix_matrix(tk):
    r = lax.broadcasted_iota(jnp.int32, (tk, tk), 0)
    c = lax.broadcasted_iota(jnp.int32, (tk, tk), 1)
    return jnp.where(r >= c, -1.0, 0.0).astype(BF16)


def _stick_tile(qrows, kb, vb, nsuf, carry, acc, mask, feature_major=False):
    slabs = qrows if isinstance(qrows, (list, tuple)) else None
    if slabs is not None:
        z = jnp.concatenate([_dot_nt(qs, kb[:, s * LANES:(s + 1) * LANES]) for s, qs in enumerate(slabs)], axis=0)
    else:
        z = _dot(qrows, kb) if feature_major else _dot_nt(qrows, kb)
    sp = _softplus(z)
    if mask is not None:
        sp = jnp.where(mask, sp, 0.0)
    incl = _dot_split2(sp, nsuf)
    w = jnp.exp(z + incl + carry)
    if mask is not None:
        w = jnp.where(mask, w, 0.0)
    wb = w.astype(BF16)
    if slabs is not None:
        rows = slabs[0].shape[0]
        pv = jnp.concatenate([_dot(wb[s * rows:(s + 1) * rows], vb[:, s * LANES:(s + 1) * LANES])
                              for s in range(len(slabs))], axis=0)
    else:
        pv = _dot_nt(wb, vb) if feature_major else _dot(wb, vb)
    acc = acc + pv
    carry = carry + incl[:, 0:1]
    return carry, acc


def _softmax_tile(logits, vb, m, l, acc, feature_major=False):
    m_new = jnp.maximum(m, jnp.max(logits, axis=1, keepdims=True))
    alpha = jnp.exp(m - m_new)
    p = jnp.exp(logits - m_new)
    l = alpha * l + jnp.sum(p, axis=1, keepdims=True)
    pb = p.astype(BF16)
    acc = alpha * acc + (_dot_nt(pb, vb) if feature_major else _dot(pb, vb))
    return m_new, l, acc


def _stack_heads(q, n_heads):
    t, width = q.shape
    rep = jnp.concatenate([q] * n_heads, axis=0) if n_heads > 1 else q
    row = lax.broadcasted_iota(jnp.int32, (n_heads * t, width), 0)
    lane = lax.broadcasted_iota(jnp.int32, (n_heads * t, width), 1)
    lo = (row // t) * HEAD_DIM
    keep = (lane >= lo) & (lane < lo + HEAD_DIM)
    return jnp.where(keep, rep, jnp.zeros_like(rep))


def _unstack_heads(acc, n_heads):
    rows, width = acc.shape
    t = rows // n_heads
    lane = lax.broadcasted_iota(jnp.int32, (t, width), 1)
    out = jnp.zeros((t, width), acc.dtype)
    for h in range(n_heads):
        sel = (lane >= h * HEAD_DIM) & (lane < (h + 1) * HEAD_DIM)
        out = jnp.where(sel, acc[h * t:(h + 1) * t, :], out)
    return out


def _row_query_index(rows, cols, t):
    r = lax.broadcasted_iota(jnp.int32, (rows, cols), 0)
    c = lax.broadcasted_iota(jnp.int32, (rows, cols), 1)
    return r % t, c


def _stick_prompt_kernel(q_ref, k_ref, v_ref, o_ref, carry_ref, acc_ref, *, t):
    i = pl.program_id(1)
    nh = HEADS_PER_SLAB
    n_slabs = q_ref.shape[1] // LANES
    rows = n_slabs * nh * t
    q = q_ref[...]
    qrows = [_stack_heads(q[:, s * LANES:(s + 1) * LANES], nh) for s in range(n_slabs)]
    nsuf = _neg_suffix_matrix(t)
    tq_idx, c_idx = _row_query_index(rows, t, t)

    def block(j):
        s = pl.multiple_of(j * t, t)
        return k_ref[pl.ds(s, t), :], v_ref[pl.ds(s, t), :]

    def diagonal():
        return _stick_tile(qrows, *block(i), nsuf, jnp.zeros((rows, 1), F32),
                           jnp.zeros((rows, LANES), F32), c_idx < tq_idx)

    @pl.when(i == 0)
    def _():
        carry, acc = diagonal()
        carry_ref[...] = carry
        acc_ref[...] = acc

    @pl.when(i > 0)
    def _():
        carry, acc = diagonal()
        carry, acc = _stick_tile(qrows, *block(i - 1), nsuf, carry, acc, None)
        carry_ref[...] = carry
        acc_ref[...] = acc

    def cond(state):
        j, carry_max = state
        return jnp.logical_and(j >= 0, carry_max > EXP_ZERO_BELOW)

    def body(state):
        j, _ = state
        carry, acc = _stick_tile(qrows, *block(j), nsuf, carry_ref[...], acc_ref[...], None)
        carry_ref[...] = carry
        acc_ref[...] = acc
        return j - 1, jnp.max(carry)

    lax.while_loop(cond, body, (i - 2, jnp.max(carry_ref[...])))
    acc = acc_ref[...]
    outs = [_unstack_heads(acc[s * nh * t:(s + 1) * nh * t], nh) for s in range(n_slabs)]
    o_ref[...] = (outs[0] if n_slabs == 1 else jnp.concatenate(outs, axis=1)).astype(o_ref.dtype)


def _stick_prompt(q, k, v, n, t=256, slabs_per_step=2):
    width = q.shape[1]
    t = min(t, n)
    group = slabs_per_step * LANES
    assert width % group == 0
    rows = slabs_per_step * HEADS_PER_SLAB * t
    return pl.pallas_call(
        functools.partial(_stick_prompt_kernel, t=t),
        grid=(width // group, n // t),
        in_specs=[pl.BlockSpec((t, group), lambda p, i: (i, p)),
                  pl.BlockSpec((n, group), lambda p, i: (0, p)),
                  pl.BlockSpec((n, group), lambda p, i: (0, p))],
        out_specs=pl.BlockSpec((t, group), lambda p, i: (i, p)),
        out_shape=jax.ShapeDtypeStruct((n, width), BF16),
        scratch_shapes=[pltpu.VMEM((rows, 1), F32), pltpu.VMEM((rows, LANES), F32)],
        compiler_params=_cparams("parallel", "parallel"),
        name="stick_prompt",
    )(q, k, v)


def _stick_sample_kernel(q_ref, kn_ref, vn_ref, ck_ref, cv_ref, nsuf_ref, o_ref, qrows_ref, carry_ref, acc_ref,
                         *, nh):
    jj = pl.program_id(1)
    t = q_ref.shape[1]

    @pl.when(jj == 0)
    def _():
        qrows = _stack_heads(q_ref[0], nh)
        qrows_ref[...] = qrows
        tq_idx, c_idx = _row_query_index(nh * t, t, t)
        carry, acc = _stick_tile(qrows, kn_ref[0], vn_ref[0], _neg_suffix_matrix(t),
                                 jnp.zeros((nh * t, 1), F32), jnp.zeros(acc_ref.shape, F32),
                                 c_idx < tq_idx)
        carry_ref[...] = carry
        acc_ref[...] = acc

    @pl.when(jnp.max(carry_ref[...]) > EXP_ZERO_BELOW)
    def _():
        carry, acc = _stick_tile(qrows_ref[...], ck_ref[...].astype(BF16), cv_ref[...].astype(BF16),
                                 nsuf_ref[...], carry_ref[...], acc_ref[...], None, feature_major=True)
        carry_ref[...] = carry
        acc_ref[...] = acc

    @pl.when(jj == pl.num_programs(1) - 1)
    def _():
        o_ref[0] = _unstack_heads(acc_ref[...], nh).astype(o_ref.dtype)


def _feature_major(cache):
    layers, b, past, heads, hd = cache.shape
    return jnp.transpose(cache, (0, 1, 3, 4, 2)).reshape(layers, b, heads * hd, past)


def _stick_sample(q, kn, vn, cache_kt, cache_vt, layer, tk=1024):
    b, t, width = q.shape
    past = cache_kt.shape[3]
    tk = min(tk, past)
    nblk = past // tk
    nh = width // HEAD_DIM
    r = jnp.arange(tk)
    nsuf = jnp.where(r[:, None] >= r[None, :], -1.0, 0.0).astype(BF16)
    new_spec = pl.BlockSpec((1, t, width), lambda bi, jj: (bi, 0, 0))
    cache_spec = pl.BlockSpec((None, None, width, tk), lambda bi, jj: (layer, bi, 0, nblk - 1 - jj))
    return pl.pallas_call(
        functools.partial(_stick_sample_kernel, nh=nh),
        grid=(b, nblk),
        in_specs=[new_spec, new_spec, new_spec, cache_spec, cache_spec,
                  pl.BlockSpec((tk, tk), lambda bi, jj: (0, 0))],
        out_specs=pl.BlockSpec((1, t, width), lambda bi, jj: (bi, 0, 0)),
        out_shape=jax.ShapeDtypeStruct((b, t, width), BF16),
        scratch_shapes=[pltpu.VMEM((nh * t, width), BF16),
                        pltpu.VMEM((nh * t, 1), F32),
                        pltpu.VMEM((nh * t, width), F32)],
        compiler_params=_cparams("parallel", "arbitrary"),
        name="stick_sample",
    )(q, kn, vn, cache_kt, cache_vt, nsuf)


def _band_prompt_kernel(q_ref, k0_ref, k1_ref, k2_ref, v0_ref, v1_ref, v2_ref, bias_ref, o_ref,
                        *, t, nprev, fixed_shift):
    i = pl.program_id(1)
    nh = HEADS_PER_SLAB
    n_slabs = q_ref.shape[1] // LANES
    q = q_ref[...]
    krefs = (k0_ref, k1_ref, k2_ref)[3 - (nprev + 1):]
    vrefs = (v0_ref, v1_ref, v2_ref)[3 - (nprev + 1):]
    kcat = jnp.concatenate([r[...] for r in krefs], axis=0)
    vcat = jnp.concatenate([r[...] for r in vrefs], axis=0)
    outs = []
    for s in range(n_slabs):
        lanes = slice(s * LANES, (s + 1) * LANES)
        qrows = _stack_heads(q[:, lanes], nh)
        logits = _dot_nt(qrows, kcat[:, lanes]) + bias_ref[s]
        c_idx = lax.broadcasted_iota(jnp.int32, logits.shape, 1)
        logits = jnp.where(c_idx >= (nprev - i) * t, logits, MASKED)
        vs = vcat[:, lanes]
        if fixed_shift:
            v_ones = jnp.concatenate([vs, jnp.ones(vs.shape, BF16)], axis=1)
            acc = _dot(jnp.exp(logits).astype(BF16), v_ones)
            acc = acc[:, :LANES] / acc[:, LANES:]
        else:
            m = jnp.max(logits, axis=1, keepdims=True)
            p = jnp.exp(logits - m)
            l = jnp.sum(p, axis=1, keepdims=True)
            acc = _dot(p.astype(BF16), vs) / l
        outs.append(_unstack_heads(acc, nh))
    o_ref[...] = (outs[0] if n_slabs == 1 else jnp.concatenate(outs, axis=1)).astype(o_ref.dtype)


def _rel_bias(rel_table, nq, nk, q0, k0):
    m_len = nq + nk - 1
    u = np.arange(m_len)
    d = np.where(u < nk, (q0 - k0) - u, (q0 - k0) + (m_len - u))
    g = jnp.take(rel_table.astype(F32), np.clip(d, -REL_CLIP, REL_CLIP) + REL_CLIP, axis=1)
    n_heads = g.shape[0]
    bias = jnp.tile(g, (1, nq))[:, :nq * (m_len - 1)].reshape(n_heads, nq, m_len - 1)[:, :, :nk]
    qp = q0 + np.arange(nq)[:, None]
    kp = k0 + np.arange(nk)[None, :]
    qc, kc = qp // CHUNK, kp // CHUNK
    visible = (kp >= 0) & (kc <= qc) & (kc >= qc - BAND_CHUNKS)
    return bias, jnp.asarray(visible)[None]


def _band_prompt(q, k, v, rel_table, n, logit_bound, t=256):
    width = q.shape[1]
    t = min(t, n)
    assert t % CHUNK == 0
    nprev = min(-(-BAND // t), 2)
    assert nprev * t >= BAND
    slabs = width // LANES
    bias, visible = _rel_bias(rel_table, t, (nprev + 1) * t, nprev * t, 0)
    bias_shape = (slabs, HEADS_PER_SLAB * t, (nprev + 1) * t)
    per_step = 2 if slabs % 2 == 0 else 1
    group = per_step * LANES

    def kv_spec(back):
        return pl.BlockSpec((t, group), lambda p, i: (jnp.maximum(i - back, 0), p))

    def call(masked_bias, fixed_shift):
        return pl.pallas_call(
            functools.partial(_band_prompt_kernel, t=t, nprev=nprev, fixed_shift=fixed_shift),
            grid=(slabs // per_step, n // t),
            in_specs=[pl.BlockSpec((t, group), lambda p, i: (i, p)),
                      kv_spec(2), kv_spec(1), kv_spec(0), kv_spec(2), kv_spec(1), kv_spec(0),
                      pl.BlockSpec((per_step,) + bias_shape[1:], lambda p, i: (p, 0, 0))],
            out_specs=pl.BlockSpec((t, group), lambda p, i: (i, p)),
            out_shape=jax.ShapeDtypeStruct((n, width), BF16),
            compiler_params=_cparams("parallel", "parallel"),
            name="band_prompt" if fixed_shift else "band_prompt_row_max",
        )(q, k, k, k, v, v, v, masked_bias.reshape(bias_shape))

    tab_max, tab_min = jnp.max(rel_table), jnp.min(rel_table)
    shift = logit_bound + tab_max
    spread = 2.0 * logit_bound + (tab_max - tab_min)
    return lax.cond(spread < 2.0 * MAX_FIXED_SHIFT,
                    lambda: call(jnp.where(visible, bias - shift, MASKED), True),
                    lambda: call(jnp.where(visible, bias, MASKED), False))


def _band_sample_kernel(q_ref, kn_ref, vn_ref, ck_ref, cv_ref, bc_ref, bn_ref, o_ref, *, nh):
    qrows = _stack_heads(q_ref[0], nh)
    lc = _dot(qrows, ck_ref[...].astype(BF16)) + bc_ref[...]
    ln = _dot_nt(qrows, kn_ref[0]) + bn_ref[...]
    m = jnp.maximum(jnp.max(lc, axis=1, keepdims=True), jnp.max(ln, axis=1, keepdims=True))
    pc = jnp.exp(lc - m)
    pn = jnp.exp(ln - m)
    l = jnp.sum(pc, axis=1, keepdims=True) + jnp.sum(pn, axis=1, keepdims=True)
    acc = (_dot_nt(pc.astype(BF16), cv_ref[...].astype(BF16)) + _dot(pn.astype(BF16), vn_ref[0])) / l
    o_ref[0] = _unstack_heads(acc, nh).astype(o_ref.dtype)


def _band_sample(q, kn, vn, cache_kt, cache_vt, layer, rel_table, past_len):
    b, t, width = q.shape
    buf_len = cache_kt.shape[3]
    nh = width // HEAD_DIM
    def masked_bias(nk, k0):
        bias, visible = _rel_bias(rel_table, t, nk, past_len, k0)
        return jnp.where(visible, bias, MASKED).reshape(nh * t, nk)

    bias_c = masked_bias(buf_len, past_len - buf_len)
    bias_n = masked_bias(t, past_len)
    new_spec = pl.BlockSpec((1, t, width), lambda bi: (bi, 0, 0))
    cache_spec = pl.BlockSpec((None, None, width, buf_len), lambda bi: (layer, bi, 0, 0))
    return pl.pallas_call(
        functools.partial(_band_sample_kernel, nh=nh),
        grid=(b,),
        in_specs=[new_spec, new_spec, new_spec, cache_spec, cache_spec,
                  _full_spec(bias_c.shape), _full_spec(bias_n.shape)],
        out_specs=pl.BlockSpec((1, t, width), lambda bi: (bi, 0, 0)),
        out_shape=jax.ShapeDtypeStruct((b, t, width), BF16),
        compiler_params=_cparams("parallel"),
        name="band_sample",
    )(q, kn, vn, cache_kt, cache_vt, bias_c, bias_n)


def _key_decay_rows(dk, t):
    n_heads, tk = dk.shape
    return jnp.concatenate([jnp.broadcast_to(dk[h:h + 1, :], (t, tk)) for h in range(n_heads)], axis=0)


def _fox_prompt_kernel(q_ref, dq_ref, k_ref, v_ref, dk_ref, o_ref, m_ref, l_ref, acc_ref, *, t):
    i = pl.program_id(1)
    nh = HEADS_PER_SLAB
    qrows = _stack_heads(q_ref[...], nh)
    dq = jnp.concatenate([dq_ref[:, h:h + 1] for h in range(nh)], axis=0)
    m_ref[...] = jnp.full(m_ref.shape, MASKED, F32)
    l_ref[...] = jnp.zeros(l_ref.shape, F32)
    acc_ref[...] = jnp.zeros(acc_ref.shape, F32)

    def tile(j, mask):
        s = pl.multiple_of(j * t, t)
        z = _dot_nt(qrows, k_ref[pl.ds(s, t), :])
        logits = (z + dq) - _key_decay_rows(dk_ref[:, pl.ds(s, t)], t)
        if mask is not None:
            logits = jnp.where(mask, logits, MASKED)
        m, l, acc = _softmax_tile(logits, v_ref[pl.ds(s, t), :], m_ref[...], l_ref[...], acc_ref[...])
        m_ref[...] = m
        l_ref[...] = l
        acc_ref[...] = acc

    def body(j, carry):
        tile(j, None)
        return carry

    lax.fori_loop(0, i, body, 0)
    tq_idx, c_idx = _row_query_index(nh * t, t, t)
    tile(i, c_idx <= tq_idx)
    o_ref[...] = _unstack_heads(acc_ref[...] / l_ref[...], nh).astype(o_ref.dtype)


def _fox_prompt_bounded_kernel(shift_ref, first_ref, q_ref, dq_ref, k_ref, v_ref, dk_ref, o_ref,
                               dqrep_ref, acc_ref, z_ref, *, tq, tk):
    slab = pl.program_id(0)
    i = pl.program_id(1)
    ratio = tq // tk
    q = q_ref[...]
    ones = jnp.ones((tk, LANES), BF16)
    row = lax.broadcasted_iota(jnp.int32, (tq, LANES), 0)
    lane = lax.broadcasted_iota(jnp.int32, (tq, LANES), 1)
    outs = []
    for hh in range(HEADS_PER_SLAB):
        qm = jnp.where((lane >= hh * HEAD_DIM) & (lane < (hh + 1) * HEAD_DIM), q, jnp.zeros_like(q))
        dqrep_ref[...] = jnp.broadcast_to(dq_ref[:, hh:hh + 1] - shift_ref[0, 0], (tq, LANES))
        acc_ref[...] = jnp.zeros(acc_ref.shape, F32)

        def scores(j, r0, qm=qm):
            s = pl.multiple_of(j * tk, tk)
            return _dot_nt(qm[r0:], k_ref[pl.ds(s, tk), :])

        def tile(j, r0, causal, next_r0, hh=hh, scores=scores):
            s = pl.multiple_of(j * tk, tk)
            z = z_ref[r0:, :]
            dqh = dqrep_ref[r0:, :]
            cols = []
            for c in range(tk // LANES):
                lg = (z[:, c * LANES:(c + 1) * LANES] + dqh) - dk_ref[hh:hh + 1, pl.ds(s + c * LANES, LANES)]
                if causal:
                    key_pos = s + c * LANES + lax.broadcasted_iota(jnp.int32, lg.shape, 1)
                    query_pos = i * tq + r0 + lax.broadcasted_iota(jnp.int32, lg.shape, 0)
                    lg = jnp.where(key_pos <= query_pos, lg, MASKED)
                cols.append(jnp.exp(lg).astype(BF16))
            if next_r0 is not None:
                z_next = scores(j + 1, next_r0)
            v_ones = jnp.concatenate([v_ref[pl.ds(s, tk), :], ones], axis=1)
            acc_ref[r0:, :] += _dot(jnp.concatenate(cols, axis=1), v_ones)
            if next_r0 is not None:
                z_ref[next_r0:, :] = z_next

        start = first_ref[slab * HEADS_PER_SLAB + hh, i]
        z_ref[...] = scores(start, 0)

        count = i * ratio - start

        def run(j, n_tiles, tile=tile):
            for u in range(n_tiles):
                tile(j + u, 0, False, 0)

        @pl.when((count & 1) == 1)
        def _(run=run, start=start):
            run(start, 1)

        after_one = start + (count & 1)

        @pl.when((count & 2) == 2)
        def _(run=run, after_one=after_one):
            run(after_one, 2)

        def quad_body(jj, carry, run=run, first_quad=after_one + (count & 2)):
            run(first_quad + 4 * jj, 4)
            return carry

        lax.fori_loop(0, lax.shift_right_logical(count, 2), quad_body, 0)
        for d in range(ratio):
            tile(i * ratio + d, d * tk, True, (d + 1) * tk if d + 1 < ratio else None)
        acc = acc_ref[...]
        outs.append(acc[:, :LANES] / acc[:, LANES:])
    o_ref[...] = jnp.where(lane < HEAD_DIM, outs[0], outs[1]).astype(o_ref.dtype)


MAX_FIXED_SHIFT = 30.0


def _fox_prompt(q, k, v, cum, n, logit_bound, t_exact=256, tq_bounded=1024, tk_bounded=512):
    width = q.shape[1]
    slabs = width // LANES
    nh = HEADS_PER_SLAB
    dq = cum.reshape(n, slabs, nh).transpose(1, 0, 2)
    dk = cum.reshape(n, slabs, nh).transpose(1, 2, 0)

    def specs(t):
        return ([pl.BlockSpec((t, LANES), lambda p, i: (i, p)),
                 pl.BlockSpec((None, t, nh), lambda p, i: (p, i, 0)),
                 pl.BlockSpec((n, LANES), lambda p, i: (0, p)),
                 pl.BlockSpec((n, LANES), lambda p, i: (0, p)),
                 pl.BlockSpec((None, nh, n), lambda p, i: (p, 0, 0))],
                pl.BlockSpec((t, LANES), lambda p, i: (i, p)))

    def exact(_):
        t = min(t_exact, n)
        in_specs, out_spec = specs(t)
        return pl.pallas_call(
            functools.partial(_fox_prompt_kernel, t=t),
            grid=(slabs, n // t),
            in_specs=in_specs, out_specs=out_spec,
            out_shape=jax.ShapeDtypeStruct((n, width), BF16),
            scratch_shapes=[pltpu.VMEM((nh * t, 1), F32), pltpu.VMEM((nh * t, 1), F32),
                            pltpu.VMEM((nh * t, LANES), F32)],
            compiler_params=_cparams("parallel", "parallel"),
            name="fox_prompt_running_max",
        )(q, dq, k, v, dk)

    def bounded(shift):
        tq = min(tq_bounded, n)
        tk = min(tk_bounded, n)
        in_specs, out_spec = specs(tq)
        cum_t = cum.T
        decay = cum_t[:, ::tq][:, :, None] - cum_t[:, tk - 1::tk][:, None, :]
        before = jnp.arange(n // tk)[None, None, :] < (jnp.arange(n // tq) * (tq // tk))[None, :, None]
        dead = jnp.logical_and(decay < EXP_ZERO_BELOW, before)
        first = jnp.sum(jnp.cumprod(dead.astype(jnp.int32), axis=2), axis=2).astype(jnp.int32)
        smem = pl.BlockSpec(memory_space=pltpu.SMEM)
        return pl.pallas_call(
            functools.partial(_fox_prompt_bounded_kernel, tq=tq, tk=tk),
            grid=(slabs, n // tq),
            in_specs=[smem, smem] + in_specs, out_specs=out_spec,
            out_shape=jax.ShapeDtypeStruct((n, width), BF16),
            scratch_shapes=[pltpu.VMEM((tq, LANES), F32), pltpu.VMEM((tq, 2 * LANES), F32),
                            pltpu.VMEM((tq, tk), F32)],
            compiler_params=_cparams("parallel", "parallel"),
            name="fox_prompt",
        )(shift.reshape(1, 1), first, q, dq, k, v, dk)

    return lax.cond(logit_bound < MAX_FIXED_SHIFT, bounded, exact, logit_bound)


def _fox_sample_kernel(q_ref, dq_ref, kn_ref, vn_ref, dkn_ref, ck_ref, cv_ref, dkc_ref, o_ref,
                       qrows_ref, m_ref, l_ref, acc_ref, *, nh):
    jj = pl.program_id(1)
    t = q_ref.shape[1]
    dq = dq_ref[0]

    @pl.when(jj == 0)
    def _():
        qrows = _stack_heads(q_ref[0], nh)
        qrows_ref[...] = qrows
        tq_idx, c_idx = _row_query_index(nh * t, t, t)
        logits = (_dot_nt(qrows, kn_ref[0]) + dq) - _key_decay_rows(dkn_ref[0], t)
        logits = jnp.where(c_idx <= tq_idx, logits, MASKED)
        m, l, acc = _softmax_tile(logits, vn_ref[0], jnp.full(m_ref.shape, MASKED, F32),
                                  jnp.zeros(l_ref.shape, F32), jnp.zeros(acc_ref.shape, F32))
        m_ref[...] = m
        l_ref[...] = l
        acc_ref[...] = acc

    logits = (_dot(qrows_ref[...], ck_ref[...].astype(BF16)) + dq) - _key_decay_rows(dkc_ref[0], t)
    m, l, acc = _softmax_tile(logits, cv_ref[...].astype(BF16), m_ref[...], l_ref[...], acc_ref[...],
                              feature_major=True)
    m_ref[...] = m
    l_ref[...] = l
    acc_ref[...] = acc

    @pl.when(jj == pl.num_programs(1) - 1)
    def _():
        o_ref[0] = _unstack_heads(acc_ref[...] / l_ref[...], nh).astype(o_ref.dtype)


def _fox_sample(q, kn, vn, cache_kt, cache_vt, layer, cum_cache, cum_new, tk=1024):
    b, t, width = q.shape
    past = cache_kt.shape[3]
    tk = min(tk, past)
    nblk = past // tk
    nh = width // HEAD_DIM
    dq = cum_new.reshape(b, nh * t, 1)
    new_spec = pl.BlockSpec((1, t, width), lambda bi, jj: (bi, 0, 0))
    cache_spec = pl.BlockSpec((None, None, width, tk), lambda bi, jj: (layer, bi, 0, jj))
    return pl.pallas_call(
        functools.partial(_fox_sample_kernel, nh=nh),
        grid=(b, nblk),
        in_specs=[new_spec, pl.BlockSpec((1, nh * t, 1), lambda bi, jj: (bi, 0, 0)),
                  new_spec, new_spec, pl.BlockSpec((1, nh, t), lambda bi, jj: (bi, 0, 0)),
                  cache_spec, cache_spec, pl.BlockSpec((1, nh, tk), lambda bi, jj: (bi, 0, jj))],
        out_specs=pl.BlockSpec((1, t, width), lambda bi, jj: (bi, 0, 0)),
        out_shape=jax.ShapeDtypeStruct((b, t, width), BF16),
        scratch_shapes=[pltpu.VMEM((nh * t, width), BF16), pltpu.VMEM((nh * t, 1), F32),
                        pltpu.VMEM((nh * t, 1), F32), pltpu.VMEM((nh * t, width), F32)],
        compiler_params=_cparams("parallel", "arbitrary"),
        name="fox_sample",
    )(q, dq, kn, vn, cum_new, cache_kt, cache_vt, cum_cache)


def _pad_to(x, multiple, axis):
    size = x.shape[axis]
    target = -(-size // multiple) * multiple
    if target == size:
        return x
    pad = [(0, 0)] * x.ndim
    pad[axis] = (0, target - size)
    return jnp.pad(x, pad)


def kernel(x_prompt, x_sample, cache_a_k, cache_a_v, cache_b_k, cache_b_v, cache_c_k, cache_c_v, cache_c_logf, p_prompt, p_sample, g_mix, w_in_even, g_qb, g_kb, rel_bias, w_in_odd, b_forget, g_qc, g_kc, w_out, g_mlp, w_ff1, w_ff2, g_ple, w_ple_gate, w_ple_proj):
    bp, n, d = x_prompt.shape
    bs, ts, _ = x_sample.shape
    assert bp == 1, "prompt kernels assume a single prompt stream"
    depth = g_mix.shape[0]
    past = cache_a_k.shape[2]
    buf_len = cache_b_k.shape[2]
    ha, hb, hc = cache_a_k.shape[3], cache_b_k.shape[3], cache_c_k.shape[3]
    wa, wb, wc = ha * HEAD_DIM, hb * HEAD_DIM, hc * HEAD_DIM
    rows_s = bs * ts

    blk = jnp.arange(NORM_SLAB) // HEAD_DIM
    gmat = jnp.where(blk[:, None] == blk[None, :], 1.0 / HEAD_DIM, 0.0).astype(BF16)

    cache_a_kt, cache_a_vt = _feature_major(cache_a_k), _feature_major(cache_a_v)
    cache_b_kt, cache_b_vt = _feature_major(cache_b_k), _feature_major(cache_b_v)
    cache_c_kt, cache_c_vt = _feature_major(cache_c_k), _feature_major(cache_c_v)

    rows = n + rows_s
    tail = -rows % ROW_TILE
    h = jnp.concatenate([x_prompt.reshape(n, d), x_sample.reshape(rows_s, d), jnp.zeros((tail, d), F32)], axis=0)
    sample_pad = rows + tail - n
    d_ple = p_prompt.shape[-1]
    p_prompt2 = p_prompt.reshape(depth, n, d_ple)
    p_sample2 = _pad_to(p_sample.reshape(depth, rows_s, d_ple), sample_pad, 1)
    w_even16 = w_in_even.astype(BF16)
    w_odd16 = _pad_to(w_in_odd, LANES, 2).astype(BF16)
    w_out16, w_ff1_16, w_ff2_16 = w_out.astype(BF16), w_ff1.astype(BF16), w_ff2.astype(BF16)
    w_gate16, w_proj16 = w_ple_gate.astype(BF16), w_ple_proj.astype(BF16)

    def prompt_state(per_layer, heads, lo=0):
        xt = jnp.concatenate([x[:, :, lo:] for x in per_layer], axis=0)
        return jnp.transpose(xt.reshape(len(per_layer), heads, HEAD_DIM, n - lo), (0, 3, 1, 2))[:, None]

    def sample_state(xt, heads):
        return jnp.transpose(xt[:, :rows_s].reshape(heads, HEAD_DIM, bs, ts), (2, 3, 0, 1))

    st = {name: [] for name in ("pc_f", "sa_k", "sa_v", "sb_k", "sb_v", "sc_k", "sc_v", "sc_f")}
    even_states, odd_states = [], []
    for i in range(depth):
        j = i // 2
        g_row = g_mix[i].reshape(1, d)
        if i % 2 == 0:
            gq_row = jnp.tile(g_qb[j], hb).reshape(1, wb)
            gk_row = jnp.tile(g_kb[j], hb).reshape(1, wb)
            outs = _proj_even(h, g_row, w_even16, j, gq_row, gk_row, gmat, wa, wb, n)
            qa, ka16, va16, qb, kb16, vb16 = outs[:6]
            even_states.append(outs[6:10])
            ka_s, va_s, kb_s, vb_s = outs[10:]
            mix_a_p = _stick_prompt(qa, ka16, va16, n)
            band_bound = 1.01 * HEAD_DIM * SCALE * jnp.max(jnp.abs(g_qb[j])) * jnp.max(jnp.abs(g_kb[j]))
            mix_b_p = _band_prompt(qb, kb16, vb16, rel_bias[j], n, band_bound.astype(F32))
            s3 = lambda a: a[n:rows].reshape(bs, ts, a.shape[1])
            mix_a_s = _stick_sample(s3(qa), s3(ka16), s3(va16), cache_a_kt, cache_a_vt, j)
            mix_b_s = _band_sample(s3(qb), s3(kb16), s3(vb16), cache_b_kt, cache_b_vt, j, rel_bias[j], past)
            mix_prompt = [mix_a_p, mix_b_p]
            mix_sample = _pad_to(jnp.concatenate([mix_a_s.reshape(rows_s, wa), mix_b_s.reshape(rows_s, wb)],
                                                 axis=1), sample_pad, 0)
            st["sa_k"].append(sample_state(ka_s, ha))
            st["sa_v"].append(sample_state(va_s, ha))
            st["sb_k"].append(sample_state(kb_s, hb))
            st["sb_v"].append(sample_state(vb_s, hb))
        else:
            gq_row = jnp.tile(g_qc[j], hc).reshape(1, wc)
            gk_row = jnp.tile(g_kc[j], hc).reshape(1, wc)
            bf_row = _pad_to(b_forget[j].reshape(1, hc), LANES, 1)
            outs = _proj_odd(h, g_row, w_odd16, j, gq_row, gk_row, gmat, bf_row, wc, n)
            q, k16, v16, lf = outs[:4]
            odd_states.append(outs[4:6])
            k_s, v_s = outs[6:]
            log_f = lf[:, :hc]
            cum_p = _cumsum_lanes(log_f[:n].T)
            logit_bound = 1.01 * HEAD_DIM * SCALE * jnp.max(jnp.abs(g_qc[j])) * jnp.max(jnp.abs(g_kc[j]))
            mix_p = _fox_prompt(q, k16, v16, cum_p.T, n, logit_bound.astype(F32))
            lf_new = log_f[n:rows].reshape(bs, ts, hc).transpose(0, 2, 1)
            lf_all = jnp.concatenate([cache_c_logf[j].transpose(0, 2, 1), lf_new], axis=2)
            cum_s = _cumsum_lanes(lf_all.reshape(bs * hc, past + ts)).reshape(bs, hc, past + ts)
            s3 = lambda a: a[n:rows].reshape(bs, ts, a.shape[1])
            mix_s = _fox_sample(s3(q), s3(k16), s3(v16), cache_c_kt, cache_c_vt, j,
                                cum_s[:, :, :past], cum_s[:, :, past:past + ts])
            mix_prompt = [mix_p]
            mix_sample = _pad_to(mix_s.reshape(rows_s, wc), sample_pad, 0)
            st["pc_f"].append(log_f[:n].reshape(1, n, hc))
            st["sc_k"].append(sample_state(k_s, hc))
            st["sc_v"].append(sample_state(v_s, hc))
            st["sc_f"].append(log_f[n:rows].reshape(bs, ts, hc))
        h = _post(h, mix_prompt, mix_sample, n, i, w_out16, g_mlp[i].reshape(1, d), w_ff1_16, w_ff2_16,
                  g_ple[i].reshape(1, d), w_gate16, p_prompt2, p_sample2, w_proj16, split_output=i == depth - 1)

    y_prompt = h[0].reshape(1, n, d)
    y_sample = h[1][:rows_s].reshape(bs, ts, d)
    stk = {name: jnp.stack(vals) for name, vals in st.items()}
    assert buf_len >= ts
    stk["sb_k"] = jnp.concatenate([cache_b_k[:, :, ts:], stk["sb_k"]], axis=2)
    stk["sb_v"] = jnp.concatenate([cache_b_v[:, :, ts:], stk["sb_v"]], axis=2)
    keep = min(BAND, n)
    ka_l, va_l, kb_l, vb_l = zip(*even_states)
    kc_l, vc_l = zip(*odd_states)
    stk["pa_k"], stk["pa_v"] = prompt_state(ka_l, ha), prompt_state(va_l, ha)
    stk["pb_k"], stk["pb_v"] = prompt_state(kb_l, hb, n - keep), prompt_state(vb_l, hb, n - keep)
    stk["pc_k"], stk["pc_v"] = prompt_state(kc_l, hc), prompt_state(vc_l, hc)
    return (y_prompt, y_sample, stk["pa_k"], stk["pa_v"], stk["pb_k"], stk["pb_v"],
            stk["pc_k"], stk["pc_v"], stk["pc_f"], stk["sa_k"], stk["sa_v"], stk["sb_k"], stk["sb_v"],
            stk["sc_k"], stk["sc_v"], stk["sc_f"])
```

```python
import functools

import numpy as np
import jax
import jax.numpy as jnp
from jax import lax
from jax.experimental import pallas as pl
from jax.experimental.pallas import tpu as pltpu

F32 = jnp.float32
BF16 = jnp.bfloat16

HEAD_DIM = 64
CHUNK = 64
BAND_CHUNKS = 8
BAND = BAND_CHUNKS * CHUNK
REL_CLIP = 128
EPS = 1e-6
SCALE = HEAD_DIM ** -0.5
LANES = 128
HEADS_PER_SLAB = LANES // HEAD_DIM
NORM_SLAB = 256
EXP_ZERO_BELOW = -104.0
MASKED = -1e30
VMEM_LIMIT = 56 * 1024 * 1024


def _cparams(*sem):
    return pltpu.CompilerParams(dimension_semantics=sem, vmem_limit_bytes=VMEM_LIMIT)


def _split2(x):
    hi = x.astype(BF16)
    lo = (x - hi.astype(F32)).astype(BF16)
    return hi, lo


def _split3(x):
    hi = x.astype(BF16)
    r = x - hi.astype(F32)
    mid = r.astype(BF16)
    lo = (r - mid.astype(F32)).astype(BF16)
    return hi, mid, lo


def _dot(a, b):
    return jnp.dot(a, b, preferred_element_type=F32)


def _dot_nt(a, b):
    return lax.dot_general(a, b, (((1,), (1,)), ((), ())), preferred_element_type=F32)


def _dot_split2(x, m):
    hi, lo = _split2(x)
    return _dot(hi, m) + _dot(lo, m)


def _dot_split3(x, m):
    hi, mid, lo = _split3(x)
    return (_dot(hi, m) + _dot(mid, m)) + _dot(lo, m)


def _rms_rows(x, g):
    ms = jnp.mean(x * x, axis=-1, keepdims=True)
    return x * lax.rsqrt(ms + EPS) * g


def _head_rms(x, gmat, gain_row):
    width = x.shape[1]
    outs = []
    for s in range(width // NORM_SLAB):
        xs = x[:, NORM_SLAB * s:NORM_SLAB * (s + 1)]
        ms = _dot_split2(xs * xs, gmat)
        outs.append(xs * lax.rsqrt(ms + EPS))
    y = outs[0] if len(outs) == 1 else jnp.concatenate(outs, axis=1)
    return y * gain_row


def _softplus(z):
    return jnp.maximum(z, 0.0) + jnp.log(1.0 + jnp.exp(-jnp.abs(z)))


def _log_sigmoid(z):
    return jnp.minimum(z, 0.0) - jnp.log(1.0 + jnp.exp(-jnp.abs(z)))


def _store_state(values, prompt_refs, sample_refs, prompt_tiles):
    step = pl.program_id(0)

    @pl.when(step < prompt_tiles)
    def _():
        for value, ref in zip(values, prompt_refs):
            ref[...] = value.T

    @pl.when(step >= prompt_tiles)
    def _():
        for value, ref in zip(values, sample_refs):
            ref[...] = value.T


def _proj_even_kernel(h_ref, g_ref, w_ref, gq_ref, gk_ref, gmat_ref,
                      qa_ref, ka16_ref, va16_ref, qb_ref, kb16_ref, vb16_ref,
                      ka_p_ref, va_p_ref, kb_p_ref, vb_p_ref, ka_s_ref, va_s_ref, kb_s_ref, vb_s_ref,
                      *, wa, wb, prompt_tiles):
    xn = _rms_rows(h_ref[...], g_ref[...]).astype(BF16)
    proj = _dot(xn, w_ref[...])
    gmat = gmat_ref[...]
    qa = proj[:, 0:wa]
    ka = proj[:, wa:2 * wa]
    va = proj[:, 2 * wa:3 * wa]
    o = 3 * wa
    qb = _head_rms(proj[:, o:o + wb], gmat, gq_ref[...])
    kb = _head_rms(proj[:, o + wb:o + 2 * wb], gmat, gk_ref[...])
    vb = proj[:, o + 2 * wb:o + 3 * wb]
    qa_ref[...] = (qa * SCALE).astype(BF16)
    ka16_ref[...] = ka.astype(BF16)
    va16_ref[...] = va.astype(BF16)
    qb_ref[...] = (qb * SCALE).astype(BF16)
    kb16_ref[...] = kb.astype(BF16)
    vb16_ref[...] = vb.astype(BF16)
    _store_state([ka, va, kb, vb], [ka_p_ref, va_p_ref, kb_p_ref, vb_p_ref],
                 [ka_s_ref, va_s_ref, kb_s_ref, vb_s_ref], prompt_tiles)


def _proj_odd_kernel(h_ref, g_ref, w_ref, gq_ref, gk_ref, gmat_ref, bf_ref,
                     q_ref, k16_ref, v16_ref, lf_ref, k_p_ref, v_p_ref, k_s_ref, v_s_ref, *, wc, prompt_tiles):
    xn = _rms_rows(h_ref[...], g_ref[...]).astype(BF16)
    proj = _dot(xn, w_ref[...])
    gmat = gmat_ref[...]
    q = _head_rms(proj[:, 0:wc], gmat, gq_ref[...])
    k = _head_rms(proj[:, wc:2 * wc], gmat, gk_ref[...])
    v = proj[:, 2 * wc:3 * wc]
    f = proj[:, 3 * wc:3 * wc + LANES]
    q_ref[...] = (q * SCALE).astype(BF16)
    k16_ref[...] = k.astype(BF16)
    v16_ref[...] = v.astype(BF16)
    _store_state([k, v], [k_p_ref, v_p_ref], [k_s_ref, v_s_ref], prompt_tiles)
    lf_ref[...] = _log_sigmoid(f + bf_ref[...])


def _post1_kernel(*refs, n_mix, ff_chunk, prompt_tiles):
    h_ref, mix_p_refs, mix_s_ref = refs[0], refs[1:1 + n_mix], refs[1 + n_mix]
    wo_ref, g_ref, w1_ref, h1_ref, hid_ref = refs[2 + n_mix:]
    pieces = [r[...] for r in mix_p_refs]
    mix_p = pieces[0] if n_mix == 1 else jnp.concatenate(pieces, axis=1)
    mix = jnp.where(pl.program_id(0) < prompt_tiles, mix_p, mix_s_ref[...])
    h1 = h_ref[...] + _dot(mix, wo_ref[...])
    h1_ref[...] = h1
    xn = _rms_rows(h1, g_ref[...]).astype(BF16)
    d_ff = w1_ref.shape[1]
    for c in range(d_ff // ff_chunk):
        sl = slice(c * ff_chunk, (c + 1) * ff_chunk)
        a = jnp.maximum(_dot(xn, w1_ref[:, sl]), 0.0)
        hid_ref[:, sl] = (a * a).astype(BF16)


def _post2_kernel(h1_ref, hid_ref, w2_ref, g_ref, wg_ref, p_p_ref, p_s_ref, wp_ref, *out_refs, prompt_tiles):
    step = pl.program_id(0)
    h2 = h1_ref[...] + _dot(hid_ref[...], w2_ref[...])
    xg = _rms_rows(h2, g_ref[...]).astype(BF16)
    gate = jax.nn.sigmoid(_dot(xg, wg_ref[...]))
    p = jnp.where(step < prompt_tiles, p_p_ref[...], p_s_ref[...]).astype(BF16)
    out = h2 + gate * _dot(p, wp_ref[...])
    if len(out_refs) == 1:
        out_refs[0][...] = out
    else:
        @pl.when(step < prompt_tiles)
        def _():
            out_refs[0][...] = out

        @pl.when(step >= prompt_tiles)
        def _():
            out_refs[1][...] = out


def _row_spec(tm, width):
    return pl.BlockSpec((tm, width), lambda i: (i, 0))


def _full_spec(shape):
    return pl.BlockSpec(shape, lambda i: (0,) * len(shape))


ROW_TILE = 512


def _row_tile(rows):
    assert rows % ROW_TILE == 0, rows
    return ROW_TILE


def _proj_call(body, name, inputs, layer, n, row_widths, row_dtypes, state_widths):
    h = inputs[0]
    rows, d = h.shape
    tm = _row_tile(rows)
    assert n % tm == 0, (n, tm)
    prompt_tiles = n // tm
    prompt_spec = lambda wd: pl.BlockSpec((None, wd, tm), lambda i: (0, 0, jnp.minimum(i, prompt_tiles - 1)))
    sample_spec = lambda wd: pl.BlockSpec((wd, tm), lambda i: (0, jnp.maximum(i - prompt_tiles, 0)))
    return pl.pallas_call(
        functools.partial(body, prompt_tiles=prompt_tiles),
        grid=(rows // tm,),
        in_specs=[_row_spec(tm, d)] + [_layer_spec(x, layer) if x.ndim == 3 else _full_spec(x.shape)
                                       for x in inputs[1:]],
        out_specs=([_row_spec(tm, wd) for wd in row_widths] + [prompt_spec(wd) for wd in state_widths]
                   + [sample_spec(wd) for wd in state_widths]),
        out_shape=([jax.ShapeDtypeStruct((rows, wd), dt) for wd, dt in zip(row_widths, row_dtypes)]
                   + [jax.ShapeDtypeStruct((1, wd, n), F32) for wd in state_widths]
                   + [jax.ShapeDtypeStruct((wd, rows - n), F32) for wd in state_widths]),
        compiler_params=_cparams("arbitrary"),
        name=name,
    )(*inputs)


def _proj_even(h, g, w_stack, layer, gq_row, gk_row, gmat, wa, wb, n):
    return _proj_call(functools.partial(_proj_even_kernel, wa=wa, wb=wb), "proj_even",
                      (h, g, w_stack, gq_row, gk_row, gmat), layer, n,
                      [wa, wa, wa, wb, wb, wb], [BF16] * 6, [wa, wa, wb, wb])


def _proj_odd(h, g, w_stack, layer, gq_row, gk_row, gmat, bf_row, wc, n):
    return _proj_call(functools.partial(_proj_odd_kernel, wc=wc), "proj_odd",
                      (h, g, w_stack, gq_row, gk_row, gmat, bf_row), layer, n,
                      [wc, wc, wc, LANES], [BF16] * 3 + [F32], [wc, wc])


def _layer_spec(stacked, layer):
    return pl.BlockSpec((None,) + stacked.shape[1:], lambda i: (layer, 0, 0))


def _post(h, mix_prompt, mix_sample, n, layer, wo, g_mlp, w1, w2, g_ple, wg, p_prompt, p_sample, wp, split_output):
    rows, d = h.shape
    tm = _row_tile(rows)
    assert n % tm == 0, (n, tm)
    prompt_tiles = n // tm
    d_ff = w1.shape[2]
    d_ple = p_prompt.shape[2]
    ff_chunk = 1024 if d_ff % 1024 == 0 else d_ff
    prompt_rows = lambda wd: pl.BlockSpec((tm, wd), lambda i: (jnp.minimum(i, prompt_tiles - 1), 0))
    sample_rows = lambda wd: pl.BlockSpec((tm, wd), lambda i: (jnp.maximum(i - prompt_tiles, 0), 0))
    h1, hid = pl.pallas_call(
        functools.partial(_post1_kernel, n_mix=len(mix_prompt), ff_chunk=ff_chunk, prompt_tiles=prompt_tiles),
        grid=(rows // tm,),
        in_specs=([_row_spec(tm, d)] + [prompt_rows(m.shape[1]) for m in mix_prompt]
                  + [sample_rows(mix_sample.shape[1]), _layer_spec(wo, layer), _full_spec(g_mlp.shape),
                     _layer_spec(w1, layer)]),
        out_specs=[_row_spec(tm, d), _row_spec(tm, d_ff)],
        out_shape=[jax.ShapeDtypeStruct((rows, d), F32), jax.ShapeDtypeStruct((rows, d_ff), BF16)],
        compiler_params=_cparams("arbitrary"),
        name="post_attn_mlp_up",
    )(h, *mix_prompt, mix_sample, wo, g_mlp, w1)
    if split_output:
        out_specs = [prompt_rows(d), sample_rows(d)]
        out_shape = [jax.ShapeDtypeStruct((n, d), F32), jax.ShapeDtypeStruct((rows - n, d), F32)]
    else:
        out_specs = _row_spec(tm, d)
        out_shape = jax.ShapeDtypeStruct((rows, d), F32)
    return pl.pallas_call(
        functools.partial(_post2_kernel, prompt_tiles=prompt_tiles),
        grid=(rows // tm,),
        in_specs=[_row_spec(tm, d), _row_spec(tm, d_ff), _layer_spec(w2, layer), _full_spec(g_ple.shape),
                  _layer_spec(wg, layer),
                  pl.BlockSpec((None, tm, d_ple), lambda i: (layer, jnp.minimum(i, prompt_tiles - 1), 0)),
                  pl.BlockSpec((None, tm, d_ple), lambda i: (layer, jnp.maximum(i - prompt_tiles, 0), 0)),
                  _layer_spec(wp, layer)],
        out_specs=out_specs,
        out_shape=out_shape,
        compiler_params=_cparams("arbitrary"),
        name="mlp_down_ple",
    )(h1, hid, w2, g_ple, wg, p_prompt, p_sample, wp)


def _cumsum_kernel(x_ref, o_ref):
    gs, nb, _ = x_ref.shape
    x2 = x_ref[...].reshape(gs * nb, LANES)
    r = lax.broadcasted_iota(jnp.int32, (LANES, LANES), 0)
    c = lax.broadcasted_iota(jnp.int32, (LANES, LANES), 1)
    upper = jnp.where(r <= c, 1.0, 0.0).astype(BF16)
    ones = jnp.ones((LANES, LANES), BF16)
    within = _dot_split3(x2, upper).reshape(gs, nb, LANES)
    total = _dot_split3(x2, ones).reshape(gs, nb, LANES)
    rb = lax.broadcasted_iota(jnp.int32, (nb, nb), 0)
    cb = lax.broadcasted_iota(jnp.int32, (nb, nb), 1)
    strict_lower = jnp.where(cb < rb, 1.0, 0.0).astype(BF16)
    for g in range(gs):
        hi, mid, lo = _split3(total[g])
        offs = (_dot(strict_lower, hi) + _dot(strict_lower, mid)) + _dot(strict_lower, lo)
        o_ref[g] = within[g] + offs


def _cumsum_lanes(x):
    n_in = x.shape[1]
    x = _pad_to(x, LANES * LANES, 1)
    groups, n = x.shape
    nb = n // LANES
    gs = min(groups, 16)
    out = pl.pallas_call(
        _cumsum_kernel,
        grid=(groups // gs,),
        in_specs=[pl.BlockSpec((gs, nb, LANES), lambda i: (i, 0, 0))],
        out_specs=pl.BlockSpec((gs, nb, LANES), lambda i: (i, 0, 0)),
        out_shape=jax.ShapeDtypeStruct((groups, nb, LANES), F32),
        compiler_params=_cparams("parallel"),
        name="logf_cumsum",
    )(x.reshape(groups, nb, LANES))
    return out.reshape(groups, n)[:, :n_in]


def _neg_suffix_matrix(tk):
    r = lax.broadcasted_iota(jnp.int32, (tk, tk), 0)
    c = lax.broadcasted_iota(jnp.int32, (tk, tk), 1)
    return jnp.where(r >= c, -1.0, 0.0).astype(BF16)


def _stick_tile(qrows, kb, vb, nsuf, carry, acc, mask, feature_major=False):
    slabs = qrows if isinstance(qrows, (list, tuple)) else None
    if slabs is not None:
        z = jnp.concatenate([_dot_nt(qs, kb[:, s * LANES:(s + 1) * LANES]) for s, qs in enumerate(slabs)], axis=0)
    else:
        z = _dot(qrows, kb) if feature_major else _dot_nt(qrows, kb)
    sp = _softplus(z)
    if mask is not None:
        sp = jnp.where(mask, sp, 0.0)
    incl = _dot_split2(sp, nsuf)
    w = jnp.exp(z + incl + carry)
    if mask is not None:
        w = jnp.where(mask, w, 0.0)
    wb = w.astype(BF16)
    if slabs is not None:
        rows = slabs[0].shape[0]
        pv = jnp.concatenate([_dot(wb[s * rows:(s + 1) * rows], vb[:, s * LANES:(s + 1) * LANES])
                              for s in range(len(slabs))], axis=0)
    else:
        pv = _dot_nt(wb, vb) if feature_major else _dot(wb, vb)
    acc = acc + pv
    carry = carry + incl[:, 0:1]
    return carry, acc


def _softmax_tile(logits, vb, m, l, acc, feature_major=False):
    m_new = jnp.maximum(m, jnp.max(logits, axis=1, keepdims=True))
    alpha = jnp.exp(m - m_new)
    p = jnp.exp(logits - m_new)
    l = alpha * l + jnp.sum(p, axis=1, keepdims=True)
    pb = p.astype(BF16)
    acc = alpha * acc + (_dot_nt(pb, vb) if feature_major else _dot(pb, vb))
    return m_new, l, acc


def _stack_heads(q, n_heads):
    t, width = q.shape
    rep = jnp.concatenate([q] * n_heads, axis=0) if n_heads > 1 else q
    row = lax.broadcasted_iota(jnp.int32, (n_heads * t, width), 0)
    lane = lax.broadcasted_iota(jnp.int32, (n_heads * t, width), 1)
    lo = (row // t) * HEAD_DIM
    keep = (lane >= lo) & (lane < lo + HEAD_DIM)
    return jnp.where(keep, rep, jnp.zeros_like(rep))


def _unstack_heads(acc, n_heads):
    rows, width = acc.shape
    t = rows // n_heads
    lane = lax.broadcasted_iota(jnp.int32, (t, width), 1)
    out = jnp.zeros((t, width), acc.dtype)
    for h in range(n_heads):
        sel = (lane >= h * HEAD_DIM) & (lane < (h + 1) * HEAD_DIM)
        out = jnp.where(sel, acc[h * t:(h + 1) * t, :], out)
    return out


def _row_query_index(rows, cols, t):
    r = lax.broadcasted_iota(jnp.int32, (rows, cols), 0)
    c = lax.broadcasted_iota(jnp.int32, (rows, cols), 1)
    return r % t, c


def _stick_prompt_kernel(q_ref, k_ref, v_ref, o_ref, carry_ref, acc_ref, *, t):
    i = pl.program_id(1)
    nh = HEADS_PER_SLAB
    n_slabs = q_ref.shape[1] // LANES
    rows = n_slabs * nh * t
    q = q_ref[...]
    qrows = [_stack_heads(q[:, s * LANES:(s + 1) * LANES], nh) for s in range(n_slabs)]
    nsuf = _neg_suffix_matrix(t)
    tq_idx, c_idx = _row_query_index(rows, t, t)

    def block(j):
        s = pl.multiple_of(j * t, t)
        return k_ref[pl.ds(s, t), :], v_ref[pl.ds(s, t), :]

    def diagonal():
        return _stick_tile(qrows, *block(i), nsuf, jnp.zeros((rows, 1), F32),
                           jnp.zeros((rows, LANES), F32), c_idx < tq_idx)

    @pl.when(i == 0)
    def _():
        carry, acc = diagonal()
        carry_ref[...] = carry
        acc_ref[...] = acc

    @pl.when(i > 0)
    def _():
        carry, acc = diagonal()
        carry, acc = _stick_tile(qrows, *block(i - 1), nsuf, carry, acc, None)
        carry_ref[...] = carry
        acc_ref[...] = acc

    def cond(state):
        j, carry_max = state
        return jnp.logical_and(j >= 0, carry_max > EXP_ZERO_BELOW)

    def body(state):
        j, _ = state
        carry, acc = _stick_tile(qrows, *block(j), nsuf, carry_ref[...], acc_ref[...], None)
        carry_ref[...] = carry
        acc_ref[...] = acc
        return j - 1, jnp.max(carry)

    lax.while_loop(cond, body, (i - 2, jnp.max(carry_ref[...])))
    acc = acc_ref[...]
    outs = [_unstack_heads(acc[s * nh * t:(s + 1) * nh * t], nh) for s in range(n_slabs)]
    o_ref[...] = (outs[0] if n_slabs == 1 else jnp.concatenate(outs, axis=1)).astype(o_ref.dtype)


def _stick_prompt(q, k, v, n, t=256, slabs_per_step=2):
    width = q.shape[1]
    t = min(t, n)
    group = slabs_per_step * LANES
    assert width % group == 0
    rows = slabs_per_step * HEADS_PER_SLAB * t
    return pl.pallas_call(
        functools.partial(_stick_prompt_kernel, t=t),
        grid=(width // group, n // t),
        in_specs=[pl.BlockSpec((t, group), lambda p, i: (i, p)),
                  pl.BlockSpec((n, group), lambda p, i: (0, p)),
                  pl.BlockSpec((n, group), lambda p, i: (0, p))],
        out_specs=pl.BlockSpec((t, group), lambda p, i: (i, p)),
        out_shape=jax.ShapeDtypeStruct((n, width), BF16),
        scratch_shapes=[pltpu.VMEM((rows, 1), F32), pltpu.VMEM((rows, LANES), F32)],
        compiler_params=_cparams("parallel", "parallel"),
        name="stick_prompt",
    )(q, k, v)


def _stick_sample_kernel(q_ref, kn_ref, vn_ref, ck_ref, cv_ref, nsuf_ref, o_ref, qrows_ref, carry_ref, acc_ref,
                         *, nh):
    jj = pl.program_id(1)
    t = q_ref.shape[1]

    @pl.when(jj == 0)
    def _():
        qrows = _stack_heads(q_ref[0], nh)
        qrows_ref[...] = qrows
        tq_idx, c_idx = _row_query_index(nh * t, t, t)
        carry, acc = _stick_tile(qrows, kn_ref[0], vn_ref[0], _neg_suffix_matrix(t),
                                 jnp.zeros((nh * t, 1), F32), jnp.zeros(acc_ref.shape, F32),
                                 c_idx < tq_idx)
        carry_ref[...] = carry
        acc_ref[...] = acc

    @pl.when(jnp.max(carry_ref[...]) > EXP_ZERO_BELOW)
    def _():
        carry, acc = _stick_tile(qrows_ref[...], ck_ref[...].astype(BF16), cv_ref[...].astype(BF16),
                                 nsuf_ref[...], carry_ref[...], acc_ref[...], None, feature_major=True)
        carry_ref[...] = carry
        acc_ref[...] = acc

    @pl.when(jj == pl.num_programs(1) - 1)
    def _():
        o_ref[0] = _unstack_heads(acc_ref[...], nh).astype(o_ref.dtype)


def _feature_major(cache):
    layers, b, past, heads, hd = cache.shape
    return jnp.transpose(cache, (0, 1, 3, 4, 2)).reshape(layers, b, heads * hd, past)


def _stick_sample(q, kn, vn, cache_kt, cache_vt, layer, tk=1024):
    b, t, width = q.shape
    past = cache_kt.shape[3]
    tk = min(tk, past)
    nblk = past // tk
    nh = width // HEAD_DIM
    r = jnp.arange(tk)
    nsuf = jnp.where(r[:, None] >= r[None, :], -1.0, 0.0).astype(BF16)
    new_spec = pl.BlockSpec((1, t, width), lambda bi, jj: (bi, 0, 0))
    cache_spec = pl.BlockSpec((None, None, width, tk), lambda bi, jj: (layer, bi, 0, nblk - 1 - jj))
    return pl.pallas_call(
        functools.partial(_stick_sample_kernel, nh=nh),
        grid=(b, nblk),
        in_specs=[new_spec, new_spec, new_spec, cache_spec, cache_spec,
                  pl.BlockSpec((tk, tk), lambda bi, jj: (0, 0))],
        out_specs=pl.BlockSpec((1, t, width), lambda bi, jj: (bi, 0, 0)),
        out_shape=jax.ShapeDtypeStruct((b, t, width), BF16),
        scratch_shapes=[pltpu.VMEM((nh * t, width), BF16),
                        pltpu.VMEM((nh * t, 1), F32),
                        pltpu.VMEM((nh * t, width), F32)],
        compiler_params=_cparams("parallel", "arbitrary"),
        name="stick_sample",
    )(q, kn, vn, cache_kt, cache_vt, nsuf)


def _band_prompt_kernel(q_ref, k0_ref, k1_ref, k2_ref, v0_ref, v1_ref, v2_ref, bias_ref, o_ref,
                        *, t, nprev, fixed_shift):
    i = pl.program_id(1)
    nh = HEADS_PER_SLAB
    n_slabs = q_ref.shape[1] // LANES
    q = q_ref[...]
    krefs = (k0_ref, k1_ref, k2_ref)[3 - (nprev + 1):]
    vrefs = (v0_ref, v1_ref, v2_ref)[3 - (nprev + 1):]
    kcat = jnp.concatenate([r[...] for r in krefs], axis=0)
    vcat = jnp.concatenate([r[...] for r in vrefs], axis=0)
    outs = []
    for s in range(n_slabs):
        lanes = slice(s * LANES, (s + 1) * LANES)
        qrows = _stack_heads(q[:, lanes], nh)
        logits = _dot_nt(qrows, kcat[:, lanes]) + bias_ref[s]
        c_idx = lax.broadcasted_iota(jnp.int32, logits.shape, 1)
        logits = jnp.where(c_idx >= (nprev - i) * t, logits, MASKED)
        vs = vcat[:, lanes]
        if fixed_shift:
            v_ones = jnp.concatenate([vs, jnp.ones(vs.shape, BF16)], axis=1)
            acc = _dot(jnp.exp(logits).astype(BF16), v_ones)
            acc = acc[:, :LANES] / acc[:, LANES:]
        else:
            m = jnp.max(logits, axis=1, keepdims=True)
            p = jnp.exp(logits - m)
            l = jnp.sum(p, axis=1, keepdims=True)
            acc = _dot(p.astype(BF16), vs) / l
        outs.append(_unstack_heads(acc, nh))
    o_ref[...] = (outs[0] if n_slabs == 1 else jnp.concatenate(outs, axis=1)).astype(o_ref.dtype)


def _rel_bias(rel_table, nq, nk, q0, k0):
    m_len = nq + nk - 1
    u = np.arange(m_len)
    d = np.where(u < nk, (q0 - k0) - u, (q0 - k0) + (m_len - u))
    g = jnp.take(rel_table.astype(F32), np.clip(d, -REL_CLIP, REL_CLIP) + REL_CLIP, axis=1)
    n_heads = g.shape[0]
    bias = jnp.tile(g, (1, nq))[:, :nq * (m_len - 1)].reshape(n_heads, nq, m_len - 1)[:, :, :nk]
    qp = q0 + np.arange(nq)[:, None]
    kp = k0 + np.arange(nk)[None, :]
    qc, kc = qp // CHUNK, kp // CHUNK
    visible = (kp >= 0) & (kc <= qc) & (kc >= qc - BAND_CHUNKS)
    return bias, jnp.asarray(visible)[None]


def _band_prompt(q, k, v, rel_table, n, logit_bound, t=256):
    width = q.shape[1]
    t = min(t, n)
    assert t % CHUNK == 0
    nprev = min(-(-BAND // t), 2)
    assert nprev * t >= BAND
    slabs = width // LANES
    bias, visible = _rel_bias(rel_table, t, (nprev + 1) * t, nprev * t, 0)
    bias_shape = (slabs, HEADS_PER_SLAB * t, (nprev + 1) * t)
    per_step = next(c for c in (4, 2, 1) if slabs % c == 0)
    group = per_step * LANES

    def kv_spec(back):
        return pl.BlockSpec((t, group), lambda p, i: (jnp.maximum(i - back, 0), p))

    def call(masked_bias, fixed_shift):
        return pl.pallas_call(
            functools.partial(_band_prompt_kernel, t=t, nprev=nprev, fixed_shift=fixed_shift),
            grid=(slabs // per_step, n // t),
            in_specs=[pl.BlockSpec((t, group), lambda p, i: (i, p)),
                      kv_spec(2), kv_spec(1), kv_spec(0), kv_spec(2), kv_spec(1), kv_spec(0),
                      pl.BlockSpec((per_step,) + bias_shape[1:], lambda p, i: (p, 0, 0))],
            out_specs=pl.BlockSpec((t, group), lambda p, i: (i, p)),
            out_shape=jax.ShapeDtypeStruct((n, width), BF16),
            compiler_params=_cparams("parallel", "parallel"),
            name="band_prompt" if fixed_shift else "band_prompt_row_max",
        )(q, k, k, k, v, v, v, masked_bias.reshape(bias_shape))

    tab_max, tab_min = jnp.max(rel_table), jnp.min(rel_table)
    shift = logit_bound + tab_max
    spread = 2.0 * logit_bound + (tab_max - tab_min)
    return lax.cond(spread < 2.0 * MAX_FIXED_SHIFT,
                    lambda: call(jnp.where(visible, bias - shift, MASKED), True),
                    lambda: call(jnp.where(visible, bias, MASKED), False))


def _band_sample_kernel(q_ref, kn_ref, vn_ref, ck_ref, cv_ref, bc_ref, bn_ref, o_ref, *, nh):
    qrows = _stack_heads(q_ref[0], nh)
    lc = _dot(qrows, ck_ref[...].astype(BF16)) + bc_ref[...]
    ln = _dot_nt(qrows, kn_ref[0]) + bn_ref[...]
    m = jnp.maximum(jnp.max(lc, axis=1, keepdims=True), jnp.max(ln, axis=1, keepdims=True))
    pc = jnp.exp(lc - m)
    pn = jnp.exp(ln - m)
    l = jnp.sum(pc, axis=1, keepdims=True) + jnp.sum(pn, axis=1, keepdims=True)
    acc = (_dot_nt(pc.astype(BF16), cv_ref[...].astype(BF16)) + _dot(pn.astype(BF16), vn_ref[0])) / l
    o_ref[0] = _unstack_heads(acc, nh).astype(o_ref.dtype)


def _band_sample(q, kn, vn, cache_kt, cache_vt, layer, rel_table, past_len):
    b, t, width = q.shape
    buf_len = cache_kt.shape[3]
    nh = width // HEAD_DIM
    def masked_bias(nk, k0):
        bias, visible = _rel_bias(rel_table, t, nk, past_len, k0)
        return jnp.where(visible, bias, MASKED).reshape(nh * t, nk)

    bias_c = masked_bias(buf_len, past_len - buf_len)
    bias_n = masked_bias(t, past_len)
    new_spec = pl.BlockSpec((1, t, width), lambda bi: (bi, 0, 0))
    cache_spec = pl.BlockSpec((None, None, width, buf_len), lambda bi: (layer, bi, 0, 0))
    return pl.pallas_call(
        functools.partial(_band_sample_kernel, nh=nh),
        grid=(b,),
        in_specs=[new_spec, new_spec, new_spec, cache_spec, cache_spec,
                  _full_spec(bias_c.shape), _full_spec(bias_n.shape)],
        out_specs=pl.BlockSpec((1, t, width), lambda bi: (bi, 0, 0)),
        out_shape=jax.ShapeDtypeStruct((b, t, width), BF16),
        compiler_params=_cparams("parallel"),
        name="band_sample",
    )(q, kn, vn, cache_kt, cache_vt, bias_c, bias_n)


def _key_decay_rows(dk, t):
    n_heads, tk = dk.shape
    return jnp.concatenate([jnp.broadcast_to(dk[h:h + 1, :], (t, tk)) for h in range(n_heads)], axis=0)


def _fox_prompt_kernel(q_ref, dq_ref, k_ref, v_ref, dk_ref, o_ref, m_ref, l_ref, acc_ref, *, t):
    i = pl.program_id(1)
    nh = HEADS_PER_SLAB
    qrows = _stack_heads(q_ref[...], nh)
    dq = jnp.concatenate([dq_ref[:, h:h + 1] for h in range(nh)], axis=0)
    m_ref[...] = jnp.full(m_ref.shape, MASKED, F32)
    l_ref[...] = jnp.zeros(l_ref.shape, F32)
    acc_ref[...] = jnp.zeros(acc_ref.shape, F32)

    def tile(j, mask):
        s = pl.multiple_of(j * t, t)
        z = _dot_nt(qrows, k_ref[pl.ds(s, t), :])
        logits = (z + dq) - _key_decay_rows(dk_ref[:, pl.ds(s, t)], t)
        if mask is not None:
            logits = jnp.where(mask, logits, MASKED)
        m, l, acc = _softmax_tile(logits, v_ref[pl.ds(s, t), :], m_ref[...], l_ref[...], acc_ref[...])
        m_ref[...] = m
        l_ref[...] = l
        acc_ref[...] = acc

    def body(j, carry):
        tile(j, None)
        return carry

    lax.fori_loop(0, i, body, 0)
    tq_idx, c_idx = _row_query_index(nh * t, t, t)
    tile(i, c_idx <= tq_idx)
    o_ref[...] = _unstack_heads(acc_ref[...] / l_ref[...], nh).astype(o_ref.dtype)


def _fox_prompt_bounded_kernel(shift_ref, first_ref, q_ref, dq_ref, k_ref, v_ref, dk_ref, o_ref,
                               dqrep_ref, acc_ref, z_ref, *, tq, tk):
    slab = pl.program_id(0)
    i = pl.program_id(1)
    ratio = tq // tk
    q = q_ref[...]
    ones = jnp.ones((tk, LANES), BF16)
    row = lax.broadcasted_iota(jnp.int32, (tq, LANES), 0)
    lane = lax.broadcasted_iota(jnp.int32, (tq, LANES), 1)
    outs = []
    for hh in range(HEADS_PER_SLAB):
        qm = jnp.where((lane >= hh * HEAD_DIM) & (lane < (hh + 1) * HEAD_DIM), q, jnp.zeros_like(q))
        dqrep_ref[...] = jnp.broadcast_to(dq_ref[:, hh:hh + 1] - shift_ref[0, 0], (tq, LANES))
        acc_ref[...] = jnp.zeros(acc_ref.shape, F32)

        def scores(j, r0, qm=qm):
            s = pl.multiple_of(j * tk, tk)
            return _dot_nt(qm[r0:], k_ref[pl.ds(s, tk), :])

        def tile(j, r0, causal, next_r0, hh=hh, scores=scores):
            s = pl.multiple_of(j * tk, tk)
            z = z_ref[r0:, :]
            dqh = dqrep_ref[r0:, :]
            cols = []
            for c in range(tk // LANES):
                lg = (z[:, c * LANES:(c + 1) * LANES] + dqh) - dk_ref[hh:hh + 1, pl.ds(s + c * LANES, LANES)]
                if causal:
                    key_pos = s + c * LANES + lax.broadcasted_iota(jnp.int32, lg.shape, 1)
                    query_pos = i * tq + r0 + lax.broadcasted_iota(jnp.int32, lg.shape, 0)
                    lg = jnp.where(key_pos <= query_pos, lg, MASKED)
                cols.append(jnp.exp(lg).astype(BF16))
            if next_r0 is not None:
                z_next = scores(j + 1, next_r0)
            v_ones = jnp.concatenate([v_ref[pl.ds(s, tk), :], ones], axis=1)
            acc_ref[r0:, :] += _dot(jnp.concatenate(cols, axis=1), v_ones)
            if next_r0 is not None:
                z_ref[next_r0:, :] = z_next

        start = first_ref[slab * HEADS_PER_SLAB + hh, i]
        z_ref[...] = scores(start, 0)

        count = i * ratio - start

        def run(j, n_tiles, tile=tile):
            for u in range(n_tiles):
                tile(j + u, 0, False, 0)

        @pl.when((count & 1) == 1)
        def _(run=run, start=start):
            run(start, 1)

        after_one = start + (count & 1)

        @pl.when((count & 2) == 2)
        def _(run=run, after_one=after_one):
            run(after_one, 2)

        def quad_body(jj, carry, run=run, first_quad=after_one + (count & 2)):
            run(first_quad + 4 * jj, 4)
            return carry

        lax.fori_loop(0, lax.shift_right_logical(count, 2), quad_body, 0)
        for d in range(ratio):
            tile(i * ratio + d, d * tk, True, (d + 1) * tk if d + 1 < ratio else None)
        acc = acc_ref[...]
        outs.append(acc[:, :LANES] / acc[:, LANES:])
    o_ref[...] = jnp.where(lane < HEAD_DIM, outs[0], outs[1]).astype(o_ref.dtype)


MAX_FIXED_SHIFT = 30.0


def _fox_prompt(q, k, v, cum, n, logit_bound, t_exact=256, tq_bounded=1024, tk_bounded=512):
    width = q.shape[1]
    slabs = width // LANES
    nh = HEADS_PER_SLAB
    dq = cum.reshape(n, slabs, nh).transpose(1, 0, 2)
    dk = cum.reshape(n, slabs, nh).transpose(1, 2, 0)

    def specs(t):
        return ([pl.BlockSpec((t, LANES), lambda p, i: (i, p)),
                 pl.BlockSpec((None, t, nh), lambda p, i: (p, i, 0)),
                 pl.BlockSpec((n, LANES), lambda p, i: (0, p)),
                 pl.BlockSpec((n, LANES), lambda p, i: (0, p)),
                 pl.BlockSpec((None, nh, n), lambda p, i: (p, 0, 0))],
                pl.BlockSpec((t, LANES), lambda p, i: (i, p)))

    def exact(_):
        t = min(t_exact, n)
        in_specs, out_spec = specs(t)
        return pl.pallas_call(
            functools.partial(_fox_prompt_kernel, t=t),
            grid=(slabs, n // t),
            in_specs=in_specs, out_specs=out_spec,
            out_shape=jax.ShapeDtypeStruct((n, width), BF16),
            scratch_shapes=[pltpu.VMEM((nh * t, 1), F32), pltpu.VMEM((nh * t, 1), F32),
                            pltpu.VMEM((nh * t, LANES), F32)],
            compiler_params=_cparams("parallel", "parallel"),
            name="fox_prompt_running_max",
        )(q, dq, k, v, dk)

    def bounded(shift):
        tq = min(tq_bounded, n)
        tk = min(tk_bounded, n)
        in_specs, out_spec = specs(tq)
        cum_t = cum.T
        decay = cum_t[:, ::tq][:, :, None] - cum_t[:, tk - 1::tk][:, None, :]
        before = jnp.arange(n // tk)[None, None, :] < (jnp.arange(n // tq) * (tq // tk))[None, :, None]
        dead = jnp.logical_and(decay < EXP_ZERO_BELOW, before)
        first = jnp.sum(jnp.cumprod(dead.astype(jnp.int32), axis=2), axis=2).astype(jnp.int32)
        smem = pl.BlockSpec(memory_space=pltpu.SMEM)
        return pl.pallas_call(
            functools.partial(_fox_prompt_bounded_kernel, tq=tq, tk=tk),
            grid=(slabs, n // tq),
            in_specs=[smem, smem] + in_specs, out_specs=out_spec,
            out_shape=jax.ShapeDtypeStruct((n, width), BF16),
            scratch_shapes=[pltpu.VMEM((tq, LANES), F32), pltpu.VMEM((tq, 2 * LANES), F32),
                            pltpu.VMEM((tq, tk), F32)],
            compiler_params=_cparams("parallel", "parallel"),
            name="fox_prompt",
        )(shift.reshape(1, 1), first, q, dq, k, v, dk)

    return lax.cond(logit_bound < MAX_FIXED_SHIFT, bounded, exact, logit_bound)


def _fox_sample_kernel(q_ref, dq_ref, kn_ref, vn_ref, dkn_ref, ck_ref, cv_ref, dkc_ref, o_ref,
                       qrows_ref, m_ref, l_ref, acc_ref, *, nh):
    jj = pl.program_id(1)
    t = q_ref.shape[1]
    dq = dq_ref[0]

    @pl.when(jj == 0)
    def _():
        qrows = _stack_heads(q_ref[0], nh)
        qrows_ref[...] = qrows
        tq_idx, c_idx = _row_query_index(nh * t, t, t)
        logits = (_dot_nt(qrows, kn_ref[0]) + dq) - _key_decay_rows(dkn_ref[0], t)
        logits = jnp.where(c_idx <= tq_idx, logits, MASKED)
        m, l, acc = _softmax_tile(logits, vn_ref[0], jnp.full(m_ref.shape, MASKED, F32),
                                  jnp.zeros(l_ref.shape, F32), jnp.zeros(acc_ref.shape, F32))
        m_ref[...] = m
        l_ref[...] = l
        acc_ref[...] = acc

    logits = (_dot(qrows_ref[...], ck_ref[...].astype(BF16)) + dq) - _key_decay_rows(dkc_ref[0], t)
    m, l, acc = _softmax_tile(logits, cv_ref[...].astype(BF16), m_ref[...], l_ref[...], acc_ref[...],
                              feature_major=True)
    m_ref[...] = m
    l_ref[...] = l
    acc_ref[...] = acc

    @pl.when(jj == pl.num_programs(1) - 1)
    def _():
        o_ref[0] = _unstack_heads(acc_ref[...] / l_ref[...], nh).astype(o_ref.dtype)


def _fox_sample(q, kn, vn, cache_kt, cache_vt, layer, cum_cache, cum_new, tk=1024):
    b, t, width = q.shape
    past = cache_kt.shape[3]
    tk = min(tk, past)
    nblk = past // tk
    nh = width // HEAD_DIM
    dq = cum_new.reshape(b, nh * t, 1)
    new_spec = pl.BlockSpec((1, t, width), lambda bi, jj: (bi, 0, 0))
    cache_spec = pl.BlockSpec((None, None, width, tk), lambda bi, jj: (layer, bi, 0, jj))
    return pl.pallas_call(
        functools.partial(_fox_sample_kernel, nh=nh),
        grid=(b, nblk),
        in_specs=[new_spec, pl.BlockSpec((1, nh * t, 1), lambda bi, jj: (bi, 0, 0)),
                  new_spec, new_spec, pl.BlockSpec((1, nh, t), lambda bi, jj: (bi, 0, 0)),
                  cache_spec, cache_spec, pl.BlockSpec((1, nh, tk), lambda bi, jj: (bi, 0, jj))],
        out_specs=pl.BlockSpec((1, t, width), lambda bi, jj: (bi, 0, 0)),
        out_shape=jax.ShapeDtypeStruct((b, t, width), BF16),
        scratch_shapes=[pltpu.VMEM((nh * t, width), BF16), pltpu.VMEM((nh * t, 1), F32),
                        pltpu.VMEM((nh * t, 1), F32), pltpu.VMEM((nh * t, width), F32)],
        compiler_params=_cparams("parallel", "arbitrary"),
        name="fox_sample",
    )(q, dq, kn, vn, cum_new, cache_kt, cache_vt, cum_cache)


def _pad_to(x, multiple, axis):
    size = x.shape[axis]
    target = -(-size // multiple) * multiple
    if target == size:
        return x
    pad = [(0, 0)] * x.ndim
    pad[axis] = (0, target - size)
    return jnp.pad(x, pad)


def kernel(x_prompt, x_sample, cache_a_k, cache_a_v, cache_b_k, cache_b_v, cache_c_k, cache_c_v, cache_c_logf, p_prompt, p_sample, g_mix, w_in_even, g_qb, g_kb, rel_bias, w_in_odd, b_forget, g_qc, g_kc, w_out, g_mlp, w_ff1, w_ff2, g_ple, w_ple_gate, w_ple_proj):
    bp, n, d = x_prompt.shape
    bs, ts, _ = x_sample.shape
    assert bp == 1, "prompt kernels assume a single prompt stream"
    depth = g_mix.shape[0]
    past = cache_a_k.shape[2]
    buf_len = cache_b_k.shape[2]
    ha, hb, hc = cache_a_k.shape[3], cache_b_k.shape[3], cache_c_k.shape[3]
    wa, wb, wc = ha * HEAD_DIM, hb * HEAD_DIM, hc * HEAD_DIM
    rows_s = bs * ts

    blk = jnp.arange(NORM_SLAB) // HEAD_DIM
    gmat = jnp.where(blk[:, None] == blk[None, :], 1.0 / HEAD_DIM, 0.0).astype(BF16)

    cache_a_kt, cache_a_vt = _feature_major(cache_a_k), _feature_major(cache_a_v)
    cache_b_kt, cache_b_vt = _feature_major(cache_b_k), _feature_major(cache_b_v)
    cache_c_kt, cache_c_vt = _feature_major(cache_c_k), _feature_major(cache_c_v)

    rows = n + rows_s
    tail = -rows % ROW_TILE
    h = jnp.concatenate([x_prompt.reshape(n, d), x_sample.reshape(rows_s, d), jnp.zeros((tail, d), F32)], axis=0)
    sample_pad = rows + tail - n
    d_ple = p_prompt.shape[-1]
    p_prompt2 = p_prompt.reshape(depth, n, d_ple)
    p_sample2 = _pad_to(p_sample.reshape(depth, rows_s, d_ple), sample_pad, 1)
    w_even16 = w_in_even.astype(BF16)
    w_odd16 = _pad_to(w_in_odd, LANES, 2).astype(BF16)
    w_out16, w_ff1_16, w_ff2_16 = w_out.astype(BF16), w_ff1.astype(BF16), w_ff2.astype(BF16)
    w_gate16, w_proj16 = w_ple_gate.astype(BF16), w_ple_proj.astype(BF16)

    def prompt_state(per_layer, heads, lo=0):
        xt = jnp.concatenate([x[:, :, lo:] for x in per_layer], axis=0)
        return jnp.transpose(xt.reshape(len(per_layer), heads, HEAD_DIM, n - lo), (0, 3, 1, 2))[:, None]

    def sample_state(xt, heads):
        return jnp.transpose(xt[:, :rows_s].reshape(heads, HEAD_DIM, bs, ts), (2, 3, 0, 1))

    st = {name: [] for name in ("pc_f", "sa_k", "sa_v", "sb_k", "sb_v", "sc_k", "sc_v", "sc_f")}
    even_states, odd_states = [], []
    for i in range(depth):
        j = i // 2
        g_row = g_mix[i].reshape(1, d)
        if i % 2 == 0:
            gq_row = jnp.tile(g_qb[j], hb).reshape(1, wb)
            gk_row = jnp.tile(g_kb[j], hb).reshape(1, wb)
            outs = _proj_even(h, g_row, w_even16, j, gq_row, gk_row, gmat, wa, wb, n)
            qa, ka16, va16, qb, kb16, vb16 = outs[:6]
            even_states.append(outs[6:10])
            ka_s, va_s, kb_s, vb_s = outs[10:]
            mix_a_p = _stick_prompt(qa, ka16, va16, n)
            band_bound = 1.01 * HEAD_DIM * SCALE * jnp.max(jnp.abs(g_qb[j])) * jnp.max(jnp.abs(g_kb[j]))
            mix_b_p = _band_prompt(qb, kb16, vb16, rel_bias[j], n, band_bound.astype(F32))
            s3 = lambda a: a[n:rows].reshape(bs, ts, a.shape[1])
            mix_a_s = _stick_sample(s3(qa), s3(ka16), s3(va16), cache_a_kt, cache_a_vt, j)
            mix_b_s = _band_sample(s3(qb), s3(kb16), s3(vb16), cache_b_kt, cache_b_vt, j, rel_bias[j], past)
            mix_prompt = [mix_a_p, mix_b_p]
            mix_sample = _pad_to(jnp.concatenate([mix_a_s.reshape(rows_s, wa), mix_b_s.reshape(rows_s, wb)],
                                                 axis=1), sample_pad, 0)
            st["sa_k"].append(sample_state(ka_s, ha))
            st["sa_v"].append(sample_state(va_s, ha))
            st["sb_k"].append(sample_state(kb_s, hb))
            st["sb_v"].append(sample_state(vb_s, hb))
        else:
            gq_row = jnp.tile(g_qc[j], hc).reshape(1, wc)
            gk_row = jnp.tile(g_kc[j], hc).reshape(1, wc)
            bf_row = _pad_to(b_forget[j].reshape(1, hc), LANES, 1)
            outs = _proj_odd(h, g_row, w_odd16, j, gq_row, gk_row, gmat, bf_row, wc, n)
            q, k16, v16, lf = outs[:4]
            odd_states.append(outs[4:6])
            k_s, v_s = outs[6:]
            log_f = lf[:, :hc]
            cum_p = _cumsum_lanes(log_f[:n].T)
            logit_bound = 1.01 * HEAD_DIM * SCALE * jnp.max(jnp.abs(g_qc[j])) * jnp.max(jnp.abs(g_kc[j]))
            mix_p = _fox_prompt(q, k16, v16, cum_p.T, n, logit_bound.astype(F32))
            lf_new = log_f[n:rows].reshape(bs, ts, hc).transpose(0, 2, 1)
            lf_all = jnp.concatenate([cache_c_logf[j].transpose(0, 2, 1), lf_new], axis=2)
            cum_s = _cumsum_lanes(lf_all.reshape(bs * hc, past + ts)).reshape(bs, hc, past + ts)
            s3 = lambda a: a[n:rows].reshape(bs, ts, a.shape[1])
            mix_s = _fox_sample(s3(q), s3(k16), s3(v16), cache_c_kt, cache_c_vt, j,
                                cum_s[:, :, :past], cum_s[:, :, past:past + ts])
            mix_prompt = [mix_p]
            mix_sample = _pad_to(mix_s.reshape(rows_s, wc), sample_pad, 0)
            st["pc_f"].append(log_f[:n].reshape(1, n, hc))
            st["sc_k"].append(sample_state(k_s, hc))
            st["sc_v"].append(sample_state(v_s, hc))
            st["sc_f"].append(log_f[n:rows].reshape(bs, ts, hc))
        h = _post(h, mix_prompt, mix_sample, n, i, w_out16, g_mlp[i].reshape(1, d), w_ff1_16, w_ff2_16,
                  g_ple[i].reshape(1, d), w_gate16, p_prompt2, p_sample2, w_proj16, split_output=i == depth - 1)

    y_prompt = h[0].reshape(1, n, d)
    y_sample = h[1][:rows_s].reshape(bs, ts, d)
    stk = {name: jnp.stack(vals) for name, vals in st.items()}
    assert buf_len >= ts
    stk["sb_k"] = jnp.concatenate([cache_b_k[:, :, ts:], stk["sb_k"]], axis=2)
    stk["sb_v"] = jnp.concatenate([cache_b_v[:, :, ts:], stk["sb_v"]], axis=2)
    keep = min(BAND, n)
    ka_l, va_l, kb_l, vb_l = zip(*even_states)
    kc_l, vc_l = zip(*odd_states)
    stk["pa_k"], stk["pa_v"] = prompt_state(ka_l, ha), prompt_state(va_l, ha)
    stk["pb_k"], stk["pb_v"] = prompt_state(kb_l, hb, n - keep), prompt_state(vb_l, hb, n - keep)
    stk["pc_k"], stk["pc_v"] = prompt_state(kc_l, hc), prompt_state(vc_l, hc)
    return (y_prompt, y_sample, stk["pa_k"], stk["pa_v"], stk["pb_k"], stk["pb_v"],
            stk["pc_k"], stk["pc_v"], stk["pc_f"], stk["sa_k"], stk["sa_v"], stk["sb_k"], stk["sb_v"],
            stk["sc_k"], stk["sc_v"], stk["sc_f"])
```

```python
import functools

import numpy as np
import jax
import jax.numpy as jnp
from jax import lax
from jax.experimental import pallas as pl
from jax.experimental.pallas import tpu as pltpu

F32 = jnp.float32
BF16 = jnp.bfloat16

HEAD_DIM = 64
CHUNK = 64
BAND_CHUNKS = 8
BAND = BAND_CHUNKS * CHUNK
REL_CLIP = 128
EPS = 1e-6
SCALE = HEAD_DIM ** -0.5
LANES = 128
HEADS_PER_SLAB = LANES // HEAD_DIM
NORM_SLAB = 256
EXP_ZERO_BELOW = -104.0
MASKED = -1e30
VMEM_LIMIT = 56 * 1024 * 1024


def _cparams(*sem):
    return pltpu.CompilerParams(dimension_semantics=sem, vmem_limit_bytes=VMEM_LIMIT)


def _split2(x):
    hi = x.astype(BF16)
    lo = (x - hi.astype(F32)).astype(BF16)
    return hi, lo


def _split3(x):
    hi = x.astype(BF16)
    r = x - hi.astype(F32)
    mid = r.astype(BF16)
    lo = (r - mid.astype(F32)).astype(BF16)
    return hi, mid, lo


def _dot(a, b):
    return jnp.dot(a, b, preferred_element_type=F32)


def _dot_nt(a, b):
    return lax.dot_general(a, b, (((1,), (1,)), ((), ())), preferred_element_type=F32)


def _dot_split2(x, m):
    hi, lo = _split2(x)
    return _dot(hi, m) + _dot(lo, m)


def _dot_split3(x, m):
    hi, mid, lo = _split3(x)
    return (_dot(hi, m) + _dot(mid, m)) + _dot(lo, m)


def _rms_rows(x, g):
    ms = jnp.mean(x * x, axis=-1, keepdims=True)
    return x * lax.rsqrt(ms + EPS) * g


def _head_rms(x, gmat, gain_row):
    width = x.shape[1]
    outs = []
    for s in range(width // NORM_SLAB):
        xs = x[:, NORM_SLAB * s:NORM_SLAB * (s + 1)]
        ms = _dot_split2(xs * xs, gmat)
        outs.append(xs * lax.rsqrt(ms + EPS))
    y = outs[0] if len(outs) == 1 else jnp.concatenate(outs, axis=1)
    return y * gain_row


def _softplus(z):
    return jnp.maximum(z, 0.0) + jnp.log(1.0 + jnp.exp(-jnp.abs(z)))


def _log_sigmoid(z):
    return jnp.minimum(z, 0.0) - jnp.log(1.0 + jnp.exp(-jnp.abs(z)))


def _store_state(values, prompt_refs, sample_refs, prompt_tiles):
    step = pl.program_id(0)

    @pl.when(step < prompt_tiles)
    def _():
        for value, ref in zip(values, prompt_refs):
            ref[...] = value.T

    @pl.when(step >= prompt_tiles)
    def _():
        for value, ref in zip(values, sample_refs):
            ref[...] = value.T


def _proj_even_kernel(h_ref, g_ref, w_ref, gq_ref, gk_ref, gmat_ref,
                      qa_ref, ka16_ref, va16_ref, qb_ref, kb16_ref, vb16_ref,
                      ka_p_ref, va_p_ref, kb_p_ref, vb_p_ref, ka_s_ref, va_s_ref, kb_s_ref, vb_s_ref,
                      *, wa, wb, prompt_tiles):
    xn = _rms_rows(h_ref[...], g_ref[...]).astype(BF16)
    proj = _dot(xn, w_ref[...])
    gmat = gmat_ref[...]
    qa = proj[:, 0:wa]
    ka = proj[:, wa:2 * wa]
    va = proj[:, 2 * wa:3 * wa]
    o = 3 * wa
    qb = _head_rms(proj[:, o:o + wb], gmat, gq_ref[...])
    kb = _head_rms(proj[:, o + wb:o + 2 * wb], gmat, gk_ref[...])
    vb = proj[:, o + 2 * wb:o + 3 * wb]
    qa_ref[...] = (qa * SCALE).astype(BF16)
    ka16_ref[...] = ka.astype(BF16)
    va16_ref[...] = va.astype(BF16)
    qb_ref[...] = (qb * SCALE).astype(BF16)
    kb16_ref[...] = kb.astype(BF16)
    vb16_ref[...] = vb.astype(BF16)
    _store_state([ka, va, kb, vb], [ka_p_ref, va_p_ref, kb_p_ref, vb_p_ref],
                 [ka_s_ref, va_s_ref, kb_s_ref, vb_s_ref], prompt_tiles)


def _proj_odd_kernel(h_ref, g_ref, w_ref, gq_ref, gk_ref, gmat_ref, bf_ref,
                     q_ref, k16_ref, v16_ref, lf_ref, k_p_ref, v_p_ref, k_s_ref, v_s_ref, *, wc, prompt_tiles):
    xn = _rms_rows(h_ref[...], g_ref[...]).astype(BF16)
    proj = _dot(xn, w_ref[...])
    gmat = gmat_ref[...]
    q = _head_rms(proj[:, 0:wc], gmat, gq_ref[...])
    k = _head_rms(proj[:, wc:2 * wc], gmat, gk_ref[...])
    v = proj[:, 2 * wc:3 * wc]
    f = proj[:, 3 * wc:3 * wc + LANES]
    q_ref[...] = (q * SCALE).astype(BF16)
    k16_ref[...] = k.astype(BF16)
    v16_ref[...] = v.astype(BF16)
    _store_state([k, v], [k_p_ref, v_p_ref], [k_s_ref, v_s_ref], prompt_tiles)
    lf_ref[...] = _log_sigmoid(f + bf_ref[...])


def _post1_kernel(*refs, n_mix, ff_chunk, prompt_tiles):
    h_ref, mix_p_refs, mix_s_ref = refs[0], refs[1:1 + n_mix], refs[1 + n_mix]
    wo_ref, g_ref, w1_ref, h1_ref, hid_ref = refs[2 + n_mix:]
    pieces = [r[...] for r in mix_p_refs]
    mix_p = pieces[0] if n_mix == 1 else jnp.concatenate(pieces, axis=1)
    mix = jnp.where(pl.program_id(0) < prompt_tiles, mix_p, mix_s_ref[...])
    h1 = h_ref[...] + _dot(mix, wo_ref[...])
    h1_ref[...] = h1
    xn = _rms_rows(h1, g_ref[...]).astype(BF16)
    d_ff = w1_ref.shape[1]
    for c in range(d_ff // ff_chunk):
        sl = slice(c * ff_chunk, (c + 1) * ff_chunk)
        a = jnp.maximum(_dot(xn, w1_ref[:, sl]), 0.0)
        hid_ref[:, sl] = (a * a).astype(BF16)


def _post2_kernel(h1_ref, hid_ref, w2_ref, g_ref, wg_ref, p_p_ref, p_s_ref, wp_ref, *out_refs, prompt_tiles):
    step = pl.program_id(0)
    h2 = h1_ref[...] + _dot(hid_ref[...], w2_ref[...])
    xg = _rms_rows(h2, g_ref[...]).astype(BF16)
    gate = jax.nn.sigmoid(_dot(xg, wg_ref[...]))
    p = jnp.where(step < prompt_tiles, p_p_ref[...], p_s_ref[...]).astype(BF16)
    out = h2 + gate * _dot(p, wp_ref[...])
    if len(out_refs) == 1:
        out_refs[0][...] = out
    else:
        @pl.when(step < prompt_tiles)
        def _():
            out_refs[0][...] = out

        @pl.when(step >= prompt_tiles)
        def _():
            out_refs[1][...] = out


def _row_spec(tm, width):
    return pl.BlockSpec((tm, width), lambda i: (i, 0))


def _full_spec(shape):
    return pl.BlockSpec(shape, lambda i: (0,) * len(shape))


ROW_TILE = 512


def _row_tile(rows):
    assert rows % ROW_TILE == 0, rows
    return ROW_TILE


def _proj_call(body, name, inputs, layer, n, row_widths, row_dtypes, state_widths):
    h = inputs[0]
    rows, d = h.shape
    tm = _row_tile(rows)
    assert n % tm == 0, (n, tm)
    prompt_tiles = n // tm
    prompt_spec = lambda wd: pl.BlockSpec((None, wd, tm), lambda i: (0, 0, jnp.minimum(i, prompt_tiles - 1)))
    sample_spec = lambda wd: pl.BlockSpec((wd, tm), lambda i: (0, jnp.maximum(i - prompt_tiles, 0)))
    return pl.pallas_call(
        functools.partial(body, prompt_tiles=prompt_tiles),
        grid=(rows // tm,),
        in_specs=[_row_spec(tm, d)] + [_layer_spec(x, layer) if x.ndim == 3 else _full_spec(x.shape)
                                       for x in inputs[1:]],
        out_specs=([_row_spec(tm, wd) for wd in row_widths] + [prompt_spec(wd) for wd in state_widths]
                   + [sample_spec(wd) for wd in state_widths]),
        out_shape=([jax.ShapeDtypeStruct((rows, wd), dt) for wd, dt in zip(row_widths, row_dtypes)]
                   + [jax.ShapeDtypeStruct((1, wd, n), F32) for wd in state_widths]
                   + [jax.ShapeDtypeStruct((wd, rows - n), F32) for wd in state_widths]),
        compiler_params=_cparams("arbitrary"),
        name=name,
    )(*inputs)


def _proj_even(h, g, w_stack, layer, gq_row, gk_row, gmat, wa, wb, n):
    return _proj_call(functools.partial(_proj_even_kernel, wa=wa, wb=wb), "proj_even",
                      (h, g, w_stack, gq_row, gk_row, gmat), layer, n,
                      [wa, wa, wa, wb, wb, wb], [BF16] * 6, [wa, wa, wb, wb])


def _proj_odd(h, g, w_stack, layer, gq_row, gk_row, gmat, bf_row, wc, n):
    return _proj_call(functools.partial(_proj_odd_kernel, wc=wc), "proj_odd",
                      (h, g, w_stack, gq_row, gk_row, gmat, bf_row), layer, n,
                      [wc, wc, wc, LANES], [BF16] * 3 + [F32], [wc, wc])


def _layer_spec(stacked, layer):
    return pl.BlockSpec((None,) + stacked.shape[1:], lambda i: (layer, 0, 0))


def _post(h, mix_prompt, mix_sample, n, layer, wo, g_mlp, w1, w2, g_ple, wg, p_prompt, p_sample, wp, split_output):
    rows, d = h.shape
    tm = _row_tile(rows)
    assert n % tm == 0, (n, tm)
    prompt_tiles = n // tm
    d_ff = w1.shape[2]
    d_ple = p_prompt.shape[2]
    ff_chunk = 1024 if d_ff % 1024 == 0 else d_ff
    prompt_rows = lambda wd: pl.BlockSpec((tm, wd), lambda i: (jnp.minimum(i, prompt_tiles - 1), 0))
    sample_rows = lambda wd: pl.BlockSpec((tm, wd), lambda i: (jnp.maximum(i - prompt_tiles, 0), 0))
    h1, hid = pl.pallas_call(
        functools.partial(_post1_kernel, n_mix=len(mix_prompt), ff_chunk=ff_chunk, prompt_tiles=prompt_tiles),
        grid=(rows // tm,),
        in_specs=([_row_spec(tm, d)] + [prompt_rows(m.shape[1]) for m in mix_prompt]
                  + [sample_rows(mix_sample.shape[1]), _layer_spec(wo, layer), _full_spec(g_mlp.shape),
                     _layer_spec(w1, layer)]),
        out_specs=[_row_spec(tm, d), _row_spec(tm, d_ff)],
        out_shape=[jax.ShapeDtypeStruct((rows, d), F32), jax.ShapeDtypeStruct((rows, d_ff), BF16)],
        compiler_params=_cparams("arbitrary"),
        name="post_attn_mlp_up",
    )(h, *mix_prompt, mix_sample, wo, g_mlp, w1)
    if split_output:
        out_specs = [prompt_rows(d), sample_rows(d)]
        out_shape = [jax.ShapeDtypeStruct((n, d), F32), jax.ShapeDtypeStruct((rows - n, d), F32)]
    else:
        out_specs = _row_spec(tm, d)
        out_shape = jax.ShapeDtypeStruct((rows, d), F32)
    return pl.pallas_call(
        functools.partial(_post2_kernel, prompt_tiles=prompt_tiles),
        grid=(rows // tm,),
        in_specs=[_row_spec(tm, d), _row_spec(tm, d_ff), _layer_spec(w2, layer), _full_spec(g_ple.shape),
                  _layer_spec(wg, layer),
                  pl.BlockSpec((None, tm, d_ple), lambda i: (layer, jnp.minimum(i, prompt_tiles - 1), 0)),
                  pl.BlockSpec((None, tm, d_ple), lambda i: (layer, jnp.maximum(i - prompt_tiles, 0), 0)),
                  _layer_spec(wp, layer)],
        out_specs=out_specs,
        out_shape=out_shape,
        compiler_params=_cparams("arbitrary"),
        name="mlp_down_ple",
    )(h1, hid, w2, g_ple, wg, p_prompt, p_sample, wp)


def _cumsum_kernel(x_ref, o_ref):
    gs, nb, _ = x_ref.shape
    x2 = x_ref[...].reshape(gs * nb, LANES)
    r = lax.broadcasted_iota(jnp.int32, (LANES, LANES), 0)
    c = lax.broadcasted_iota(jnp.int32, (LANES, LANES), 1)
    upper = jnp.where(r <= c, 1.0, 0.0).astype(BF16)
    ones = jnp.ones((LANES, LANES), BF16)
    within = _dot_split3(x2, upper).reshape(gs, nb, LANES)
    total = _dot_split3(x2, ones).reshape(gs, nb, LANES)
    rb = lax.broadcasted_iota(jnp.int32, (nb, nb), 0)
    cb = lax.broadcasted_iota(jnp.int32, (nb, nb), 1)
    strict_lower = jnp.where(cb < rb, 1.0, 0.0).astype(BF16)
    for g in range(gs):
        hi, mid, lo = _split3(total[g])
        offs = (_dot(strict_lower, hi) + _dot(strict_lower, mid)) + _dot(strict_lower, lo)
        o_ref[g] = within[g] + offs


def _cumsum_lanes(x):
    n_in = x.shape[1]
    x = _pad_to(x, LANES * LANES, 1)
    groups, n = x.shape
    nb = n // LANES
    gs = min(groups, 16)
    out = pl.pallas_call(
        _cumsum_kernel,
        grid=(groups // gs,),
        in_specs=[pl.BlockSpec((gs, nb, LANES), lambda i: (i, 0, 0))],
        out_specs=pl.BlockSpec((gs, nb, LANES), lambda i: (i, 0, 0)),
        out_shape=jax.ShapeDtypeStruct((groups, nb, LANES), F32),
        compiler_params=_cparams("parallel"),
        name="logf_cumsum",
    )(x.reshape(groups, nb, LANES))
    return out.reshape(groups, n)[:, :n_in]


def _neg_suffix_matrix(tk):
    r = lax.broadcasted_iota(jnp.int32, (tk, tk), 0)
    c = lax.broadcasted_iota(jnp.int32, (tk, tk), 1)
    return jnp.where(r >= c, -1.0, 0.0).astype(BF16)


def _stick_tile(qrows, kb, vb, nsuf, carry, acc, mask, feature_major=False):
    slabs = qrows if isinstance(qrows, (list, tuple)) else None
    if slabs is not None:
        z = jnp.concatenate([_dot_nt(qs, kb[:, s * LANES:(s + 1) * LANES]) for s, qs in enumerate(slabs)], axis=0)
    else:
        z = _dot(qrows, kb) if feature_major else _dot_nt(qrows, kb)
    sp = _softplus(z)
    if mask is not None:
        sp = jnp.where(mask, sp, 0.0)
    incl = _dot_split2(sp, nsuf)
    w = jnp.exp(z + incl + carry)
    if mask is not None:
        w = jnp.where(mask, w, 0.0)
    wb = w.astype(BF16)
    if slabs is not None:
        rows = slabs[0].shape[0]
        pv = jnp.concatenate([_dot(wb[s * rows:(s + 1) * rows], vb[:, s * LANES:(s + 1) * LANES])
                              for s in range(len(slabs))], axis=0)
    else:
        pv = _dot_nt(wb, vb) if feature_major else _dot(wb, vb)
    acc = acc + pv
    carry = carry + incl[:, 0:1]
    return carry, acc


def _softmax_tile(logits, vb, m, l, acc, feature_major=False):
    m_new = jnp.maximum(m, jnp.max(logits, axis=1, keepdims=True))
    alpha = jnp.exp(m - m_new)
    p = jnp.exp(logits - m_new)
    l = alpha * l + jnp.sum(p, axis=1, keepdims=True)
    pb = p.astype(BF16)
    acc = alpha * acc + (_dot_nt(pb, vb) if feature_major else _dot(pb, vb))
    return m_new, l, acc


def _stack_heads(q, n_heads):
    t, width = q.shape
    rep = jnp.concatenate([q] * n_heads, axis=0) if n_heads > 1 else q
    row = lax.broadcasted_iota(jnp.int32, (n_heads * t, width), 0)
    lane = lax.broadcasted_iota(jnp.int32, (n_heads * t, width), 1)
    lo = (row // t) * HEAD_DIM
    keep = (lane >= lo) & (lane < lo + HEAD_DIM)
    return jnp.where(keep, rep, jnp.zeros_like(rep))


def _unstack_heads(acc, n_heads):
    rows, width = acc.shape
    t = rows // n_heads
    lane = lax.broadcasted_iota(jnp.int32, (t, width), 1)
    out = jnp.zeros((t, width), acc.dtype)
    for h in range(n_heads):
        sel = (lane >= h * HEAD_DIM) & (lane < (h + 1) * HEAD_DIM)
        out = jnp.where(sel, acc[h * t:(h + 1) * t, :], out)
    return out


def _row_query_index(rows, cols, t):
    r = lax.broadcasted_iota(jnp.int32, (rows, cols), 0)
    c = lax.broadcasted_iota(jnp.int32, (rows, cols), 1)
    return r % t, c


def _stick_prompt_kernel(q_ref, k_ref, v_ref, o_ref, carry_ref, acc_ref, *, t):
    i = pl.program_id(1)
    nh = HEADS_PER_SLAB
    n_slabs = q_ref.shape[1] // LANES
    rows = n_slabs * nh * t
    q = q_ref[...]
    qrows = [_stack_heads(q[:, s * LANES:(s + 1) * LANES], nh) for s in range(n_slabs)]
    nsuf = _neg_suffix_matrix(t)
    tq_idx, c_idx = _row_query_index(rows, t, t)

    def block(j):
        s = pl.multiple_of(j * t, t)
        return k_ref[pl.ds(s, t), :], v_ref[pl.ds(s, t), :]

    def diagonal():
        return _stick_tile(qrows, *block(i), nsuf, jnp.zeros((rows, 1), F32),
                           jnp.zeros((rows, LANES), F32), c_idx < tq_idx)

    @pl.when(i == 0)
    def _():
        carry, acc = diagonal()
        carry_ref[...] = carry
        acc_ref[...] = acc

    @pl.when(i > 0)
    def _():
        carry, acc = diagonal()
        carry, acc = _stick_tile(qrows, *block(i - 1), nsuf, carry, acc, None)
        carry_ref[...] = carry
        acc_ref[...] = acc

    def cond(state):
        j, carry_max = state
        return jnp.logical_and(j >= 0, carry_max > EXP_ZERO_BELOW)

    def body(state):
        j, _ = state
        carry, acc = _stick_tile(qrows, *block(j), nsuf, carry_ref[...], acc_ref[...], None)
        carry_ref[...] = carry
        acc_ref[...] = acc
        return j - 1, jnp.max(carry)

    lax.while_loop(cond, body, (i - 2, jnp.max(carry_ref[...])))
    acc = acc_ref[...]
    outs = [_unstack_heads(acc[s * nh * t:(s + 1) * nh * t], nh) for s in range(n_slabs)]
    o_ref[...] = (outs[0] if n_slabs == 1 else jnp.concatenate(outs, axis=1)).astype(o_ref.dtype)


def _stick_prompt(q, k, v, n, t=256, slabs_per_step=2):
    width = q.shape[1]
    t = min(t, n)
    group = slabs_per_step * LANES
    assert width % group == 0
    rows = slabs_per_step * HEADS_PER_SLAB * t
    return pl.pallas_call(
        functools.partial(_stick_prompt_kernel, t=t),
        grid=(width // group, n // t),
        in_specs=[pl.BlockSpec((t, group), lambda p, i: (i, p)),
                  pl.BlockSpec((n, group), lambda p, i: (0, p)),
                  pl.BlockSpec((n, group), lambda p, i: (0, p))],
        out_specs=pl.BlockSpec((t, group), lambda p, i: (i, p)),
        out_shape=jax.ShapeDtypeStruct((n, width), BF16),
        scratch_shapes=[pltpu.VMEM((rows, 1), F32), pltpu.VMEM((rows, LANES), F32)],
        compiler_params=_cparams("parallel", "parallel"),
        name="stick_prompt",
    )(q, k, v)


def _stick_sample_kernel(q_ref, kn_ref, vn_ref, ck_ref, cv_ref, nsuf_ref, o_ref, qrows_ref, carry_ref, acc_ref,
                         *, nh):
    jj = pl.program_id(1)
    t = q_ref.shape[1]

    @pl.when(jj == 0)
    def _():
        qrows = _stack_heads(q_ref[0], nh)
        qrows_ref[...] = qrows
        tq_idx, c_idx = _row_query_index(nh * t, t, t)
        carry, acc = _stick_tile(qrows, kn_ref[0], vn_ref[0], _neg_suffix_matrix(t),
                                 jnp.zeros((nh * t, 1), F32), jnp.zeros(acc_ref.shape, F32),
                                 c_idx < tq_idx)
        carry_ref[...] = carry
        acc_ref[...] = acc

    @pl.when(jnp.max(carry_ref[...]) > EXP_ZERO_BELOW)
    def _():
        carry, acc = _stick_tile(qrows_ref[...], ck_ref[...].astype(BF16), cv_ref[...].astype(BF16),
                                 nsuf_ref[...], carry_ref[...], acc_ref[...], None, feature_major=True)
        carry_ref[...] = carry
        acc_ref[...] = acc

    @pl.when(jj == pl.num_programs(1) - 1)
    def _():
        o_ref[0] = _unstack_heads(acc_ref[...], nh).astype(o_ref.dtype)


def _feature_major(cache):
    layers, b, past, heads, hd = cache.shape
    return jnp.transpose(cache, (0, 1, 3, 4, 2)).reshape(layers, b, heads * hd, past)


def _stick_sample(q, kn, vn, cache_kt, cache_vt, layer, tk=1024):
    b, t, width = q.shape
    past = cache_kt.shape[3]
    tk = min(tk, past)
    nblk = past // tk
    nh = width // HEAD_DIM
    r = jnp.arange(tk)
    nsuf = jnp.where(r[:, None] >= r[None, :], -1.0, 0.0).astype(BF16)
    new_spec = pl.BlockSpec((1, t, width), lambda bi, jj: (bi, 0, 0))
    cache_spec = pl.BlockSpec((None, None, width, tk), lambda bi, jj: (layer, bi, 0, nblk - 1 - jj))
    return pl.pallas_call(
        functools.partial(_stick_sample_kernel, nh=nh),
        grid=(b, nblk),
        in_specs=[new_spec, new_spec, new_spec, cache_spec, cache_spec,
                  pl.BlockSpec((tk, tk), lambda bi, jj: (0, 0))],
        out_specs=pl.BlockSpec((1, t, width), lambda bi, jj: (bi, 0, 0)),
        out_shape=jax.ShapeDtypeStruct((b, t, width), BF16),
        scratch_shapes=[pltpu.VMEM((nh * t, width), BF16),
                        pltpu.VMEM((nh * t, 1), F32),
                        pltpu.VMEM((nh * t, width), F32)],
        compiler_params=_cparams("parallel", "arbitrary"),
        name="stick_sample",
    )(q, kn, vn, cache_kt, cache_vt, nsuf)


def _band_prompt_kernel(q_ref, k0_ref, k1_ref, k2_ref, v0_ref, v1_ref, v2_ref, bias_ref, o_ref,
                        *, t, nprev, fixed_shift):
    i = pl.program_id(1)
    nh = HEADS_PER_SLAB
    n_slabs = q_ref.shape[1] // LANES
    q = q_ref[...]
    krefs = (k0_ref, k1_ref, k2_ref)[3 - (nprev + 1):]
    vrefs = (v0_ref, v1_ref, v2_ref)[3 - (nprev + 1):]
    kcat = jnp.concatenate([r[...] for r in krefs], axis=0)
    vcat = jnp.concatenate([r[...] for r in vrefs], axis=0)
    outs = []
    for s in range(n_slabs):
        lanes = slice(s * LANES, (s + 1) * LANES)
        qrows = _stack_heads(q[:, lanes], nh)
        logits = _dot_nt(qrows, kcat[:, lanes]) + bias_ref[s]
        c_idx = lax.broadcasted_iota(jnp.int32, logits.shape, 1)
        logits = jnp.where(c_idx >= (nprev - i) * t, logits, MASKED)
        vs = vcat[:, lanes]
        if fixed_shift:
            v_ones = jnp.concatenate([vs, jnp.ones(vs.shape, BF16)], axis=1)
            acc = _dot(jnp.exp(logits).astype(BF16), v_ones)
            acc = acc[:, :LANES] / acc[:, LANES:]
        else:
            m = jnp.max(logits, axis=1, keepdims=True)
            p = jnp.exp(logits - m)
            l = jnp.sum(p, axis=1, keepdims=True)
            acc = _dot(p.astype(BF16), vs) / l
        outs.append(_unstack_heads(acc, nh))
    o_ref[...] = (outs[0] if n_slabs == 1 else jnp.concatenate(outs, axis=1)).astype(o_ref.dtype)


def _rel_bias(rel_table, nq, nk, q0, k0):
    m_len = nq + nk - 1
    u = np.arange(m_len)
    d = np.where(u < nk, (q0 - k0) - u, (q0 - k0) + (m_len - u))
    g = jnp.take(rel_table.astype(F32), np.clip(d, -REL_CLIP, REL_CLIP) + REL_CLIP, axis=1)
    n_heads = g.shape[0]
    bias = jnp.tile(g, (1, nq))[:, :nq * (m_len - 1)].reshape(n_heads, nq, m_len - 1)[:, :, :nk]
    qp = q0 + np.arange(nq)[:, None]
    kp = k0 + np.arange(nk)[None, :]
    qc, kc = qp // CHUNK, kp // CHUNK
    visible = (kp >= 0) & (kc <= qc) & (kc >= qc - BAND_CHUNKS)
    return bias, jnp.asarray(visible)[None]


def _band_prompt(q, k, v, rel_table, n, logit_bound, t=256):
    width = q.shape[1]
    t = min(t, n)
    assert t % CHUNK == 0
    nprev = min(-(-BAND // t), 2)
    assert nprev * t >= BAND
    slabs = width // LANES
    bias, visible = _rel_bias(rel_table, t, (nprev + 1) * t, nprev * t, 0)
    bias_shape = (slabs, HEADS_PER_SLAB * t, (nprev + 1) * t)
    per_step = next(c for c in (4, 2, 1) if slabs % c == 0)
    group = per_step * LANES

    def kv_spec(back):
        return pl.BlockSpec((t, group), lambda p, i: (jnp.maximum(i - back, 0), p))

    def call(masked_bias, fixed_shift):
        return pl.pallas_call(
            functools.partial(_band_prompt_kernel, t=t, nprev=nprev, fixed_shift=fixed_shift),
            grid=(slabs // per_step, n // t),
            in_specs=[pl.BlockSpec((t, group), lambda p, i: (i, p)),
                      kv_spec(2), kv_spec(1), kv_spec(0), kv_spec(2), kv_spec(1), kv_spec(0),
                      pl.BlockSpec((per_step,) + bias_shape[1:], lambda p, i: (p, 0, 0))],
            out_specs=pl.BlockSpec((t, group), lambda p, i: (i, p)),
            out_shape=jax.ShapeDtypeStruct((n, width), BF16),
            compiler_params=_cparams("parallel", "parallel"),
            name="band_prompt" if fixed_shift else "band_prompt_row_max",
        )(q, k, k, k, v, v, v, masked_bias.reshape(bias_shape))

    tab_max, tab_min = jnp.max(rel_table), jnp.min(rel_table)
    shift = logit_bound + tab_max
    spread = 2.0 * logit_bound + (tab_max - tab_min)
    return lax.cond(spread < 2.0 * MAX_FIXED_SHIFT,
                    lambda: call(jnp.where(visible, bias - shift, MASKED), True),
                    lambda: call(jnp.where(visible, bias, MASKED), False))


def _band_sample_kernel(q_ref, kn_ref, vn_ref, ck_ref, cv_ref, bc_ref, bn_ref, o_ref, *, nh):
    qrows = _stack_heads(q_ref[0], nh)
    lc = _dot(qrows, ck_ref[...].astype(BF16)) + bc_ref[...]
    ln = _dot_nt(qrows, kn_ref[0]) + bn_ref[...]
    m = jnp.maximum(jnp.max(lc, axis=1, keepdims=True), jnp.max(ln, axis=1, keepdims=True))
    pc = jnp.exp(lc - m)
    pn = jnp.exp(ln - m)
    l = jnp.sum(pc, axis=1, keepdims=True) + jnp.sum(pn, axis=1, keepdims=True)
    acc = (_dot_nt(pc.astype(BF16), cv_ref[...].astype(BF16)) + _dot(pn.astype(BF16), vn_ref[0])) / l
    o_ref[0] = _unstack_heads(acc, nh).astype(o_ref.dtype)


def _band_sample(q, kn, vn, cache_kt, cache_vt, layer, rel_table, past_len):
    b, t, width = q.shape
    buf_len = cache_kt.shape[3]
    nh = width // HEAD_DIM
    def masked_bias(nk, k0):
        bias, visible = _rel_bias(rel_table, t, nk, past_len, k0)
        return jnp.where(visible, bias, MASKED).reshape(nh * t, nk)

    bias_c = masked_bias(buf_len, past_len - buf_len)
    bias_n = masked_bias(t, past_len)
    new_spec = pl.BlockSpec((1, t, width), lambda bi: (bi, 0, 0))
    cache_spec = pl.BlockSpec((None, None, width, buf_len), lambda bi: (layer, bi, 0, 0))
    return pl.pallas_call(
        functools.partial(_band_sample_kernel, nh=nh),
        grid=(b,),
        in_specs=[new_spec, new_spec, new_spec, cache_spec, cache_spec,
                  _full_spec(bias_c.shape), _full_spec(bias_n.shape)],
        out_specs=pl.BlockSpec((1, t, width), lambda bi: (bi, 0, 0)),
        out_shape=jax.ShapeDtypeStruct((b, t, width), BF16),
        compiler_params=_cparams("parallel"),
        name="band_sample",
    )(q, kn, vn, cache_kt, cache_vt, bias_c, bias_n)


def _key_decay_rows(dk, t):
    n_heads, tk = dk.shape
    return jnp.concatenate([jnp.broadcast_to(dk[h:h + 1, :], (t, tk)) for h in range(n_heads)], axis=0)


def _fox_prompt_kernel(q_ref, dq_ref, k_ref, v_ref, dk_ref, o_ref, m_ref, l_ref, acc_ref, *, t):
    i = pl.program_id(1)
    nh = HEADS_PER_SLAB
    qrows = _stack_heads(q_ref[...], nh)
    dq = jnp.concatenate([dq_ref[:, h:h + 1] for h in range(nh)], axis=0)
    m_ref[...] = jnp.full(m_ref.shape, MASKED, F32)
    l_ref[...] = jnp.zeros(l_ref.shape, F32)
    acc_ref[...] = jnp.zeros(acc_ref.shape, F32)

    def tile(j, mask):
        s = pl.multiple_of(j * t, t)
        z = _dot_nt(qrows, k_ref[pl.ds(s, t), :])
        logits = (z + dq) - _key_decay_rows(dk_ref[:, pl.ds(s, t)], t)
        if mask is not None:
            logits = jnp.where(mask, logits, MASKED)
        m, l, acc = _softmax_tile(logits, v_ref[pl.ds(s, t), :], m_ref[...], l_ref[...], acc_ref[...])
        m_ref[...] = m
        l_ref[...] = l
        acc_ref[...] = acc

    def body(j, carry):
        tile(j, None)
        return carry

    lax.fori_loop(0, i, body, 0)
    tq_idx, c_idx = _row_query_index(nh * t, t, t)
    tile(i, c_idx <= tq_idx)
    o_ref[...] = _unstack_heads(acc_ref[...] / l_ref[...], nh).astype(o_ref.dtype)


def _fox_prompt_bounded_kernel(shift_ref, first_ref, q_ref, dq_ref, k_ref, v_ref, dk_ref, o_ref,
                               dqrep_ref, acc_ref, z_ref, *, tq, tk):
    slab = pl.program_id(0)
    i = pl.program_id(1)
    ratio = tq // tk
    q = q_ref[...]
    ones = jnp.ones((tk, LANES), BF16)
    row = lax.broadcasted_iota(jnp.int32, (tq, LANES), 0)
    lane = lax.broadcasted_iota(jnp.int32, (tq, LANES), 1)
    outs = []
    for hh in range(HEADS_PER_SLAB):
        qm = jnp.where((lane >= hh * HEAD_DIM) & (lane < (hh + 1) * HEAD_DIM), q, jnp.zeros_like(q))
        dqrep_ref[...] = jnp.broadcast_to(dq_ref[:, hh:hh + 1] - shift_ref[0, 0], (tq, LANES))
        acc_ref[...] = jnp.zeros(acc_ref.shape, F32)

        def scores(j, r0, qm=qm):
            s = pl.multiple_of(j * tk, tk)
            return _dot_nt(qm[r0:], k_ref[pl.ds(s, tk), :])

        def tile(j, r0, causal, next_r0, hh=hh, scores=scores):
            s = pl.multiple_of(j * tk, tk)
            z = z_ref[r0:, :]
            dqh = dqrep_ref[r0:, :]
            cols = []
            for c in range(tk // LANES):
                lg = (z[:, c * LANES:(c + 1) * LANES] + dqh) - dk_ref[hh:hh + 1, pl.ds(s + c * LANES, LANES)]
                if causal:
                    key_pos = s + c * LANES + lax.broadcasted_iota(jnp.int32, lg.shape, 1)
                    query_pos = i * tq + r0 + lax.broadcasted_iota(jnp.int32, lg.shape, 0)
                    lg = jnp.where(key_pos <= query_pos, lg, MASKED)
                cols.append(jnp.exp(lg).astype(BF16))
            if next_r0 is not None:
                z_next = scores(j + 1, next_r0)
            v_ones = jnp.concatenate([v_ref[pl.ds(s, tk), :], ones], axis=1)
            acc_ref[r0:, :] += _dot(jnp.concatenate(cols, axis=1), v_ones)
            if next_r0 is not None:
                z_ref[next_r0:, :] = z_next

        start = first_ref[slab * HEADS_PER_SLAB + hh, i]
        z_ref[...] = scores(start, 0)

        count = i * ratio - start

        def run(j, n_tiles, tile=tile):
            for u in range(n_tiles):
                tile(j + u, 0, False, 0)

        first = start
        for size in (1, 2, 4):
            @pl.when((count & size) == size)
            def _(run=run, first=first, size=size):
                run(first, size)

            first = first + (count & size)

        def oct_body(jj, carry, run=run, first=first):
            run(first + 8 * jj, 8)
            return carry

        lax.fori_loop(0, lax.shift_right_logical(count, 3), oct_body, 0)
        for d in range(ratio):
            tile(i * ratio + d, d * tk, True, (d + 1) * tk if d + 1 < ratio else None)
        acc = acc_ref[...]
        outs.append(acc[:, :LANES] / acc[:, LANES:])
    o_ref[...] = jnp.where(lane < HEAD_DIM, outs[0], outs[1]).astype(o_ref.dtype)


MAX_FIXED_SHIFT = 30.0


def _fox_prompt(q, k, v, cum, n, logit_bound, t_exact=256, tq_bounded=1024, tk_bounded=512):
    width = q.shape[1]
    slabs = width // LANES
    nh = HEADS_PER_SLAB
    dq = cum.reshape(n, slabs, nh).transpose(1, 0, 2)
    dk = cum.reshape(n, slabs, nh).transpose(1, 2, 0)

    def specs(t):
        return ([pl.BlockSpec((t, LANES), lambda p, i: (i, p)),
                 pl.BlockSpec((None, t, nh), lambda p, i: (p, i, 0)),
                 pl.BlockSpec((n, LANES), lambda p, i: (0, p)),
                 pl.BlockSpec((n, LANES), lambda p, i: (0, p)),
                 pl.BlockSpec((None, nh, n), lambda p, i: (p, 0, 0))],
                pl.BlockSpec((t, LANES), lambda p, i: (i, p)))

    def exact(_):
        t = min(t_exact, n)
        in_specs, out_spec = specs(t)
        return pl.pallas_call(
            functools.partial(_fox_prompt_kernel, t=t),
            grid=(slabs, n // t),
            in_specs=in_specs, out_specs=out_spec,
            out_shape=jax.ShapeDtypeStruct((n, width), BF16),
            scratch_shapes=[pltpu.VMEM((nh * t, 1), F32), pltpu.VMEM((nh * t, 1), F32),
                            pltpu.VMEM((nh * t, LANES), F32)],
            compiler_params=_cparams("parallel", "parallel"),
            name="fox_prompt_running_max",
        )(q, dq, k, v, dk)

    def bounded(shift):
        tq = min(tq_bounded, n)
        tk = min(tk_bounded, n)
        in_specs, out_spec = specs(tq)
        cum_t = cum.T
        decay = cum_t[:, ::tq][:, :, None] - cum_t[:, tk - 1::tk][:, None, :]
        before = jnp.arange(n // tk)[None, None, :] < (jnp.arange(n // tq) * (tq // tk))[None, :, None]
        dead = jnp.logical_and(decay < EXP_ZERO_BELOW, before)
        first = jnp.sum(jnp.cumprod(dead.astype(jnp.int32), axis=2), axis=2).astype(jnp.int32)
        smem = pl.BlockSpec(memory_space=pltpu.SMEM)
        return pl.pallas_call(
            functools.partial(_fox_prompt_bounded_kernel, tq=tq, tk=tk),
            grid=(slabs, n // tq),
            in_specs=[smem, smem] + in_specs, out_specs=out_spec,
            out_shape=jax.ShapeDtypeStruct((n, width), BF16),
            scratch_shapes=[pltpu.VMEM((tq, LANES), F32), pltpu.VMEM((tq, 2 * LANES), F32),
                            pltpu.VMEM((tq, tk), F32)],
            compiler_params=_cparams("parallel", "parallel"),
            name="fox_prompt",
        )(shift.reshape(1, 1), first, q, dq, k, v, dk)

    return lax.cond(logit_bound < MAX_FIXED_SHIFT, bounded, exact, logit_bound)


def _fox_sample_kernel(q_ref, dq_ref, kn_ref, vn_ref, dkn_ref, ck_ref, cv_ref, dkc_ref, o_ref,
                       qrows_ref, m_ref, l_ref, acc_ref, *, nh):
    jj = pl.program_id(1)
    t = q_ref.shape[1]
    dq = dq_ref[0]

    @pl.when(jj == 0)
    def _():
        qrows = _stack_heads(q_ref[0], nh)
        qrows_ref[...] = qrows
        tq_idx, c_idx = _row_query_index(nh * t, t, t)
        logits = (_dot_nt(qrows, kn_ref[0]) + dq) - _key_decay_rows(dkn_ref[0], t)
        logits = jnp.where(c_idx <= tq_idx, logits, MASKED)
        m, l, acc = _softmax_tile(logits, vn_ref[0], jnp.full(m_ref.shape, MASKED, F32),
                                  jnp.zeros(l_ref.shape, F32), jnp.zeros(acc_ref.shape, F32))
        m_ref[...] = m
        l_ref[...] = l
        acc_ref[...] = acc

    logits = (_dot(qrows_ref[...], ck_ref[...].astype(BF16)) + dq) - _key_decay_rows(dkc_ref[0], t)
    m, l, acc = _softmax_tile(logits, cv_ref[...].astype(BF16), m_ref[...], l_ref[...], acc_ref[...],
                              feature_major=True)
    m_ref[...] = m
    l_ref[...] = l
    acc_ref[...] = acc

    @pl.when(jj == pl.num_programs(1) - 1)
    def _():
        o_ref[0] = _unstack_heads(acc_ref[...] / l_ref[...], nh).astype(o_ref.dtype)


def _fox_sample(q, kn, vn, cache_kt, cache_vt, layer, cum_cache, cum_new, tk=1024):
    b, t, width = q.shape
    past = cache_kt.shape[3]
    tk = min(tk, past)
    nblk = past // tk
    nh = width // HEAD_DIM
    dq = cum_new.reshape(b, nh * t, 1)
    new_spec = pl.BlockSpec((1, t, width), lambda bi, jj: (bi, 0, 0))
    cache_spec = pl.BlockSpec((None, None, width, tk), lambda bi, jj: (layer, bi, 0, jj))
    return pl.pallas_call(
        functools.partial(_fox_sample_kernel, nh=nh),
        grid=(b, nblk),
        in_specs=[new_spec, pl.BlockSpec((1, nh * t, 1), lambda bi, jj: (bi, 0, 0)),
                  new_spec, new_spec, pl.BlockSpec((1, nh, t), lambda bi, jj: (bi, 0, 0)),
                  cache_spec, cache_spec, pl.BlockSpec((1, nh, tk), lambda bi, jj: (bi, 0, jj))],
        out_specs=pl.BlockSpec((1, t, width), lambda bi, jj: (bi, 0, 0)),
        out_shape=jax.ShapeDtypeStruct((b, t, width), BF16),
        scratch_shapes=[pltpu.VMEM((nh * t, width), BF16), pltpu.VMEM((nh * t, 1), F32),
                        pltpu.VMEM((nh * t, 1), F32), pltpu.VMEM((nh * t, width), F32)],
        compiler_params=_cparams("parallel", "arbitrary"),
        name="fox_sample",
    )(q, dq, kn, vn, cum_new, cache_kt, cache_vt, cum_cache)


def _pad_to(x, multiple, axis):
    size = x.shape[axis]
    target = -(-size // multiple) * multiple
    if target == size:
        return x
    pad = [(0, 0)] * x.ndim
    pad[axis] = (0, target - size)
    return jnp.pad(x, pad)


def kernel(x_prompt, x_sample, cache_a_k, cache_a_v, cache_b_k, cache_b_v, cache_c_k, cache_c_v, cache_c_logf, p_prompt, p_sample, g_mix, w_in_even, g_qb, g_kb, rel_bias, w_in_odd, b_forget, g_qc, g_kc, w_out, g_mlp, w_ff1, w_ff2, g_ple, w_ple_gate, w_ple_proj):
    bp, n, d = x_prompt.shape
    bs, ts, _ = x_sample.shape
    assert bp == 1, "prompt kernels assume a single prompt stream"
    depth = g_mix.shape[0]
    past = cache_a_k.shape[2]
    buf_len = cache_b_k.shape[2]
    ha, hb, hc = cache_a_k.shape[3], cache_b_k.shape[3], cache_c_k.shape[3]
    wa, wb, wc = ha * HEAD_DIM, hb * HEAD_DIM, hc * HEAD_DIM
    rows_s = bs * ts

    blk = jnp.arange(NORM_SLAB) // HEAD_DIM
    gmat = jnp.where(blk[:, None] == blk[None, :], 1.0 / HEAD_DIM, 0.0).astype(BF16)

    cache_a_kt, cache_a_vt = _feature_major(cache_a_k), _feature_major(cache_a_v)
    cache_b_kt, cache_b_vt = _feature_major(cache_b_k), _feature_major(cache_b_v)
    cache_c_kt, cache_c_vt = _feature_major(cache_c_k), _feature_major(cache_c_v)

    rows = n + rows_s
    tail = -rows % ROW_TILE
    h = jnp.concatenate([x_prompt.reshape(n, d), x_sample.reshape(rows_s, d), jnp.zeros((tail, d), F32)], axis=0)
    sample_pad = rows + tail - n
    d_ple = p_prompt.shape[-1]
    p_prompt2 = p_prompt.reshape(depth, n, d_ple)
    p_sample2 = _pad_to(p_sample.reshape(depth, rows_s, d_ple), sample_pad, 1)
    w_even16 = w_in_even.astype(BF16)
    w_odd16 = _pad_to(w_in_odd, LANES, 2).astype(BF16)
    w_out16, w_ff1_16, w_ff2_16 = w_out.astype(BF16), w_ff1.astype(BF16), w_ff2.astype(BF16)
    w_gate16, w_proj16 = w_ple_gate.astype(BF16), w_ple_proj.astype(BF16)

    def prompt_state(per_layer, heads, lo=0):
        xt = jnp.concatenate([x[:, :, lo:] for x in per_layer], axis=0)
        return jnp.transpose(xt.reshape(len(per_layer), heads, HEAD_DIM, n - lo), (0, 3, 1, 2))[:, None]

    def sample_state(xt, heads):
        return jnp.transpose(xt[:, :rows_s].reshape(heads, HEAD_DIM, bs, ts), (2, 3, 0, 1))

    st = {name: [] for name in ("pc_f", "sa_k", "sa_v", "sb_k", "sb_v", "sc_k", "sc_v", "sc_f")}
    even_states, odd_states = [], []
    for i in range(depth):
        j = i // 2
        g_row = g_mix[i].reshape(1, d)
        if i % 2 == 0:
            gq_row = jnp.tile(g_qb[j], hb).reshape(1, wb)
            gk_row = jnp.tile(g_kb[j], hb).reshape(1, wb)
            outs = _proj_even(h, g_row, w_even16, j, gq_row, gk_row, gmat, wa, wb, n)
            qa, ka16, va16, qb, kb16, vb16 = outs[:6]
            even_states.append(outs[6:10])
            ka_s, va_s, kb_s, vb_s = outs[10:]
            mix_a_p = _stick_prompt(qa, ka16, va16, n)
            band_bound = 1.01 * HEAD_DIM * SCALE * jnp.max(jnp.abs(g_qb[j])) * jnp.max(jnp.abs(g_kb[j]))
            mix_b_p = _band_prompt(qb, kb16, vb16, rel_bias[j], n, band_bound.astype(F32))
            s3 = lambda a: a[n:rows].reshape(bs, ts, a.shape[1])
            mix_a_s = _stick_sample(s3(qa), s3(ka16), s3(va16), cache_a_kt, cache_a_vt, j)
            mix_b_s = _band_sample(s3(qb), s3(kb16), s3(vb16), cache_b_kt, cache_b_vt, j, rel_bias[j], past)
            mix_prompt = [mix_a_p, mix_b_p]
            mix_sample = _pad_to(jnp.concatenate([mix_a_s.reshape(rows_s, wa), mix_b_s.reshape(rows_s, wb)],
                                                 axis=1), sample_pad, 0)
            st["sa_k"].append(sample_state(ka_s, ha))
            st["sa_v"].append(sample_state(va_s, ha))
            st["sb_k"].append(sample_state(kb_s, hb))
            st["sb_v"].append(sample_state(vb_s, hb))
        else:
            gq_row = jnp.tile(g_qc[j], hc).reshape(1, wc)
            gk_row = jnp.tile(g_kc[j], hc).reshape(1, wc)
            bf_row = _pad_to(b_forget[j].reshape(1, hc), LANES, 1)
            outs = _proj_odd(h, g_row, w_odd16, j, gq_row, gk_row, gmat, bf_row, wc, n)
            q, k16, v16, lf = outs[:4]
            odd_states.append(outs[4:6])
            k_s, v_s = outs[6:]
            log_f = lf[:, :hc]
            cum_p = _cumsum_lanes(log_f[:n].T)
            logit_bound = 1.01 * HEAD_DIM * SCALE * jnp.max(jnp.abs(g_qc[j])) * jnp.max(jnp.abs(g_kc[j]))
            mix_p = _fox_prompt(q, k16, v16, cum_p.T, n, logit_bound.astype(F32))
            lf_new = log_f[n:rows].reshape(bs, ts, hc).transpose(0, 2, 1)
            lf_all = jnp.concatenate([cache_c_logf[j].transpose(0, 2, 1), lf_new], axis=2)
            cum_s = _cumsum_lanes(lf_all.reshape(bs * hc, past + ts)).reshape(bs, hc, past + ts)
            s3 = lambda a: a[n:rows].reshape(bs, ts, a.shape[1])
            mix_s = _fox_sample(s3(q), s3(k16), s3(v16), cache_c_kt, cache_c_vt, j,
                                cum_s[:, :, :past], cum_s[:, :, past:past + ts])
            mix_prompt = [mix_p]
            mix_sample = _pad_to(mix_s.reshape(rows_s, wc), sample_pad, 0)
            st["pc_f"].append(log_f[:n].reshape(1, n, hc))
            st["sc_k"].append(sample_state(k_s, hc))
            st["sc_v"].append(sample_state(v_s, hc))
            st["sc_f"].append(log_f[n:rows].reshape(bs, ts, hc))
        h = _post(h, mix_prompt, mix_sample, n, i, w_out16, g_mlp[i].reshape(1, d), w_ff1_16, w_ff2_16,
                  g_ple[i].reshape(1, d), w_gate16, p_prompt2, p_sample2, w_proj16, split_output=i == depth - 1)

    y_prompt = h[0].reshape(1, n, d)
    y_sample = h[1][:rows_s].reshape(bs, ts, d)
    stk = {name: jnp.stack(vals) for name, vals in st.items()}
    assert buf_len >= ts
    stk["sb_k"] = jnp.concatenate([cache_b_k[:, :, ts:], stk["sb_k"]], axis=2)
    stk["sb_v"] = jnp.concatenate([cache_b_v[:, :, ts:], stk["sb_v"]], axis=2)
    keep = min(BAND, n)
    ka_l, va_l, kb_l, vb_l = zip(*even_states)
    kc_l, vc_l = zip(*odd_states)
    stk["pa_k"], stk["pa_v"] = prompt_state(ka_l, ha), prompt_state(va_l, ha)
    stk["pb_k"], stk["pb_v"] = prompt_state(kb_l, hb, n - keep), prompt_state(vb_l, hb, n - keep)
    stk["pc_k"], stk["pc_v"] = prompt_state(kc_l, hc), prompt_state(vc_l, hc)
    return (y_prompt, y_sample, stk["pa_k"], stk["pa_v"], stk["pb_k"], stk["pb_v"],
            stk["pc_k"], stk["pc_v"], stk["pc_f"], stk["sa_k"], stk["sa_v"], stk["sb_k"], stk["sb_v"],
            stk["sc_k"], stk["sc_v"], stk["sc_f"])
```
